```python
import math
import jax
import jax.numpy as jnp
from jax import lax
import numpy as np

D_MODEL = 1024
BATCH = 8
SEQ = 2048
DEPTH = 2
DEC_BATCH = 8
DEC_SEQ = 16
PAST_LEN = 2048

CHUNK = 64
N_A = DEPTH // 2
N_B = DEPTH - N_A
HGRN_EXPAND = 128
A_WIDTH = D_MODEL
A_HEADS = A_WIDTH // HGRN_EXPAND
A_DK = HGRN_EXPAND
A_DV = A_WIDTH // A_HEADS
B_HEADS = 16
Q_LORA = 512
KV_LORA = 256
NOPE_DIM = 128
ROPE_DIM = 64
B_DV = 128
B_WIDTH = B_HEADS * B_DV
Q_BLOCK = 128
ROPE_BASE = 10000.0
EPS = 1e-6
NEG_INF = -1e30
ATTN_SCALE = (NOPE_DIM + ROPE_DIM) ** -0.5
ALPHA = (2 * DEPTH) ** 0.25
BETA = (8 * DEPTH) ** -0.25

kernel_name = 'hybrid_hgrn2_mla_yoco_stream_step'


def rms_norm(x, g):
    xf = x.astype(jnp.float32)
    y = xf * lax.rsqrt(jnp.mean(xf * xf, axis=-1, keepdims=True) + EPS)
    return (y * g.astype(jnp.float32)).astype(x.dtype)


def layer_norm(x, g, b):
    xf = x.astype(jnp.float32)
    mu = jnp.mean(xf, axis=-1, keepdims=True)
    var = jnp.mean(jnp.square(xf - mu), axis=-1, keepdims=True)
    y = (xf - mu) * lax.rsqrt(var + EPS) * g.astype(jnp.float32) + b.astype(jnp.float32)
    return y.astype(x.dtype)


def rope(x, pos):
    half = ROPE_DIM // 2
    inv = jnp.power(ROPE_BASE, -jnp.arange(half, dtype=jnp.float32) / half)
    ang = pos.astype(jnp.float32)[:, None] * inv[None, :]
    ang = ang.reshape((ang.shape[0],) + (1,) * (x.ndim - 3) + (half,))
    cos, sin = jnp.cos(ang), jnp.sin(ang)
    xf = x.astype(jnp.float32)
    x1, x2 = xf[..., :half], xf[..., half:]
    return jnp.concatenate([x1 * cos - x2 * sin, x1 * sin + x2 * cos], axis=-1).astype(x.dtype)


def gla_chunked(q, k, v, log_f, s0, chunk):
    bsz, t, h, dk = q.shape
    dv = v.shape[-1]
    n = t // chunk
    f32 = jnp.float32
    q, k, log_f = [a.astype(f32).reshape(bsz, n, chunk, h, dk) for a in (q, k, log_f)]
    v = v.astype(f32).reshape(bsz, n, chunk, h, dv)
    b = jnp.cumsum(log_f, axis=2)
    mid = chunk // 2
    b_mid = b[:, :, mid:mid + 1]
    q_in = q * jnp.exp(b - b_mid)
    k_in = k * jnp.exp(b_mid - b)
    causal = jnp.tril(jnp.ones((chunk, chunk), dtype=bool))
    scores = jnp.einsum('bnlhk,bnshk->bnhls', q_in, k_in)
    scores = jnp.where(causal, scores, 0.0)
    o_intra = jnp.einsum('bnhls,bnshv->bnlhv', scores, v)
    b_last = b[:, :, -1]
    chunk_state = jnp.einsum('bnshk,bnshv->bnhkv', k * jnp.exp(b_last[:, :, None] - b), v)
    decay = jnp.exp(b_last)

    def step(s, inp):
        d, cs = inp
        return d[..., None] * s + cs, s

    s_final, s_prev = lax.scan(step, s0.astype(f32),
                               (jnp.moveaxis(decay, 1, 0), jnp.moveaxis(chunk_state, 1, 0)))
    s_prev = jnp.moveaxis(s_prev, 0, 1)
    o_inter = jnp.einsum('bnlhk,bnhkv->bnlhv', q * jnp.exp(b), s_prev)
    o = (o_intra + o_inter).reshape(bsz, t, h, dv)
    return o, s_final


def mixer_hgrn2(x, w_in, lb, norm_g, w_out, s0, chunk):
    bsz, t, _ = x.shape
    proj = jnp.einsum('btd,de->bte', x, w_in)
    qa, za, ia, ga = jnp.split(proj, 4, axis=-1)
    shp = (bsz, t, A_HEADS, A_DK)
    lbh = lb.reshape(A_HEADS, A_DK)
    zf = za.astype(jnp.float32).reshape(shp)
    log_f = jnp.log(lbh + (1.0 - lbh) * jax.nn.sigmoid(zf))
    k = (1.0 - lbh) * jax.nn.sigmoid(-zf)
    o, s = gla_chunked(qa.reshape(shp), k, ia.reshape(bsz, t, A_HEADS, A_DV), log_f, s0, chunk)
    o = rms_norm(o, norm_g).reshape(bsz, t, A_WIDTH).astype(x.dtype) * jax.nn.silu(ga)
    return jnp.einsum('bte,ed->btd', o, w_out), s


def shared_latent_kv(x, pos, w_dkv, kv_norm_g):
    kv = jnp.einsum('btd,de->bte', x, w_dkv)
    c = rms_norm(kv[..., :KV_LORA], kv_norm_g)
    kr = rope(kv[..., KV_LORA:], pos)
    return c, kr


def mla_queries(x, pos, w_in, q_norm_g, w_uq, w_uk):
    bsz, t, _ = x.shape
    proj = jnp.einsum('btd,de->bte', x, w_in)
    cq = rms_norm(proj[..., :Q_LORA], q_norm_g)
    gate = proj[..., Q_LORA:]
    q = jnp.einsum('btr,re->bte', cq, w_uq).reshape(bsz, t, B_HEADS, NOPE_DIM + ROPE_DIM)
    q_lat = jnp.einsum('bthn,khn->bthk', q[..., :NOPE_DIM], w_uk)
    q_rope = rope(q[..., NOPE_DIM:], pos)
    return q_lat, q_rope, gate


def latent_attend(q_lat, q_rope, c, kr, w_uv, mask):
    s = (jnp.einsum('bqhk,bsk->bhqs', q_lat, c)
         + jnp.einsum('bqhr,bsr->bhqs', q_rope, kr)).astype(jnp.float32) * ATTN_SCALE
    if mask is not None:
        s = jnp.where(mask, s, NEG_INF)
    p = jax.nn.softmax(s, axis=-1).astype(c.dtype)
    o_lat = jnp.einsum('bhqs,bsk->bqhk', p, c)
    return jnp.einsum('bqhk,khv->bqhv', o_lat, w_uv)


def attend_prompt(q_lat, q_rope, c, kr, w_uv):
    bsz, t = q_lat.shape[:2]
    nb = t // Q_BLOCK
    key_chunk = jnp.arange(t, dtype=jnp.int32) // CHUNK

    def blk(args):
        ql, qr, start = args
        q_chunk = (start + jnp.arange(Q_BLOCK, dtype=jnp.int32)) // CHUNK
        mask = key_chunk[None, :] <= q_chunk[:, None]
        return latent_attend(ql, qr, c, kr, w_uv, mask[None, None])

    to_blocks = lambda a: jnp.moveaxis(a.reshape((bsz, nb, Q_BLOCK) + a.shape[2:]), 1, 0)
    o = lax.map(blk, (to_blocks(q_lat), to_blocks(q_rope), jnp.arange(nb, dtype=jnp.int32) * Q_BLOCK))
    return jnp.moveaxis(o, 0, 1).reshape(bsz, t, B_WIDTH)


def trunk(x, pos, s0, c_past, kr_past, chunk, p):
    bsz, t, _ = x.shape
    lb_all = jnp.cumsum(jax.nn.softmax(p['lb_gamma'].astype(jnp.float32), axis=0), axis=0)
    states = []
    c_new, kr_new = None, None
    for l in range(DEPTH):
        if l < N_A:
            out, s = mixer_hgrn2(x, p['w_in_a'][l], lb_all[l], p['a_norm_g'][l], p['w_out_a'][l], s0[l], chunk)
            states.append(s)
        else:
            if l == N_A:
                c_new, kr_new = shared_latent_kv(x, pos, p['w_dkv'], p['kv_norm_g'])
            j = l - N_A
            q_lat, q_rope, gate = mla_queries(x, pos, p['w_in_b'][j], p['q_norm_g'][j], p['w_uq'][j], p['w_uk'])
            if c_past is None:
                o = attend_prompt(q_lat, q_rope, c_new, kr_new, p['w_uv'])
            else:
                c_all = jnp.concatenate([c_past.astype(c_new.dtype), c_new], axis=1)
                kr_all = jnp.concatenate([kr_past.astype(kr_new.dtype), kr_new], axis=1)
                o = latent_attend(q_lat, q_rope, c_all, kr_all, p['w_uv'], None).reshape(bsz, t, B_WIDTH)
            out = jnp.einsum('bte,ed->btd', o * jax.nn.silu(gate), p['w_out_b'][j])
        x = layer_norm(ALPHA * x + out, p['ln_g'][l], p['ln_b'][l])
    return x, jnp.stack(states), c_new, kr_new


def setup_inputs(seed: int = 0) -> dict:
    key = jax.random.key(seed)
    ks = jax.random.split(key, 20)
    nrm = lambda k, shp, s: jax.random.normal(k, shp, jnp.float32) * s
    return {
        'x_prompt': nrm(ks[0], (BATCH, SEQ, D_MODEL), 1.0),
        'x_sample': nrm(ks[1], (DEC_BATCH, DEC_SEQ, D_MODEL), 1.0),
        'state_hgrn': nrm(ks[2], (N_A, DEC_BATCH, A_HEADS, A_DK, A_DV), 0.5),
        'cache_ckv': nrm(ks[3], (DEC_BATCH, PAST_LEN, KV_LORA), 1.0),
        'cache_krope': nrm(ks[4], (DEC_BATCH, PAST_LEN, ROPE_DIM), 1.0),
        'w_in_a': nrm(ks[5], (N_A, D_MODEL, 4 * A_WIDTH), D_MODEL ** -0.5),
        'lb_gamma': nrm(ks[6], (N_A + 1, A_WIDTH), 0.1),
        'a_norm_g': 1.0 + nrm(ks[7], (N_A, A_DV), 0.02),
        'w_out_a': nrm(ks[8], (N_A, A_WIDTH, D_MODEL), BETA * A_WIDTH ** -0.5),
        'w_dkv': nrm(ks[9], (D_MODEL, KV_LORA + ROPE_DIM), D_MODEL ** -0.5),
        'kv_norm_g': 1.0 + nrm(ks[10], (KV_LORA,), 0.02),
        'w_uk': nrm(ks[11], (KV_LORA, B_HEADS, NOPE_DIM), KV_LORA ** -0.5),
        'w_uv': nrm(ks[12], (KV_LORA, B_HEADS, B_DV), KV_LORA ** -0.5),
        'w_in_b': nrm(ks[13], (N_B, D_MODEL, Q_LORA + B_WIDTH), D_MODEL ** -0.5),
        'q_norm_g': 1.0 + nrm(ks[14], (N_B, Q_LORA), 0.02),
        'w_uq': nrm(ks[15], (N_B, Q_LORA, B_HEADS * (NOPE_DIM + ROPE_DIM)), Q_LORA ** -0.5),
        'w_out_b': nrm(ks[16], (N_B, B_WIDTH, D_MODEL), BETA * B_WIDTH ** -0.5),
        'ln_g': 1.0 + nrm(ks[17], (DEPTH, D_MODEL), 0.02),
        'ln_b': nrm(ks[18], (DEPTH, D_MODEL), 0.02),
    }


def reference(x_prompt, x_sample, state_hgrn, cache_ckv, cache_krope,
              w_in_a, lb_gamma, a_norm_g, w_out_a, w_dkv, kv_norm_g, w_uk, w_uv,
              w_in_b, q_norm_g, w_uq, w_out_b, ln_g, ln_b):
    p = dict(w_in_a=w_in_a, lb_gamma=lb_gamma, a_norm_g=a_norm_g, w_out_a=w_out_a,
             w_dkv=w_dkv, kv_norm_g=kv_norm_g, w_uk=w_uk, w_uv=w_uv,
             w_in_b=w_in_b, q_norm_g=q_norm_g, w_uq=w_uq, w_out_b=w_out_b,
             ln_g=ln_g, ln_b=ln_b)
    t_p = x_prompt.shape[1]
    pos_p = jnp.arange(t_p, dtype=jnp.int32)
    s0_p = jnp.zeros((N_A, x_prompt.shape[0], A_HEADS, A_DK, A_DV), jnp.float32)
    y_prompt, state_hgrn_prompt, ckv_prompt, krope_prompt = trunk(
        x_prompt, pos_p, s0_p, None, None, CHUNK, p)
    t_s = x_sample.shape[1]
    past = cache_ckv.shape[1]
    pos_s = past + jnp.arange(t_s, dtype=jnp.int32)
    y_sample, state_hgrn_sample, ckv_sample, krope_sample = trunk(
        x_sample, pos_s, state_hgrn, cache_ckv, cache_krope, t_s, p)
    return (y_prompt, y_sample, state_hgrn_prompt, ckv_prompt, krope_prompt,
            state_hgrn_sample, ckv_sample, krope_sample)
```

```python
import functools

import jax
import jax.numpy as jnp
from jax import lax
from jax.experimental import pallas as pl
from jax.experimental.pallas import tpu as pltpu

F32 = jnp.float32
BF16 = jnp.bfloat16

D_MODEL = 1024
DEPTH = 2
CHUNK = 64
CHUNK_SHIFT = 6
N_A = DEPTH // 2
A_HEADS = 8
A_DK = 128
A_DV = 128
A_WIDTH = A_HEADS * A_DV
B_HEADS = 16
Q_LORA = 512
KV_LORA = 256
NOPE_DIM = 128
ROPE_DIM = 64
B_DV = 128
B_WIDTH = B_HEADS * B_DV
ROPE_BASE = 10000.0
EPS = 1e-6
NEG_INF = -1e30
ATTN_SCALE = (NOPE_DIM + ROPE_DIM) ** -0.5
ALPHA = (2 * DEPTH) ** 0.25

LANES = 128
KCAT = KV_LORA + 2 * ROPE_DIM
VMEM_LIMIT = 56 * 1024 * 1024

_NT = (((1,), (1,)), ((), ()))
_TN = (((0,), (0,)), ((), ()))


def _dot(a, b):
    return jnp.dot(a, b, preferred_element_type=F32)


def _layer_norm(x, g, b):
    mu = jnp.mean(x, axis=-1, keepdims=True)
    xc = x - mu
    var = jnp.mean(xc * xc, axis=-1, keepdims=True)
    return xc * lax.rsqrt(var + EPS) * g + b


def _rms_norm(x, g):
    return x * lax.rsqrt(jnp.mean(x * x, axis=-1, keepdims=True) + EPS) * g


def _sigmoid_pair(z):
    e = jnp.exp(-jnp.abs(z))
    r = 1.0 / (1.0 + e)
    er = e * r
    pos = z >= 0
    return jnp.where(pos, r, er), jnp.where(pos, er, r)


def _split3(x):
    hi = x.astype(BF16)
    r1 = x - hi.astype(F32)
    mid = r1.astype(BF16)
    lo = (r1 - mid.astype(F32)).astype(BF16)
    return hi, mid, lo


def _hgrn_kernel(x_ref, s0_ref, w_in_ref, lbg_ref, ng_ref, w_out_ref, lng_ref, lnb_ref, tri_ref,
                 y_ref, sfin_ref, st_ref, o_ref, *, layer, bblk, tt, chunk, zero_init):
    t = pl.program_id(1)
    rows = bblk * tt
    mid = chunk // 2

    @pl.when(t == 0)
    def _():
        for bb in range(bblk):
            for h in range(A_HEADS):
                if zero_init:
                    st_ref[bb, h] = jnp.zeros((A_DV, A_DK), F32)
                else:
                    st_ref[bb, h] = s0_ref[bb, h].T

    x = x_ref[...].reshape(rows, D_MODEL)
    proj = _dot(x.astype(BF16), w_in_ref[...])
    q = proj[:, 0:A_WIDTH]
    z = proj[:, A_WIDTH:2 * A_WIDTH]
    v = proj[:, 2 * A_WIDTH:3 * A_WIDTH].astype(BF16)
    g = proj[:, 3 * A_WIDTH:4 * A_WIDTH]

    lbg = lbg_ref[...]
    e = jnp.exp(lbg - jnp.max(lbg, axis=0, keepdims=True))
    lb = jnp.sum(e[0:layer + 1], axis=0, keepdims=True) / jnp.sum(e, axis=0, keepdims=True)
    one_m_lb = 1.0 - lb

    sig_pos, sig_neg = _sigmoid_pair(z)
    logf = jnp.log(lb + one_m_lb * sig_pos)
    k = one_m_lb * sig_neg

    hi, m_, lo = _split3(logf)
    b = _dot(tri_ref[...], jnp.concatenate([hi, m_, lo], axis=0))

    silu_g = g * _sigmoid_pair(g)[0]
    ng = ng_ref[...]
    li = lax.broadcasted_iota(jnp.int32, (chunk, chunk), 0)
    si = lax.broadcasted_iota(jnp.int32, (chunk, chunk), 1)
    causal = li >= si

    for bb in range(bblk):
        for c in range(tt // chunk):
            r0 = bb * tt + c * chunk
            rs = slice(r0, r0 + chunk)
            bc = b[rs]
            bmid = bc[mid:mid + 1]
            blast = bc[chunk - 1:chunk]
            qc = q[rs]
            kc = k[rs]
            q_in = (qc * jnp.exp(bc - bmid)).astype(BF16)
            k_in = (kc * jnp.exp(bmid - bc)).astype(BF16)
            q_ex = (qc * jnp.exp(bc)).astype(BF16)
            k_dec = (kc * jnp.exp(blast - bc)).astype(BF16)
            decay = jnp.exp(blast)
            vc = v[rs]
            for h in range(A_HEADS):
                hs = slice(h * A_DK, (h + 1) * A_DK)
                st = st_ref[bb, h]
                sc = lax.dot_general(q_in[:, hs], k_in[:, hs], _NT, preferred_element_type=F32)
                sc = jnp.where(causal, sc, 0.0).astype(BF16)
                o = _dot(sc, vc[:, hs]) + lax.dot_general(
                    q_ex[:, hs], st.astype(BF16), _NT, preferred_element_type=F32)
                st_ref[bb, h] = st * decay[:, hs] + lax.dot_general(
                    vc[:, hs], k_dec[:, hs], _TN, preferred_element_type=F32)
                o = _rms_norm(o, ng) * silu_g[rs, hs]
                o_ref[rs, hs] = o.astype(BF16)

    out = _dot(o_ref[...], w_out_ref[...])
    y = _layer_norm(ALPHA * x + out, lng_ref[...], lnb_ref[...])
    y_ref[...] = y.reshape(bblk, tt, D_MODEL)

    @pl.when(t == pl.num_programs(1) - 1)
    def _():
        for bb in range(bblk):
            for h in range(A_HEADS):
                sfin_ref[bb, h] = st_ref[bb, h].T


def _hgrn_layer(x, s0, w_in, lb_gamma, norm_g, w_out, ln_g, ln_b, *, layer, bblk, tt, chunk):
    bsz, t, _ = x.shape
    rows = bblk * tt
    zero_init = s0 is None
    if zero_init:
        s0 = jnp.zeros((bblk, A_HEADS, 8, LANES), F32)
        s0_spec = pl.BlockSpec((bblk, A_HEADS, 8, LANES), lambda b, i: (0, 0, 0, 0))
    else:
        s0_spec = pl.BlockSpec((bblk, A_HEADS, A_DK, A_DV), lambda b, i: (b, 0, 0, 0))
    r = jnp.arange(rows)
    tri = ((r[:, None] // chunk == r[None, :] // chunk) & (r[:, None] >= r[None, :])).astype(BF16)
    tri3 = jnp.concatenate([tri, tri, tri], axis=1)
    const = lambda b, i: (0, 0)
    kern = functools.partial(_hgrn_kernel, layer=layer, bblk=bblk, tt=tt, chunk=chunk, zero_init=zero_init)
    return pl.pallas_call(
        kern,
        grid=(bsz // bblk, t // tt),
        in_specs=[
            pl.BlockSpec((bblk, tt, D_MODEL), lambda b, i: (b, i, 0)),
            s0_spec,
            pl.BlockSpec((D_MODEL, 4 * A_WIDTH), const),
            pl.BlockSpec(lb_gamma.shape, const),
            pl.BlockSpec((1, A_DV), const),
            pl.BlockSpec((A_WIDTH, D_MODEL), const),
            pl.BlockSpec((1, D_MODEL), const),
            pl.BlockSpec((1, D_MODEL), const),
            pl.BlockSpec((rows, 3 * rows), const),
        ],
        out_specs=[
            pl.BlockSpec((bblk, tt, D_MODEL), lambda b, i: (b, i, 0)),
            pl.BlockSpec((bblk, A_HEADS, A_DK, A_DV), lambda b, i: (b, 0, 0, 0)),
        ],
        out_shape=[
            jax.ShapeDtypeStruct((bsz, t, D_MODEL), F32),
            jax.ShapeDtypeStruct((bsz, A_HEADS, A_DK, A_DV), F32),
        ],
        scratch_shapes=[
            pltpu.VMEM((bblk, A_HEADS, A_DV, A_DK), F32),
            pltpu.VMEM((rows, A_WIDTH), BF16),
        ],
        compiler_params=pltpu.CompilerParams(
            dimension_semantics=("arbitrary", "arbitrary"), vmem_limit_bytes=VMEM_LIMIT),
        name=f"hgrn_layer_{'prompt' if zero_init else 'sample'}",
    )(x, s0, w_in, lb_gamma, norm_g, w_out, ln_g, ln_b, tri3)


def _mla_proj_kernel(x_ref, cos_ref, sin_ref, w_dkv_ref, kvg_ref, w_in_ref, qg_ref, w_uq_ref, w_ukt_ref,
                     ckv_ref, krope_ref, kcat_ref, cbf_ref, qcat_ref, gate_ref, *, bblk, tt, transpose_k):
    rows = bblk * tt
    x = x_ref[...].reshape(rows, D_MODEL)
    xb = x.astype(BF16)
    cos2 = cos_ref[...]
    sin2 = sin_ref[...]
    if bblk > 1:
        cos2 = jnp.concatenate([cos2] * bblk, axis=0)
        sin2 = jnp.concatenate([sin2] * bblk, axis=0)

    kv = _dot(xb, w_dkv_ref[...])
    c = _rms_norm(kv[:, :KV_LORA], kvg_ref[...])
    krd = kv[:, KV_LORA:KV_LORA + LANES] * cos2 + kv[:, KV_LORA + LANES:KV_LORA + 2 * LANES] * sin2
    ckv_ref[...] = c.reshape(bblk, tt, KV_LORA)
    krope_ref[...] = krd[:, :ROPE_DIM].reshape(bblk, tt, ROPE_DIM)
    cbf_ref[...] = c.astype(BF16).reshape(bblk, tt, KV_LORA)
    kcat = jnp.concatenate([c, krd], axis=1)
    if transpose_k:
        kcat_ref[0, 0] = kcat.T.astype(BF16)
    else:
        kcat_ref[...] = kcat.astype(BF16).reshape(bblk, tt, KCAT)

    proj = _dot(xb, w_in_ref[...])
    cq = _rms_norm(proj[:, :Q_LORA], qg_ref[...])
    gt = proj[:, Q_LORA:]
    gate_ref[...] = (gt * _sigmoid_pair(gt)[0]).reshape(bblk, tt, B_WIDTH)
    qall = _dot(cq.astype(BF16), w_uq_ref[...])
    nope_w = B_HEADS * NOPE_DIM
    pair_w = (B_HEADS // 2) * LANES
    lane = lax.broadcasted_iota(jnp.int32, (rows, LANES), 1)
    for j in range(B_HEADS // 2):
        raw = qall[:, nope_w + j * LANES:nope_w + (j + 1) * LANES]
        rot = qall[:, nope_w + pair_w + j * LANES:nope_w + pair_w + (j + 1) * LANES]
        rp = (raw * cos2 + rot * sin2) * ATTN_SCALE
        for e in range(2):
            h = 2 * j + e
            ql = _dot(qall[:, h * NOPE_DIM:(h + 1) * NOPE_DIM].astype(BF16), w_ukt_ref[h]) * ATTN_SCALE
            keep = (lane < ROPE_DIM) if e == 0 else (lane >= ROPE_DIM)
            qc = jnp.concatenate([ql, jnp.where(keep, rp, 0.0)], axis=1).astype(BF16)
            for bb in range(bblk):
                qcat_ref[bb, h] = qc[bb * tt:(bb + 1) * tt]


def _mla_proj(x, cos2, sin2, w_dkv_ext, kv_g, w_in, q_g, w_uq_ext, w_ukt, *, bblk, tt, transpose_k, tag):
    bsz, t, _ = x.shape
    const2 = lambda b, i: (0, 0)
    const3 = lambda b, i: (0, 0, 0)
    if transpose_k:
        assert bblk == 1
        kcat_shape = jax.ShapeDtypeStruct((bsz, t // tt, KCAT, tt), BF16)
        kcat_spec = pl.BlockSpec((1, 1, KCAT, tt), lambda b, i: (b, i, 0, 0))
    else:
        kcat_shape = jax.ShapeDtypeStruct((bsz, t, KCAT), BF16)
        kcat_spec = pl.BlockSpec((bblk, tt, KCAT), lambda b, i: (b, i, 0))
    tok = lambda w: pl.BlockSpec((bblk, tt, w), lambda b, i: (b, i, 0))
    kern = functools.partial(_mla_proj_kernel, bblk=bblk, tt=tt, transpose_k=transpose_k)
    return pl.pallas_call(
        kern,
        grid=(bsz // bblk, t // tt),
        in_specs=[
            tok(D_MODEL),
            pl.BlockSpec((tt, LANES), lambda b, i: (i, 0)),
            pl.BlockSpec((tt, LANES), lambda b, i: (i, 0)),
            pl.BlockSpec(w_dkv_ext.shape, const2),
            pl.BlockSpec((1, KV_LORA), const2),
            pl.BlockSpec(w_in.shape, const2),
            pl.BlockSpec((1, Q_LORA), const2),
            pl.BlockSpec(w_uq_ext.shape, const2),
            pl.BlockSpec(w_ukt.shape, const3),
        ],
        out_specs=[
            tok(KV_LORA),
            tok(ROPE_DIM),
            kcat_spec,
            tok(KV_LORA),
            pl.BlockSpec((bblk, B_HEADS, tt, KCAT), lambda b, i: (b, 0, i, 0)),
            tok(B_WIDTH),
        ],
        out_shape=[
            jax.ShapeDtypeStruct((bsz, t, KV_LORA), F32),
            jax.ShapeDtypeStruct((bsz, t, ROPE_DIM), F32),
            kcat_shape,
            jax.ShapeDtypeStruct((bsz, t, KV_LORA), BF16),
            jax.ShapeDtypeStruct((bsz, B_HEADS, t, KCAT), BF16),
            jax.ShapeDtypeStruct((bsz, t, B_WIDTH), F32),
        ],
        compiler_params=pltpu.CompilerParams(
            dimension_semantics=("arbitrary", "arbitrary"), vmem_limit_bytes=VMEM_LIMIT),
        name=f"mla_proj_{tag}",
    )(x, cos2, sin2, w_dkv_ext, kv_g, w_in, q_g, w_uq_ext, w_ukt)


def _softmax_step(s, v, m_ref, l_ref, acc_ref):
    m_prev = m_ref[...]
    m_new = jnp.maximum(m_prev, jnp.max(s, axis=1, keepdims=True))
    a = jnp.exp(m_prev - m_new)
    p = jnp.exp(s - m_new)
    l_ref[...] = a * l_ref[...] + jnp.sum(p, axis=1, keepdims=True)
    acc_ref[...] = a * acc_ref[...] + _dot(p.astype(BF16), v)
    m_ref[...] = m_new


def _attn_output(acc_ref, l_ref, o_ref, gate, x, w_uv_ref, w_out_ref, lng_ref, lnb_ref, tq):
    for h in range(B_HEADS):
        hs = slice(h * tq, (h + 1) * tq)
        o_lat = acc_ref[hs, :] / l_ref[hs, :]
        oh = _dot(o_lat.astype(BF16), w_uv_ref[h])
        o_ref[:, h * B_DV:(h + 1) * B_DV] = (oh * gate[:, h * B_DV:(h + 1) * B_DV]).astype(BF16)
    out = _dot(o_ref[...], w_out_ref[...])
    return _layer_norm(ALPHA * x + out, lng_ref[...], lnb_ref[...])


def _attn_prompt_kernel(q_ref, kt_ref, c_ref, gate_ref, x_ref, w_uv_ref, w_out_ref, lng_ref, lnb_ref,
                        y_ref, m_ref, l_ref, acc_ref, o_ref, *, tq, tk):
    i = pl.program_id(1)
    rows = B_HEADS * tq
    q = q_ref[0].reshape(rows, KCAT)
    m_ref[...] = jnp.full((rows, 1), NEG_INF, F32)
    l_ref[...] = jnp.zeros((rows, 1), F32)
    acc_ref[...] = jnp.zeros((rows, KV_LORA), F32)

    n_full = (i * tq) // tk

    def body(j, carry):
        s = _dot(q, kt_ref[0, j])
        _softmax_step(s, c_ref[0, pl.ds(pl.multiple_of(j * tk, tk), tk), :], m_ref, l_ref, acc_ref)
        return carry

    lax.fori_loop(0, n_full, body, 0)

    n_all = ((i + 1) * tq + tk - 1) // tk
    tok = lax.broadcasted_iota(jnp.int32, (rows, tk), 0) & (tq - 1)
    q_chunk = (i * tq + tok) >> CHUNK_SHIFT
    key = lax.broadcasted_iota(jnp.int32, (rows, tk), 1)

    def body_masked(j, carry):
        s = _dot(q, kt_ref[0, j])
        k_chunk = (j * tk + key) >> CHUNK_SHIFT
        s = jnp.where(k_chunk <= q_chunk, s, NEG_INF)
        _softmax_step(s, c_ref[0, pl.ds(pl.multiple_of(j * tk, tk), tk), :], m_ref, l_ref, acc_ref)
        return carry

    lax.fori_loop(n_full, n_all, body_masked, 0)

    y_ref[0] = _attn_output(acc_ref, l_ref, o_ref, gate_ref[0], x_ref[0],
                            w_uv_ref, w_out_ref, lng_ref, lnb_ref, tq)


def _attn_prompt(qcat, kt, cbf, gate, x, w_uv_h, w_out, ln_g, ln_b, *, tq):
    bsz, _, t, _ = qcat.shape
    tk = kt.shape[-1]
    nk = kt.shape[1]
    rows = B_HEADS * tq
    const2 = lambda b, i: (0, 0)
    kern = functools.partial(_attn_prompt_kernel, tq=tq, tk=tk)
    return pl.pallas_call(
        kern,
        grid=(bsz, t // tq),
        in_specs=[
            pl.BlockSpec((1, B_HEADS, tq, KCAT), lambda b, i: (b, 0, i, 0)),
            pl.BlockSpec((1, nk, KCAT, tk), lambda b, i: (b, 0, 0, 0)),
            pl.BlockSpec((1, t, KV_LORA), lambda b, i: (b, 0, 0)),
            pl.BlockSpec((1, tq, B_WIDTH), lambda b, i: (b, i, 0)),
            pl.BlockSpec((1, tq, D_MODEL), lambda b, i: (b, i, 0)),
            pl.BlockSpec(w_uv_h.shape, lambda b, i: (0, 0, 0)),
            pl.BlockSpec(w_out.shape, const2),
            pl.BlockSpec((1, D_MODEL), const2),
            pl.BlockSpec((1, D_MODEL), const2),
        ],
        out_specs=pl.BlockSpec((1, tq, D_MODEL), lambda b, i: (b, i, 0)),
        out_shape=jax.ShapeDtypeStruct((bsz, t, D_MODEL), F32),
        scratch_shapes=[
            pltpu.VMEM((rows, 1), F32),
            pltpu.VMEM((rows, 1), F32),
            pltpu.VMEM((rows, KV_LORA), F32),
            pltpu.VMEM((tq, B_WIDTH), BF16),
        ],
        compiler_params=pltpu.CompilerParams(
            dimension_semantics=("arbitrary", "arbitrary"), vmem_limit_bytes=VMEM_LIMIT),
        name="mla_attend_prompt",
    )(qcat, kt, cbf, gate, x, w_uv_h, w_out, ln_g, ln_b)


def _attn_sample_kernel(q_ref, cc_ref, ckr_ref, kn_ref, cn_ref, gate_ref, x_ref, w_uv_ref, w_out_ref,
                        lng_ref, lnb_ref, y_ref, m_ref, l_ref, acc_ref, o_ref, *, tq, tk):
    rows = B_HEADS * tq
    past = cc_ref.shape[1]
    q = q_ref[0].reshape(rows, KCAT)
    q_lat = q[:, :KV_LORA]
    q3 = q[:, KV_LORA:].astype(F32)
    q_rope = (q3[:, :ROPE_DIM] + q3[:, ROPE_DIM:]).astype(BF16)
    m_ref[...] = jnp.full((rows, 1), NEG_INF, F32)
    l_ref[...] = jnp.zeros((rows, 1), F32)
    acc_ref[...] = jnp.zeros((rows, KV_LORA), F32)

    for j in range(past // tk):
        cb = cc_ref[0, j * tk:(j + 1) * tk, :].astype(BF16)
        krb = ckr_ref[0, j * tk:(j + 1) * tk, :].astype(BF16)
        s = (lax.dot_general(q_lat, cb, _NT, preferred_element_type=F32)
             + lax.dot_general(q_rope, krb, _NT, preferred_element_type=F32))
        _softmax_step(s, cb, m_ref, l_ref, acc_ref)

    s = lax.dot_general(q, kn_ref[0], _NT, preferred_element_type=F32)
    _softmax_step(s, cn_ref[0], m_ref, l_ref, acc_ref)

    y_ref[0] = _attn_output(acc_ref, l_ref, o_ref, gate_ref[0], x_ref[0],
                            w_uv_ref, w_out_ref, lng_ref, lnb_ref, tq)


def _attn_sample(qcat, cache_c, cache_kr, kcat_new, cbf_new, gate, x, w_uv_h, w_out, ln_g, ln_b, *, tk):
    bsz, _, tq, _ = qcat.shape
    past = cache_c.shape[1]
    rows = B_HEADS * tq
    const2 = lambda b: (0, 0)
    per_b = lambda shape: pl.BlockSpec((1,) + shape, lambda b: (b,) + (0,) * len(shape))
    kern = functools.partial(_attn_sample_kernel, tq=tq, tk=tk)
    return pl.pallas_call(
        kern,
        grid=(bsz,),
        in_specs=[
            per_b((B_HEADS, tq, KCAT)),
            per_b((past, KV_LORA)),
            per_b((past, ROPE_DIM)),
            per_b((tq, KCAT)),
            per_b((tq, KV_LORA)),
            per_b((tq, B_WIDTH)),
            per_b((tq, D_MODEL)),
            pl.BlockSpec(w_uv_h.shape, lambda b: (0, 0, 0)),
            pl.BlockSpec(w_out.shape, const2),
            pl.BlockSpec((1, D_MODEL), const2),
            pl.BlockSpec((1, D_MODEL), const2),
        ],
        out_specs=per_b((tq, D_MODEL)),
        out_shape=jax.ShapeDtypeStruct((bsz, tq, D_MODEL), F32),
        scratch_shapes=[
            pltpu.VMEM((rows, 1), F32),
            pltpu.VMEM((rows, 1), F32),
            pltpu.VMEM((rows, KV_LORA), F32),
            pltpu.VMEM((tq, B_WIDTH), BF16),
        ],
        compiler_params=pltpu.CompilerParams(
            dimension_semantics=("arbitrary",), vmem_limit_bytes=VMEM_LIMIT),
        name="mla_attend_sample",
    )(qcat, cache_c, cache_kr, kcat_new, cbf_new, gate, x, w_uv_h, w_out, ln_g, ln_b)


def _rot_cols(w):
    half = ROPE_DIM // 2
    return jnp.concatenate([-w[..., half:], w[..., :half]], axis=-1)


def _rope_tables(pos):
    half = ROPE_DIM // 2
    inv = jnp.power(ROPE_BASE, -jnp.arange(half, dtype=F32) / half)
    ang = pos.astype(F32)[:, None] * inv[None, :]
    cos, sin = jnp.cos(ang), jnp.sin(ang)
    return jnp.concatenate([cos] * 4, axis=1), jnp.concatenate([sin] * 4, axis=1)


def kernel(x_prompt, x_sample, state_hgrn, cache_ckv, cache_krope, w_in_a, lb_gamma, a_norm_g, w_out_a,
           w_dkv, kv_norm_g, w_uk, w_uv, w_in_b, q_norm_g, w_uq, w_out_b, ln_g, ln_b):
    assert N_A == 1 and DEPTH == 2
    t_p = x_prompt.shape[1]
    t_s = x_sample.shape[1]
    past = cache_ckv.shape[1]
    row = lambda a: a.reshape(1, -1)

    w_in_a_b = w_in_a[0].astype(BF16)
    w_out_a_b = w_out_a[0].astype(BF16)
    kr_w = w_dkv[:, KV_LORA:]
    w_dkv_ext = jnp.concatenate(
        [w_dkv[:, :KV_LORA], kr_w, kr_w, _rot_cols(kr_w), _rot_cols(kr_w)], axis=1).astype(BF16)
    w_uq3 = w_uq[0].reshape(Q_LORA, B_HEADS, NOPE_DIM + ROPE_DIM)
    q_rope_w = w_uq3[:, :, NOPE_DIM:]
    w_uq_ext = jnp.concatenate([
        w_uq3[:, :, :NOPE_DIM].reshape(Q_LORA, B_HEADS * NOPE_DIM),
        q_rope_w.reshape(Q_LORA, B_HEADS * ROPE_DIM),
        _rot_cols(q_rope_w).reshape(Q_LORA, B_HEADS * ROPE_DIM)], axis=1).astype(BF16)
    w_ukt = jnp.transpose(w_uk, (1, 2, 0)).astype(BF16)
    w_uv_h = jnp.transpose(w_uv, (1, 0, 2)).astype(BF16)
    w_in_b_b = w_in_b[0].astype(BF16)
    w_out_b_b = w_out_b[0].astype(BF16)

    outs = []
    for x, s0, pos0, prompt in ((x_prompt, None, 0, True), (x_sample, state_hgrn[0], past, False)):
        bsz, t, _ = x.shape
        if prompt:
            bblk, tt, chunk, tm = 1, 256, CHUNK, 256
        else:
            bblk, tt, chunk, tm = bsz, t, t, t
        x1, s_fin = _hgrn_layer(x, s0, w_in_a_b, lb_gamma, row(a_norm_g[0]), w_out_a_b,
                                row(ln_g[0]), row(ln_b[0]), layer=0, bblk=bblk, tt=tt, chunk=chunk)
        cos2, sin2 = _rope_tables(pos0 + jnp.arange(t, dtype=jnp.int32))
        ckv, krope, kcat, cbf, qcat, gate = _mla_proj(
            x1, cos2, sin2, w_dkv_ext, row(kv_norm_g), w_in_b_b, row(q_norm_g[0]), w_uq_ext, w_ukt,
            bblk=bblk, tt=tm, transpose_k=prompt, tag="prompt" if prompt else "sample")
        if prompt:
            y = _attn_prompt(qcat, kcat, cbf, gate, x1, w_uv_h, w_out_b_b, row(ln_g[1]), row(ln_b[1]), tq=256)
        else:
            y = _attn_sample(qcat, cache_ckv, cache_krope, kcat, cbf, gate, x1, w_uv_h, w_out_b_b,
                             row(ln_g[1]), row(ln_b[1]), tk=512)
        outs.append((y, s_fin[None], ckv, krope))
    (y_p, s_p, c_p, kr_p), (y_s, s_s, c_s, kr_s) = outs
    return (y_p, y_s, s_p, c_p, kr_p, s_s, c_s, kr_s)
```

```python
import functools

import jax
import jax.numpy as jnp
from jax import lax
from jax.experimental import pallas as pl
from jax.experimental.pallas import tpu as pltpu

F32 = jnp.float32
BF16 = jnp.bfloat16

D_MODEL = 1024
DEPTH = 2
CHUNK = 64
CHUNK_SHIFT = 6
N_A = DEPTH // 2
A_HEADS = 8
A_DK = 128
A_DV = 128
A_WIDTH = A_HEADS * A_DV
B_HEADS = 16
Q_LORA = 512
KV_LORA = 256
NOPE_DIM = 128
ROPE_DIM = 64
B_DV = 128
B_WIDTH = B_HEADS * B_DV
ROPE_BASE = 10000.0
EPS = 1e-6
NEG_INF = -1e30
ATTN_SCALE = (NOPE_DIM + ROPE_DIM) ** -0.5
ALPHA = (2 * DEPTH) ** 0.25

LANES = 128
KCAT = KV_LORA + 2 * ROPE_DIM
VMEM_LIMIT = 56 * 1024 * 1024

_NT = (((1,), (1,)), ((), ()))
_TN = (((0,), (0,)), ((), ()))


def _dot(a, b):
    return jnp.dot(a, b, preferred_element_type=F32)


def _layer_norm(x, g, b):
    mu = jnp.mean(x, axis=-1, keepdims=True)
    xc = x - mu
    var = jnp.mean(xc * xc, axis=-1, keepdims=True)
    return xc * lax.rsqrt(var + EPS) * g + b


def _rms_norm(x, g):
    return x * lax.rsqrt(jnp.mean(x * x, axis=-1, keepdims=True) + EPS) * g


def _sigmoid_pair(z):
    e = jnp.exp(-jnp.abs(z))
    r = 1.0 / (1.0 + e)
    er = e * r
    pos = z >= 0
    return jnp.where(pos, r, er), jnp.where(pos, er, r)


def _split3(x):
    hi = x.astype(BF16)
    r1 = x - hi.astype(F32)
    mid = r1.astype(BF16)
    lo = (r1 - mid.astype(F32)).astype(BF16)
    return hi, mid, lo


def _hgrn_kernel(x_ref, s0_ref, w_in_ref, lbg_ref, ng_ref, w_out_ref, lng_ref, lnb_ref, tri_ref,
                 y_ref, sfin_ref, st_ref, o_ref, *, layer, bblk, tt, chunk, zero_init):
    t = pl.program_id(1)
    rows = bblk * tt
    mid = chunk // 2

    @pl.when(t == 0)
    def _():
        for bb in range(bblk):
            for h in range(A_HEADS):
                if zero_init:
                    st_ref[bb, h] = jnp.zeros((A_DV, A_DK), F32)
                else:
                    st_ref[bb, h] = s0_ref[bb, h].T

    x = x_ref[...].reshape(rows, D_MODEL)
    proj = _dot(x.astype(BF16), w_in_ref[...])
    q = proj[:, 0:A_WIDTH]
    z = proj[:, A_WIDTH:2 * A_WIDTH]
    v = proj[:, 2 * A_WIDTH:3 * A_WIDTH].astype(BF16)
    g = proj[:, 3 * A_WIDTH:4 * A_WIDTH]

    lbg = lbg_ref[...]
    e = jnp.exp(lbg - jnp.max(lbg, axis=0, keepdims=True))
    lb = jnp.sum(e[0:layer + 1], axis=0, keepdims=True) / jnp.sum(e, axis=0, keepdims=True)
    one_m_lb = 1.0 - lb

    sig_pos, sig_neg = _sigmoid_pair(z)
    logf = jnp.log(lb + one_m_lb * sig_pos)
    k = one_m_lb * sig_neg

    hi, m_, lo = _split3(logf)
    b = _dot(tri_ref[...], jnp.concatenate([hi, m_, lo], axis=0))

    silu_g = g * _sigmoid_pair(g)[0]
    ng = ng_ref[...]
    li = lax.broadcasted_iota(jnp.int32, (chunk, chunk), 0)
    si = lax.broadcasted_iota(jnp.int32, (chunk, chunk), 1)
    causal = li >= si

    for bb in range(bblk):
        for c in range(tt // chunk):
            r0 = bb * tt + c * chunk
            rs = slice(r0, r0 + chunk)
            bc = b[rs]
            bmid = bc[mid:mid + 1]
            blast = bc[chunk - 1:chunk]
            qc = q[rs]
            kc = k[rs]
            q_in = (qc * jnp.exp(bc - bmid)).astype(BF16)
            k_in = (kc * jnp.exp(bmid - bc)).astype(BF16)
            q_ex = (qc * jnp.exp(bc)).astype(BF16)
            k_dec = (kc * jnp.exp(blast - bc)).astype(BF16)
            decay = jnp.exp(blast)
            vc = v[rs]
            for h in range(A_HEADS):
                hs = slice(h * A_DK, (h + 1) * A_DK)
                st = st_ref[bb, h]
                sc = lax.dot_general(q_in[:, hs], k_in[:, hs], _NT, preferred_element_type=F32)
                sc = jnp.where(causal, sc, 0.0).astype(BF16)
                o = _dot(sc, vc[:, hs]) + lax.dot_general(
                    q_ex[:, hs], st.astype(BF16), _NT, preferred_element_type=F32)
                st_ref[bb, h] = st * decay[:, hs] + lax.dot_general(
                    vc[:, hs], k_dec[:, hs], _TN, preferred_element_type=F32)
                o = _rms_norm(o, ng) * silu_g[rs, hs]
                o_ref[rs, hs] = o.astype(BF16)

    out = _dot(o_ref[...], w_out_ref[...])
    y = _layer_norm(ALPHA * x + out, lng_ref[...], lnb_ref[...])
    y_ref[...] = y.reshape(bblk, tt, D_MODEL)

    @pl.when(t == pl.num_programs(1) - 1)
    def _():
        for bb in range(bblk):
            for h in range(A_HEADS):
                sfin_ref[bb, h] = st_ref[bb, h].T


def _hgrn_layer(x, s0, w_in, lb_gamma, norm_g, w_out, ln_g, ln_b, *, layer, bblk, tt, chunk):
    bsz, t, _ = x.shape
    rows = bblk * tt
    zero_init = s0 is None
    if zero_init:
        s0 = jnp.zeros((bblk, A_HEADS, 8, LANES), F32)
        s0_spec = pl.BlockSpec((bblk, A_HEADS, 8, LANES), lambda b, i: (0, 0, 0, 0))
    else:
        s0_spec = pl.BlockSpec((bblk, A_HEADS, A_DK, A_DV), lambda b, i: (b, 0, 0, 0))
    r = jnp.arange(rows)
    tri = ((r[:, None] // chunk == r[None, :] // chunk) & (r[:, None] >= r[None, :])).astype(BF16)
    tri3 = jnp.concatenate([tri, tri, tri], axis=1)
    const = lambda b, i: (0, 0)
    kern = functools.partial(_hgrn_kernel, layer=layer, bblk=bblk, tt=tt, chunk=chunk, zero_init=zero_init)
    return pl.pallas_call(
        kern,
        grid=(bsz // bblk, t // tt),
        in_specs=[
            pl.BlockSpec((bblk, tt, D_MODEL), lambda b, i: (b, i, 0)),
            s0_spec,
            pl.BlockSpec((D_MODEL, 4 * A_WIDTH), const),
            pl.BlockSpec(lb_gamma.shape, const),
            pl.BlockSpec((1, A_DV), const),
            pl.BlockSpec((A_WIDTH, D_MODEL), const),
            pl.BlockSpec((1, D_MODEL), const),
            pl.BlockSpec((1, D_MODEL), const),
            pl.BlockSpec((rows, 3 * rows), const),
        ],
        out_specs=[
            pl.BlockSpec((bblk, tt, D_MODEL), lambda b, i: (b, i, 0)),
            pl.BlockSpec((bblk, A_HEADS, A_DK, A_DV), lambda b, i: (b, 0, 0, 0)),
        ],
        out_shape=[
            jax.ShapeDtypeStruct((bsz, t, D_MODEL), F32),
            jax.ShapeDtypeStruct((bsz, A_HEADS, A_DK, A_DV), F32),
        ],
        scratch_shapes=[
            pltpu.VMEM((bblk, A_HEADS, A_DV, A_DK), F32),
            pltpu.VMEM((rows, A_WIDTH), BF16),
        ],
        compiler_params=pltpu.CompilerParams(
            dimension_semantics=("arbitrary", "arbitrary"), vmem_limit_bytes=VMEM_LIMIT),
        name=f"hgrn_layer_{'prompt' if zero_init else 'sample'}",
    )(x, s0, w_in, lb_gamma, norm_g, w_out, ln_g, ln_b, tri3)


def _mla_proj_kernel(x_ref, cos_ref, sin_ref, w_dkv_ref, kvg_ref, w_in_ref, qg_ref, w_uq_ref, w_ukt_ref,
                     *out_refs, bblk, tt, transposed):
    if transposed:
        ckv_ref, krope_ref, kcat_ref, kt_ref, q_ref, gate_ref = out_refs
    else:
        ckv_ref, krope_ref, kcat_ref, q_ref, gate_ref = out_refs
    rows = bblk * tt
    x = x_ref[...].reshape(rows, D_MODEL)
    xb = x.astype(BF16)
    cos2 = cos_ref[...]
    sin2 = sin_ref[...]
    if bblk > 1:
        cos2 = jnp.concatenate([cos2] * bblk, axis=0)
        sin2 = jnp.concatenate([sin2] * bblk, axis=0)

    kv = _dot(xb, w_dkv_ref[...])
    c = _rms_norm(kv[:, :KV_LORA], kvg_ref[...])
    krd = kv[:, KV_LORA:KV_LORA + LANES] * cos2 + kv[:, KV_LORA + LANES:KV_LORA + 2 * LANES] * sin2
    ckv_ref[...] = c.reshape(bblk, tt, KV_LORA)
    krope_ref[...] = krd[:, :ROPE_DIM].reshape(bblk, tt, ROPE_DIM)
    kcat = jnp.concatenate([c, krd], axis=1)
    kcat_ref[...] = kcat.astype(BF16).reshape(bblk, tt, KCAT)
    if transposed:
        kt_ref[0, 0] = kcat.T.astype(BF16)

    proj = _dot(xb, w_in_ref[...])
    cq = _rms_norm(proj[:, :Q_LORA], qg_ref[...])
    gt = proj[:, Q_LORA:]
    gate_ref[...] = (gt * _sigmoid_pair(gt)[0]).reshape(bblk, tt, B_WIDTH)
    qall = _dot(cq.astype(BF16), w_uq_ref[...])
    nope_w = B_HEADS * NOPE_DIM
    pair_w = (B_HEADS // 2) * LANES
    lane = lax.broadcasted_iota(jnp.int32, (rows, LANES), 1)
    for j in range(B_HEADS // 2):
        raw = qall[:, nope_w + j * LANES:nope_w + (j + 1) * LANES]
        rot = qall[:, nope_w + pair_w + j * LANES:nope_w + pair_w + (j + 1) * LANES]
        rp = (raw * cos2 + rot * sin2) * ATTN_SCALE
        for e in range(2):
            h = 2 * j + e
            ql = _dot(qall[:, h * NOPE_DIM:(h + 1) * NOPE_DIM].astype(BF16), w_ukt_ref[h]) * ATTN_SCALE
            keep = (lane < ROPE_DIM) if e == 0 else (lane >= ROPE_DIM)
            qc = jnp.concatenate([ql, jnp.where(keep, rp, 0.0)], axis=1)
            if transposed:
                q_ref[0, 0, :, h * tt:(h + 1) * tt] = qc.T.astype(BF16)
            else:
                qc = qc.astype(BF16)
                for bb in range(bblk):
                    q_ref[bb, h] = qc[bb * tt:(bb + 1) * tt]


def _mla_proj(x, cos2, sin2, w_dkv_ext, kv_g, w_in, q_g, w_uq_ext, w_ukt, *, bblk, tt, transposed, tag):
    bsz, t, _ = x.shape
    const2 = lambda b, i: (0, 0)
    const3 = lambda b, i: (0, 0, 0)
    tok = lambda w: pl.BlockSpec((bblk, tt, w), lambda b, i: (b, i, 0))
    out_specs = [tok(KV_LORA), tok(ROPE_DIM), tok(KCAT)]
    out_shape = [
        jax.ShapeDtypeStruct((bsz, t, KV_LORA), F32),
        jax.ShapeDtypeStruct((bsz, t, ROPE_DIM), F32),
        jax.ShapeDtypeStruct((bsz, t, KCAT), BF16),
    ]
    if transposed:
        assert bblk == 1
        out_specs += [pl.BlockSpec((1, 1, KCAT, tt), lambda b, i: (b, i, 0, 0)),
                      pl.BlockSpec((1, 1, KCAT, B_HEADS * tt), lambda b, i: (b, i, 0, 0))]
        out_shape += [jax.ShapeDtypeStruct((bsz, t // tt, KCAT, tt), BF16),
                      jax.ShapeDtypeStruct((bsz, t // tt, KCAT, B_HEADS * tt), BF16)]
    else:
        out_specs += [pl.BlockSpec((bblk, B_HEADS, tt, KCAT), lambda b, i: (b, 0, i, 0))]
        out_shape += [jax.ShapeDtypeStruct((bsz, B_HEADS, t, KCAT), BF16)]
    out_specs += [tok(B_WIDTH)]
    out_shape += [jax.ShapeDtypeStruct((bsz, t, B_WIDTH), F32)]
    kern = functools.partial(_mla_proj_kernel, bblk=bblk, tt=tt, transposed=transposed)
    return pl.pallas_call(
        kern,
        grid=(bsz // bblk, t // tt),
        in_specs=[
            tok(D_MODEL),
            pl.BlockSpec((tt, LANES), lambda b, i: (i, 0)),
            pl.BlockSpec((tt, LANES), lambda b, i: (i, 0)),
            pl.BlockSpec(w_dkv_ext.shape, const2),
            pl.BlockSpec((1, KV_LORA), const2),
            pl.BlockSpec(w_in.shape, const2),
            pl.BlockSpec((1, Q_LORA), const2),
            pl.BlockSpec(w_uq_ext.shape, const2),
            pl.BlockSpec(w_ukt.shape, const3),
        ],
        out_specs=out_specs,
        out_shape=out_shape,
        compiler_params=pltpu.CompilerParams(
            dimension_semantics=("arbitrary", "arbitrary"), vmem_limit_bytes=VMEM_LIMIT),
        name=f"mla_proj_{tag}",
    )(x, cos2, sin2, w_dkv_ext, kv_g, w_in, q_g, w_uq_ext, w_ukt)


def _softmax_step(s, v, m_ref, l_ref, acc_ref):
    m_prev = m_ref[...]
    m_new = jnp.maximum(m_prev, jnp.max(s, axis=1, keepdims=True))
    a = jnp.exp(m_prev - m_new)
    p = jnp.exp(s - m_new)
    l_ref[...] = a * l_ref[...] + jnp.sum(p, axis=1, keepdims=True)
    acc_ref[...] = a * acc_ref[...] + _dot(p.astype(BF16), v)
    m_ref[...] = m_new


def _attn_output(acc_ref, l_ref, o_ref, gate, x, w_uv_ref, w_out_ref, lng_ref, lnb_ref, tq):
    for h in range(B_HEADS):
        hs = slice(h * tq, (h + 1) * tq)
        o_lat = acc_ref[hs, :] / l_ref[hs, :]
        oh = _dot(o_lat.astype(BF16), w_uv_ref[h])
        o_ref[:, h * B_DV:(h + 1) * B_DV] = (oh * gate[:, h * B_DV:(h + 1) * B_DV]).astype(BF16)
    out = _dot(o_ref[...], w_out_ref[...])
    return _layer_norm(ALPHA * x + out, lng_ref[...], lnb_ref[...])


def _attn_prompt_kernel(qt_ref, kcat_ref, kt_ref, gate_ref, x_ref, w_uvt_ref, w_out_ref, lng_ref, lnb_ref,
                        y_ref, m_ref, l_ref, acc_ref, o_ref, *, tq, rc):
    i = pl.program_id(1)
    rows = B_HEADS * tq
    m_ref[...] = jnp.full((1, rows), NEG_INF, F32)
    l_ref[...] = jnp.zeros((1, rows), F32)
    acc_ref[...] = jnp.zeros((KV_LORA, rows), F32)

    def kv_step(j, bias):
        kb = kcat_ref[0, pl.ds(pl.multiple_of(j * tq, tq), tq), :]
        ct = kt_ref[0, j, 0:KV_LORA, :]
        for r in range(rows // rc):
            ls = slice(r * rc, (r + 1) * rc)
            st = _dot(kb, qt_ref[0, 0, :, ls])
            if bias is not None:
                st = st + bias
            m_prev = m_ref[:, ls]
            m_new = jnp.maximum(m_prev, jnp.max(st, axis=0, keepdims=True))
            a = jnp.exp(m_prev - m_new)
            pt = jnp.exp(st - m_new)
            l_ref[:, ls] = a * l_ref[:, ls] + jnp.sum(pt, axis=0, keepdims=True)
            acc_ref[:, ls] = a * acc_ref[:, ls] + _dot(ct, pt.astype(BF16))
            m_ref[:, ls] = m_new

    def body(j, carry):
        kv_step(j, None)
        return carry

    lax.fori_loop(0, i, body, 0)

    key_chunk = lax.broadcasted_iota(jnp.int32, (tq, rc), 0) >> CHUNK_SHIFT
    q_chunk = (lax.broadcasted_iota(jnp.int32, (tq, rc), 1) & (tq - 1)) >> CHUNK_SHIFT
    kv_step(i, jnp.where(key_chunk <= q_chunk, 0.0, NEG_INF))

    gate = gate_ref[0]
    for h in range(B_HEADS):
        hs = slice(h * tq, (h + 1) * tq)
        o_lat = acc_ref[:, hs] / l_ref[:, hs]
        oh = _dot(w_uvt_ref[h], o_lat.astype(BF16)).T
        o_ref[:, h * B_DV:(h + 1) * B_DV] = (oh * gate[:, h * B_DV:(h + 1) * B_DV]).astype(BF16)
    out = _dot(o_ref[...], w_out_ref[...])
    y_ref[0] = _layer_norm(ALPHA * x_ref[0] + out, lng_ref[...], lnb_ref[...])


def _attn_prompt(qt, kcat, kt, gate, x, w_uvt_h, w_out, ln_g, ln_b, *, rc):
    bsz, nq, _, rows = qt.shape
    tq = rows // B_HEADS
    t = kcat.shape[1]
    assert kt.shape[-1] == tq and tq & (tq - 1) == 0 and tq % CHUNK == 0 and rc % tq == 0
    const2 = lambda b, i: (0, 0)
    kern = functools.partial(_attn_prompt_kernel, tq=tq, rc=rc)
    return pl.pallas_call(
        kern,
        grid=(bsz, nq),
        in_specs=[
            pl.BlockSpec((1, 1, KCAT, rows), lambda b, i: (b, i, 0, 0)),
            pl.BlockSpec((1, t, KCAT), lambda b, i: (b, 0, 0)),
            pl.BlockSpec((1, nq, KCAT, tq), lambda b, i: (b, 0, 0, 0)),
            pl.BlockSpec((1, tq, B_WIDTH), lambda b, i: (b, i, 0)),
            pl.BlockSpec((1, tq, D_MODEL), lambda b, i: (b, i, 0)),
            pl.BlockSpec(w_uvt_h.shape, lambda b, i: (0, 0, 0)),
            pl.BlockSpec(w_out.shape, const2),
            pl.BlockSpec((1, D_MODEL), const2),
            pl.BlockSpec((1, D_MODEL), const2),
        ],
        out_specs=pl.BlockSpec((1, tq, D_MODEL), lambda b, i: (b, i, 0)),
        out_shape=jax.ShapeDtypeStruct((bsz, t, D_MODEL), F32),
        scratch_shapes=[
            pltpu.VMEM((1, rows), F32),
            pltpu.VMEM((1, rows), F32),
            pltpu.VMEM((KV_LORA, rows), F32),
            pltpu.VMEM((tq, B_WIDTH), BF16),
        ],
        compiler_params=pltpu.CompilerParams(
            dimension_semantics=("arbitrary", "arbitrary"), vmem_limit_bytes=VMEM_LIMIT),
        name="mla_attend_prompt",
    )(qt, kcat, kt, gate, x, w_uvt_h, w_out, ln_g, ln_b)


def _attn_sample_kernel(q_ref, cc_ref, ckr_ref, kn_ref, gate_ref, x_ref, w_uv_ref, w_out_ref,
                        lng_ref, lnb_ref, y_ref, m_ref, l_ref, acc_ref, o_ref, *, tq, tk):
    rows = B_HEADS * tq
    past = cc_ref.shape[1]
    q = q_ref[0].reshape(rows, KCAT)
    q_lat = q[:, :KV_LORA]
    q3 = q[:, KV_LORA:].astype(F32)
    q_rope = (q3[:, :ROPE_DIM] + q3[:, ROPE_DIM:]).astype(BF16)
    m_ref[...] = jnp.full((rows, 1), NEG_INF, F32)
    l_ref[...] = jnp.zeros((rows, 1), F32)
    acc_ref[...] = jnp.zeros((rows, KV_LORA), F32)

    for j in range(past // tk):
        cb = cc_ref[0, j * tk:(j + 1) * tk, :].astype(BF16)
        krb = ckr_ref[0, j * tk:(j + 1) * tk, :].astype(BF16)
        s = (lax.dot_general(q_lat, cb, _NT, preferred_element_type=F32)
             + lax.dot_general(q_rope, krb, _NT, preferred_element_type=F32))
        _softmax_step(s, cb, m_ref, l_ref, acc_ref)

    kn = kn_ref[0]
    s = lax.dot_general(q, kn, _NT, preferred_element_type=F32)
    _softmax_step(s, kn[:, :KV_LORA], m_ref, l_ref, acc_ref)

    y_ref[0] = _attn_output(acc_ref, l_ref, o_ref, gate_ref[0], x_ref[0],
                            w_uv_ref, w_out_ref, lng_ref, lnb_ref, tq)


def _attn_sample(qcat, cache_c, cache_kr, kcat_new, gate, x, w_uv_h, w_out, ln_g, ln_b, *, tk):
    bsz, _, tq, _ = qcat.shape
    past = cache_c.shape[1]
    rows = B_HEADS * tq
    const2 = lambda b: (0, 0)
    per_b = lambda shape: pl.BlockSpec((1,) + shape, lambda b: (b,) + (0,) * len(shape))
    kern = functools.partial(_attn_sample_kernel, tq=tq, tk=tk)
    return pl.pallas_call(
        kern,
        grid=(bsz,),
        in_specs=[
            per_b((B_HEADS, tq, KCAT)),
            per_b((past, KV_LORA)),
            per_b((past, ROPE_DIM)),
            per_b((tq, KCAT)),
            per_b((tq, B_WIDTH)),
            per_b((tq, D_MODEL)),
            pl.BlockSpec(w_uv_h.shape, lambda b: (0, 0, 0)),
            pl.BlockSpec(w_out.shape, const2),
            pl.BlockSpec((1, D_MODEL), const2),
            pl.BlockSpec((1, D_MODEL), const2),
        ],
        out_specs=per_b((tq, D_MODEL)),
        out_shape=jax.ShapeDtypeStruct((bsz, tq, D_MODEL), F32),
        scratch_shapes=[
            pltpu.VMEM((rows, 1), F32),
            pltpu.VMEM((rows, 1), F32),
            pltpu.VMEM((rows, KV_LORA), F32),
            pltpu.VMEM((tq, B_WIDTH), BF16),
        ],
        compiler_params=pltpu.CompilerParams(
            dimension_semantics=("arbitrary",), vmem_limit_bytes=VMEM_LIMIT),
        name="mla_attend_sample",
    )(qcat, cache_c, cache_kr, kcat_new, gate, x, w_uv_h, w_out, ln_g, ln_b)


def _rot_cols(w):
    half = ROPE_DIM // 2
    return jnp.concatenate([-w[..., half:], w[..., :half]], axis=-1)


def _rope_tables(pos):
    half = ROPE_DIM // 2
    inv = jnp.power(ROPE_BASE, -jnp.arange(half, dtype=F32) / half)
    ang = pos.astype(F32)[:, None] * inv[None, :]
    cos, sin = jnp.cos(ang), jnp.sin(ang)
    return jnp.concatenate([cos] * 4, axis=1), jnp.concatenate([sin] * 4, axis=1)


def kernel(x_prompt, x_sample, state_hgrn, cache_ckv, cache_krope, w_in_a, lb_gamma, a_norm_g, w_out_a,
           w_dkv, kv_norm_g, w_uk, w_uv, w_in_b, q_norm_g, w_uq, w_out_b, ln_g, ln_b):
    assert N_A == 1 and DEPTH == 2
    t_p = x_prompt.shape[1]
    t_s = x_sample.shape[1]
    past = cache_ckv.shape[1]
    row = lambda a: a.reshape(1, -1)

    w_in_a_b = w_in_a[0].astype(BF16)
    w_out_a_b = w_out_a[0].astype(BF16)
    kr_w = w_dkv[:, KV_LORA:]
    w_dkv_ext = jnp.concatenate(
        [w_dkv[:, :KV_LORA], kr_w, kr_w, _rot_cols(kr_w), _rot_cols(kr_w)], axis=1).astype(BF16)
    w_uq3 = w_uq[0].reshape(Q_LORA, B_HEADS, NOPE_DIM + ROPE_DIM)
    q_rope_w = w_uq3[:, :, NOPE_DIM:]
    w_uq_ext = jnp.concatenate([
        w_uq3[:, :, :NOPE_DIM].reshape(Q_LORA, B_HEADS * NOPE_DIM),
        q_rope_w.reshape(Q_LORA, B_HEADS * ROPE_DIM),
        _rot_cols(q_rope_w).reshape(Q_LORA, B_HEADS * ROPE_DIM)], axis=1).astype(BF16)
    w_ukt = jnp.transpose(w_uk, (1, 2, 0)).astype(BF16)
    w_uv_h = jnp.transpose(w_uv, (1, 0, 2)).astype(BF16)
    w_uvt_h = jnp.transpose(w_uv, (1, 2, 0)).astype(BF16)
    w_in_b_b = w_in_b[0].astype(BF16)
    w_out_b_b = w_out_b[0].astype(BF16)

    outs = []
    for x, s0, pos0, prompt in ((x_prompt, None, 0, True), (x_sample, state_hgrn[0], past, False)):
        bsz, t, _ = x.shape
        if prompt:
            bblk, tt, chunk, tm = 1, 256, CHUNK, 256
        else:
            bblk, tt, chunk, tm = bsz, t, t, t
        x1, s_fin = _hgrn_layer(x, s0, w_in_a_b, lb_gamma, row(a_norm_g[0]), w_out_a_b,
                                row(ln_g[0]), row(ln_b[0]), layer=0, bblk=bblk, tt=tt, chunk=chunk)
        cos2, sin2 = _rope_tables(pos0 + jnp.arange(t, dtype=jnp.int32))
        proj_outs = _mla_proj(
            x1, cos2, sin2, w_dkv_ext, row(kv_norm_g), w_in_b_b, row(q_norm_g[0]), w_uq_ext, w_ukt,
            bblk=bblk, tt=tm, transposed=prompt, tag="prompt" if prompt else "sample")
        if prompt:
            ckv, krope, kcat, kt, qt, gate = proj_outs
            y = _attn_prompt(qt, kcat, kt, gate, x1, w_uvt_h, w_out_b_b, row(ln_g[1]), row(ln_b[1]), rc=512)
        else:
            ckv, krope, kcat, qcat, gate = proj_outs
            y = _attn_sample(qcat, cache_ckv, cache_krope, kcat, gate, x1, w_uv_h, w_out_b_b,
                             row(ln_g[1]), row(ln_b[1]), tk=512)
        outs.append((y, s_fin[None], ckv, krope))
    (y_p, s_p, c_p, kr_p), (y_s, s_s, c_s, kr_s) = outs
    return (y_p, y_s, s_p, c_p, kr_p, s_s, c_s, kr_s)
```

```python
import functools

import jax
import jax.numpy as jnp
from jax import lax
from jax.experimental import pallas as pl
from jax.experimental.pallas import tpu as pltpu

F32 = jnp.float32
BF16 = jnp.bfloat16

D_MODEL = 1024
DEPTH = 2
CHUNK = 64
CHUNK_SHIFT = 6
N_A = DEPTH // 2
A_HEADS = 8
A_DK = 128
A_DV = 128
A_WIDTH = A_HEADS * A_DV
B_HEADS = 16
Q_LORA = 512
KV_LORA = 256
NOPE_DIM = 128
ROPE_DIM = 64
B_DV = 128
B_WIDTH = B_HEADS * B_DV
ROPE_BASE = 10000.0
EPS = 1e-6
NEG_INF = -1e30
ATTN_SCALE = (NOPE_DIM + ROPE_DIM) ** -0.5
LOG2_E = 1.4426950408889634
Q_SCALE = ATTN_SCALE * LOG2_E
ALPHA = (2 * DEPTH) ** 0.25

LANES = 128
KCAT = KV_LORA + 2 * ROPE_DIM
VMEM_LIMIT = 56 * 1024 * 1024

_NT = (((1,), (1,)), ((), ()))
_TN = (((0,), (0,)), ((), ()))


def _dot(a, b):
    return jnp.dot(a, b, preferred_element_type=F32)


def _layer_norm(x, g, b):
    mu = jnp.mean(x, axis=-1, keepdims=True)
    xc = x - mu
    var = jnp.mean(xc * xc, axis=-1, keepdims=True)
    return xc * lax.rsqrt(var + EPS) * g + b


def _rms_norm(x, g):
    return x * lax.rsqrt(jnp.mean(x * x, axis=-1, keepdims=True) + EPS) * g


def _sigmoid_pair(z):
    e = jnp.exp(-jnp.abs(z))
    r = 1.0 / (1.0 + e)
    er = e * r
    pos = z >= 0
    return jnp.where(pos, r, er), jnp.where(pos, er, r)


def _split3(x):
    hi = x.astype(BF16)
    r1 = x - hi.astype(F32)
    mid = r1.astype(BF16)
    lo = (r1 - mid.astype(F32)).astype(BF16)
    return hi, mid, lo


def _hgrn_kernel(x_ref, s0_ref, w_in_ref, lbg_ref, ng_ref, w_out_ref, lng_ref, lnb_ref, tri_ref,
                 y_ref, sfin_ref, st_ref, o_ref, *, layer, bblk, tt, chunk, zero_init):
    t = pl.program_id(1)
    rows = bblk * tt
    mid = chunk // 2

    @pl.when(t == 0)
    def _():
        for bb in range(bblk):
            for h in range(A_HEADS):
                if zero_init:
                    st_ref[bb, h] = jnp.zeros((A_DV, A_DK), F32)
                else:
                    st_ref[bb, h] = s0_ref[bb, h].T

    x = x_ref[...].reshape(rows, D_MODEL)
    proj = _dot(x.astype(BF16), w_in_ref[...])
    q = proj[:, 0:A_WIDTH]
    z = proj[:, A_WIDTH:2 * A_WIDTH]
    v = proj[:, 2 * A_WIDTH:3 * A_WIDTH].astype(BF16)
    g = proj[:, 3 * A_WIDTH:4 * A_WIDTH]

    lbg = lbg_ref[...]
    e = jnp.exp(lbg - jnp.max(lbg, axis=0, keepdims=True))
    lb = jnp.sum(e[0:layer + 1], axis=0, keepdims=True) / jnp.sum(e, axis=0, keepdims=True)
    one_m_lb = 1.0 - lb

    sig_pos, sig_neg = _sigmoid_pair(z)
    logf = jnp.log(lb + one_m_lb * sig_pos)
    k = one_m_lb * sig_neg

    hi, m_, lo = _split3(logf)
    b = _dot(tri_ref[...], jnp.concatenate([hi, m_, lo], axis=0))

    silu_g = g * _sigmoid_pair(g)[0]
    ng = ng_ref[...]
    li = lax.broadcasted_iota(jnp.int32, (chunk, chunk), 0)
    si = lax.broadcasted_iota(jnp.int32, (chunk, chunk), 1)
    causal = li >= si

    for bb in range(bblk):
        for c in range(tt // chunk):
            r0 = bb * tt + c * chunk
            rs = slice(r0, r0 + chunk)
            bc = b[rs]
            bmid = bc[mid:mid + 1]
            blast = bc[chunk - 1:chunk]
            qc = q[rs]
            kc = k[rs]
            q_in = (qc * jnp.exp(bc - bmid)).astype(BF16)
            k_in = (kc * jnp.exp(bmid - bc)).astype(BF16)
            q_ex = (qc * jnp.exp(bc)).astype(BF16)
            k_dec = (kc * jnp.exp(blast - bc)).astype(BF16)
            decay = jnp.exp(blast)
            vc = v[rs]
            for h in range(A_HEADS):
                hs = slice(h * A_DK, (h + 1) * A_DK)
                st = st_ref[bb, h]
                sc = lax.dot_general(q_in[:, hs], k_in[:, hs], _NT, preferred_element_type=F32)
                sc = jnp.where(causal, sc, 0.0).astype(BF16)
                o = _dot(sc, vc[:, hs]) + lax.dot_general(
                    q_ex[:, hs], st.astype(BF16), _NT, preferred_element_type=F32)
                st_ref[bb, h] = st * decay[:, hs] + lax.dot_general(
                    vc[:, hs], k_dec[:, hs], _TN, preferred_element_type=F32)
                o = _rms_norm(o, ng) * silu_g[rs, hs]
                o_ref[rs, hs] = o.astype(BF16)

    out = _dot(o_ref[...], w_out_ref[...])
    y = _layer_norm(ALPHA * x + out, lng_ref[...], lnb_ref[...])
    y_ref[...] = y.reshape(bblk, tt, D_MODEL)

    @pl.when(t == pl.num_programs(1) - 1)
    def _():
        for bb in range(bblk):
            for h in range(A_HEADS):
                sfin_ref[bb, h] = st_ref[bb, h].T


def _hgrn_layer(x, s0, w_in, lb_gamma, norm_g, w_out, ln_g, ln_b, *, layer, bblk, tt, chunk):
    bsz, t, _ = x.shape
    rows = bblk * tt
    zero_init = s0 is None
    if zero_init:
        s0 = jnp.zeros((bblk, A_HEADS, 8, LANES), F32)
        s0_spec = pl.BlockSpec((bblk, A_HEADS, 8, LANES), lambda b, i: (0, 0, 0, 0))
    else:
        s0_spec = pl.BlockSpec((bblk, A_HEADS, A_DK, A_DV), lambda b, i: (b, 0, 0, 0))
    r = jnp.arange(rows)
    tri = ((r[:, None] // chunk == r[None, :] // chunk) & (r[:, None] >= r[None, :])).astype(BF16)
    tri3 = jnp.concatenate([tri, tri, tri], axis=1)
    const = lambda b, i: (0, 0)
    kern = functools.partial(_hgrn_kernel, layer=layer, bblk=bblk, tt=tt, chunk=chunk, zero_init=zero_init)
    return pl.pallas_call(
        kern,
        grid=(bsz // bblk, t // tt),
        in_specs=[
            pl.BlockSpec((bblk, tt, D_MODEL), lambda b, i: (b, i, 0)),
            s0_spec,
            pl.BlockSpec((D_MODEL, 4 * A_WIDTH), const),
            pl.BlockSpec(lb_gamma.shape, const),
            pl.BlockSpec((1, A_DV), const),
            pl.BlockSpec((A_WIDTH, D_MODEL), const),
            pl.BlockSpec((1, D_MODEL), const),
            pl.BlockSpec((1, D_MODEL), const),
            pl.BlockSpec((rows, 3 * rows), const),
        ],
        out_specs=[
            pl.BlockSpec((bblk, tt, D_MODEL), lambda b, i: (b, i, 0)),
            pl.BlockSpec((bblk, A_HEADS, A_DK, A_DV), lambda b, i: (b, 0, 0, 0)),
        ],
        out_shape=[
            jax.ShapeDtypeStruct((bsz, t, D_MODEL), F32),
            jax.ShapeDtypeStruct((bsz, A_HEADS, A_DK, A_DV), F32),
        ],
        scratch_shapes=[
            pltpu.VMEM((bblk, A_HEADS, A_DV, A_DK), F32),
            pltpu.VMEM((rows, A_WIDTH), BF16),
        ],
        compiler_params=pltpu.CompilerParams(
            dimension_semantics=("arbitrary", "arbitrary"), vmem_limit_bytes=VMEM_LIMIT),
        name=f"hgrn_layer_{'prompt' if zero_init else 'sample'}",
    )(x, s0, w_in, lb_gamma, norm_g, w_out, ln_g, ln_b, tri3)


def _mla_proj_kernel(x_ref, cos_ref, sin_ref, w_dkv_ref, kvg_ref, w_in_ref, qg_ref, w_uq_ref, w_ukt_ref,
                     *out_refs, bblk, tt, transposed):
    if transposed:
        ckv_ref, krope_ref, kcat_ref, kt_ref, q_ref, gate_ref = out_refs
    else:
        ckv_ref, krope_ref, kcat_ref, q_ref, gate_ref = out_refs
    rows = bblk * tt
    x = x_ref[...].reshape(rows, D_MODEL)
    xb = x.astype(BF16)
    cos2 = cos_ref[...]
    sin2 = sin_ref[...]
    if bblk > 1:
        cos2 = jnp.concatenate([cos2] * bblk, axis=0)
        sin2 = jnp.concatenate([sin2] * bblk, axis=0)

    kv = _dot(xb, w_dkv_ref[...])
    c = _rms_norm(kv[:, :KV_LORA], kvg_ref[...])
    krd = kv[:, KV_LORA:KV_LORA + LANES] * cos2 + kv[:, KV_LORA + LANES:KV_LORA + 2 * LANES] * sin2
    ckv_ref[...] = c.reshape(bblk, tt, KV_LORA)
    krope_ref[...] = krd[:, :ROPE_DIM].reshape(bblk, tt, ROPE_DIM)
    kcat = jnp.concatenate([c, krd], axis=1)
    kcat_ref[...] = kcat.astype(BF16).reshape(bblk, tt, KCAT)
    if transposed:
        kt_ref[0, 0] = kcat.T.astype(BF16)

    proj = _dot(xb, w_in_ref[...])
    cq = _rms_norm(proj[:, :Q_LORA], qg_ref[...])
    gt = proj[:, Q_LORA:]
    gate_ref[...] = (gt * _sigmoid_pair(gt)[0]).reshape(bblk, tt, B_WIDTH)
    qall = _dot(cq.astype(BF16), w_uq_ref[...])
    nope_w = B_HEADS * NOPE_DIM
    pair_w = (B_HEADS // 2) * LANES
    lane = lax.broadcasted_iota(jnp.int32, (rows, LANES), 1)
    for j in range(B_HEADS // 2):
        raw = qall[:, nope_w + j * LANES:nope_w + (j + 1) * LANES]
        rot = qall[:, nope_w + pair_w + j * LANES:nope_w + pair_w + (j + 1) * LANES]
        rp = (raw * cos2 + rot * sin2) * Q_SCALE
        for e in range(2):
            h = 2 * j + e
            ql = _dot(qall[:, h * NOPE_DIM:(h + 1) * NOPE_DIM].astype(BF16), w_ukt_ref[h]) * Q_SCALE
            keep = (lane < ROPE_DIM) if e == 0 else (lane >= ROPE_DIM)
            qc = jnp.concatenate([ql, jnp.where(keep, rp, 0.0)], axis=1)
            if transposed:
                q_ref[0, 0, :, h * tt:(h + 1) * tt] = qc.T.astype(BF16)
            else:
                qc = qc.astype(BF16)
                for bb in range(bblk):
                    q_ref[bb, h] = qc[bb * tt:(bb + 1) * tt]


def _mla_proj(x, cos2, sin2, w_dkv_ext, kv_g, w_in, q_g, w_uq_ext, w_ukt, *, bblk, tt, transposed, tag):
    bsz, t, _ = x.shape
    const2 = lambda b, i: (0, 0)
    const3 = lambda b, i: (0, 0, 0)
    tok = lambda w: pl.BlockSpec((bblk, tt, w), lambda b, i: (b, i, 0))
    out_specs = [tok(KV_LORA), tok(ROPE_DIM), tok(KCAT)]
    out_shape = [
        jax.ShapeDtypeStruct((bsz, t, KV_LORA), F32),
        jax.ShapeDtypeStruct((bsz, t, ROPE_DIM), F32),
        jax.ShapeDtypeStruct((bsz, t, KCAT), BF16),
    ]
    if transposed:
        assert bblk == 1
        out_specs += [pl.BlockSpec((1, 1, KCAT, tt), lambda b, i: (b, i, 0, 0)),
                      pl.BlockSpec((1, 1, KCAT, B_HEADS * tt), lambda b, i: (b, i, 0, 0))]
        out_shape += [jax.ShapeDtypeStruct((bsz, t // tt, KCAT, tt), BF16),
                      jax.ShapeDtypeStruct((bsz, t // tt, KCAT, B_HEADS * tt), BF16)]
    else:
        out_specs += [pl.BlockSpec((bblk, B_HEADS, tt, KCAT), lambda b, i: (b, 0, i, 0))]
        out_shape += [jax.ShapeDtypeStruct((bsz, B_HEADS, t, KCAT), BF16)]
    out_specs += [tok(B_WIDTH)]
    out_shape += [jax.ShapeDtypeStruct((bsz, t, B_WIDTH), F32)]
    kern = functools.partial(_mla_proj_kernel, bblk=bblk, tt=tt, transposed=transposed)
    return pl.pallas_call(
        kern,
        grid=(bsz // bblk, t // tt),
        in_specs=[
            tok(D_MODEL),
            pl.BlockSpec((tt, LANES), lambda b, i: (i, 0)),
            pl.BlockSpec((tt, LANES), lambda b, i: (i, 0)),
            pl.BlockSpec(w_dkv_ext.shape, const2),
            pl.BlockSpec((1, KV_LORA), const2),
            pl.BlockSpec(w_in.shape, const2),
            pl.BlockSpec((1, Q_LORA), const2),
            pl.BlockSpec(w_uq_ext.shape, const2),
            pl.BlockSpec(w_ukt.shape, const3),
        ],
        out_specs=out_specs,
        out_shape=out_shape,
        compiler_params=pltpu.CompilerParams(
            dimension_semantics=("arbitrary", "arbitrary"), vmem_limit_bytes=VMEM_LIMIT),
        name=f"mla_proj_{tag}",
    )(x, cos2, sin2, w_dkv_ext, kv_g, w_in, q_g, w_uq_ext, w_ukt)


def _softmax_step(s, v, m_ref, l_ref, acc_ref):
    m_prev = m_ref[...]
    m_new = jnp.maximum(m_prev, jnp.max(s, axis=1, keepdims=True))
    a = jnp.exp2(m_prev - m_new)
    p = jnp.exp2(s - m_new)
    l_ref[...] = a * l_ref[...] + jnp.sum(p, axis=1, keepdims=True)
    acc_ref[...] = a * acc_ref[...] + _dot(p.astype(BF16), v)
    m_ref[...] = m_new


def _attn_output(acc_ref, l_ref, o_ref, gate, x, w_uv_ref, w_out_ref, lng_ref, lnb_ref, tq):
    for h in range(B_HEADS):
        hs = slice(h * tq, (h + 1) * tq)
        o_lat = acc_ref[hs, :] / l_ref[hs, :]
        oh = _dot(o_lat.astype(BF16), w_uv_ref[h])
        o_ref[:, h * B_DV:(h + 1) * B_DV] = (oh * gate[:, h * B_DV:(h + 1) * B_DV]).astype(BF16)
    out = _dot(o_ref[...], w_out_ref[...])
    return _layer_norm(ALPHA * x + out, lng_ref[...], lnb_ref[...])


def _attn_prompt_kernel(qt_ref, kcat_ref, kt_ref, gate_ref, x_ref, w_uvt_ref, w_out_ref, lng_ref, lnb_ref,
                        y_ref, m_ref, l_ref, acc_ref, o_ref, *, tq, rc):
    i = pl.program_id(1)
    rows = B_HEADS * tq
    m_ref[...] = jnp.full((1, rows), NEG_INF, F32)
    l_ref[...] = jnp.zeros((1, rows), F32)
    acc_ref[...] = jnp.zeros((KV_LORA, rows), F32)

    def kv_step(j, bias):
        kb = kcat_ref[0, pl.ds(pl.multiple_of(j * tq, tq), tq), :]
        ct = kt_ref[0, j, 0:KV_LORA, :]
        n_rc = rows // rc

        def scores(r):
            return _dot(kb, qt_ref[0, 0, :, r * rc:(r + 1) * rc])

        def softmax(r, st):
            ls = slice(r * rc, (r + 1) * rc)
            if bias is not None:
                st = st + bias
            m_prev = m_ref[:, ls]
            m_new = jnp.maximum(m_prev, jnp.max(st, axis=0, keepdims=True))
            a = jnp.exp2(m_prev - m_new)
            pt = jnp.exp2(st - m_new)
            l_ref[:, ls] = a * l_ref[:, ls] + jnp.sum(pt, axis=0, keepdims=True)
            m_ref[:, ls] = m_new
            return a, pt.astype(BF16)

        st_cur = scores(0)
        pend = None
        for s in range(n_rc + 1):
            st_next = scores(s + 1) if s + 1 < n_rc else None
            if pend is not None:
                a_prev, pt_prev = pend
                pv = _dot(ct, pt_prev)
            new_pend = softmax(s, st_cur) if s < n_rc else None
            if pend is not None:
                ls = slice((s - 1) * rc, s * rc)
                acc_ref[:, ls] = a_prev * acc_ref[:, ls] + pv
            pend = new_pend
            st_cur = st_next

    def body(j, carry):
        kv_step(j, None)
        return carry

    lax.fori_loop(0, i, body, 0)

    key_chunk = lax.broadcasted_iota(jnp.int32, (tq, rc), 0) >> CHUNK_SHIFT
    q_chunk = (lax.broadcasted_iota(jnp.int32, (tq, rc), 1) & (tq - 1)) >> CHUNK_SHIFT
    kv_step(i, jnp.where(key_chunk <= q_chunk, 0.0, NEG_INF))

    gate = gate_ref[0]
    for h in range(B_HEADS):
        hs = slice(h * tq, (h + 1) * tq)
        o_lat = acc_ref[:, hs] / l_ref[:, hs]
        oh = _dot(w_uvt_ref[h], o_lat.astype(BF16)).T
        o_ref[:, h * B_DV:(h + 1) * B_DV] = (oh * gate[:, h * B_DV:(h + 1) * B_DV]).astype(BF16)
    out = _dot(o_ref[...], w_out_ref[...])
    y_ref[0] = _layer_norm(ALPHA * x_ref[0] + out, lng_ref[...], lnb_ref[...])


def _attn_prompt(qt, kcat, kt, gate, x, w_uvt_h, w_out, ln_g, ln_b, *, rc):
    bsz, nq, _, rows = qt.shape
    tq = rows // B_HEADS
    t = kcat.shape[1]
    assert kt.shape[-1] == tq and tq & (tq - 1) == 0 and tq % CHUNK == 0 and rc % tq == 0
    const2 = lambda b, i: (0, 0)
    kern = functools.partial(_attn_prompt_kernel, tq=tq, rc=rc)
    return pl.pallas_call(
        kern,
        grid=(bsz, nq),
        in_specs=[
            pl.BlockSpec((1, 1, KCAT, rows), lambda b, i: (b, i, 0, 0)),
            pl.BlockSpec((1, t, KCAT), lambda b, i: (b, 0, 0)),
            pl.BlockSpec((1, nq, KCAT, tq), lambda b, i: (b, 0, 0, 0)),
            pl.BlockSpec((1, tq, B_WIDTH), lambda b, i: (b, i, 0)),
            pl.BlockSpec((1, tq, D_MODEL), lambda b, i: (b, i, 0)),
            pl.BlockSpec(w_uvt_h.shape, lambda b, i: (0, 0, 0)),
            pl.BlockSpec(w_out.shape, const2),
            pl.BlockSpec((1, D_MODEL), const2),
            pl.BlockSpec((1, D_MODEL), const2),
        ],
        out_specs=pl.BlockSpec((1, tq, D_MODEL), lambda b, i: (b, i, 0)),
        out_shape=jax.ShapeDtypeStruct((bsz, t, D_MODEL), F32),
        scratch_shapes=[
            pltpu.VMEM((1, rows), F32),
            pltpu.VMEM((1, rows), F32),
            pltpu.VMEM((KV_LORA, rows), F32),
            pltpu.VMEM((tq, B_WIDTH), BF16),
        ],
        compiler_params=pltpu.CompilerParams(
            dimension_semantics=("arbitrary", "arbitrary"), vmem_limit_bytes=VMEM_LIMIT),
        name="mla_attend_prompt",
    )(qt, kcat, kt, gate, x, w_uvt_h, w_out, ln_g, ln_b)


def _attn_sample_kernel(q_ref, cc_ref, ckr_ref, kn_ref, gate_ref, x_ref, w_uv_ref, w_out_ref,
                        lng_ref, lnb_ref, y_ref, m_ref, l_ref, acc_ref, o_ref, *, tq, tk):
    rows = B_HEADS * tq
    past = cc_ref.shape[1]
    q = q_ref[0].reshape(rows, KCAT)
    q_lat = q[:, :KV_LORA]
    q3 = q[:, KV_LORA:].astype(F32)
    q_rope = (q3[:, :ROPE_DIM] + q3[:, ROPE_DIM:]).astype(BF16)
    m_ref[...] = jnp.full((rows, 1), NEG_INF, F32)
    l_ref[...] = jnp.zeros((rows, 1), F32)
    acc_ref[...] = jnp.zeros((rows, KV_LORA), F32)

    for j in range(past // tk):
        cb = cc_ref[0, j * tk:(j + 1) * tk, :].astype(BF16)
        krb = ckr_ref[0, j * tk:(j + 1) * tk, :].astype(BF16)
        s = (lax.dot_general(q_lat, cb, _NT, preferred_element_type=F32)
             + lax.dot_general(q_rope, krb, _NT, preferred_element_type=F32))
        _softmax_step(s, cb, m_ref, l_ref, acc_ref)

    kn = kn_ref[0]
    s = lax.dot_general(q, kn, _NT, preferred_element_type=F32)
    _softmax_step(s, kn[:, :KV_LORA], m_ref, l_ref, acc_ref)

    y_ref[0] = _attn_output(acc_ref, l_ref, o_ref, gate_ref[0], x_ref[0],
                            w_uv_ref, w_out_ref, lng_ref, lnb_ref, tq)


def _attn_sample(qcat, cache_c, cache_kr, kcat_new, gate, x, w_uv_h, w_out, ln_g, ln_b, *, tk):
    bsz, _, tq, _ = qcat.shape
    past = cache_c.shape[1]
    rows = B_HEADS * tq
    const2 = lambda b: (0, 0)
    per_b = lambda shape: pl.BlockSpec((1,) + shape, lambda b: (b,) + (0,) * len(shape))
    kern = functools.partial(_attn_sample_kernel, tq=tq, tk=tk)
    return pl.pallas_call(
        kern,
        grid=(bsz,),
        in_specs=[
            per_b((B_HEADS, tq, KCAT)),
            per_b((past, KV_LORA)),
            per_b((past, ROPE_DIM)),
            per_b((tq, KCAT)),
            per_b((tq, B_WIDTH)),
            per_b((tq, D_MODEL)),
            pl.BlockSpec(w_uv_h.shape, lambda b: (0, 0, 0)),
            pl.BlockSpec(w_out.shape, const2),
            pl.BlockSpec((1, D_MODEL), const2),
            pl.BlockSpec((1, D_MODEL), const2),
        ],
        out_specs=per_b((tq, D_MODEL)),
        out_shape=jax.ShapeDtypeStruct((bsz, tq, D_MODEL), F32),
        scratch_shapes=[
            pltpu.VMEM((rows, 1), F32),
            pltpu.VMEM((rows, 1), F32),
            pltpu.VMEM((rows, KV_LORA), F32),
            pltpu.VMEM((tq, B_WIDTH), BF16),
        ],
        compiler_params=pltpu.CompilerParams(
            dimension_semantics=("arbitrary",), vmem_limit_bytes=VMEM_LIMIT),
        name="mla_attend_sample",
    )(qcat, cache_c, cache_kr, kcat_new, gate, x, w_uv_h, w_out, ln_g, ln_b)


def _rot_cols(w):
    half = ROPE_DIM // 2
    return jnp.concatenate([-w[..., half:], w[..., :half]], axis=-1)


def _rope_tables(pos):
    half = ROPE_DIM // 2
    inv = jnp.power(ROPE_BASE, -jnp.arange(half, dtype=F32) / half)
    ang = pos.astype(F32)[:, None] * inv[None, :]
    cos, sin = jnp.cos(ang), jnp.sin(ang)
    return jnp.concatenate([cos] * 4, axis=1), jnp.concatenate([sin] * 4, axis=1)


def kernel(x_prompt, x_sample, state_hgrn, cache_ckv, cache_krope, w_in_a, lb_gamma, a_norm_g, w_out_a,
           w_dkv, kv_norm_g, w_uk, w_uv, w_in_b, q_norm_g, w_uq, w_out_b, ln_g, ln_b):
    assert N_A == 1 and DEPTH == 2
    t_p = x_prompt.shape[1]
    t_s = x_sample.shape[1]
    past = cache_ckv.shape[1]
    row = lambda a: a.reshape(1, -1)

    w_in_a_b = w_in_a[0].astype(BF16)
    w_out_a_b = w_out_a[0].astype(BF16)
    kr_w = w_dkv[:, KV_LORA:]
    w_dkv_ext = jnp.concatenate(
        [w_dkv[:, :KV_LORA], kr_w, kr_w, _rot_cols(kr_w), _rot_cols(kr_w)], axis=1).astype(BF16)
    w_uq3 = w_uq[0].reshape(Q_LORA, B_HEADS, NOPE_DIM + ROPE_DIM)
    q_rope_w = w_uq3[:, :, NOPE_DIM:]
    w_uq_ext = jnp.concatenate([
        w_uq3[:, :, :NOPE_DIM].reshape(Q_LORA, B_HEADS * NOPE_DIM),
        q_rope_w.reshape(Q_LORA, B_HEADS * ROPE_DIM),
        _rot_cols(q_rope_w).reshape(Q_LORA, B_HEADS * ROPE_DIM)], axis=1).astype(BF16)
    w_ukt = jnp.transpose(w_uk, (1, 2, 0)).astype(BF16)
    w_uv_h = jnp.transpose(w_uv, (1, 0, 2)).astype(BF16)
    w_uvt_h = jnp.transpose(w_uv, (1, 2, 0)).astype(BF16)
    w_in_b_b = w_in_b[0].astype(BF16)
    w_out_b_b = w_out_b[0].astype(BF16)

    outs = []
    for x, s0, pos0, prompt in ((x_prompt, None, 0, True), (x_sample, state_hgrn[0], past, False)):
        bsz, t, _ = x.shape
        if prompt:
            bblk, tt, chunk, tm = 1, 256, CHUNK, 256
        else:
            bblk, tt, chunk, tm = bsz, t, t, t
        x1, s_fin = _hgrn_layer(x, s0, w_in_a_b, lb_gamma, row(a_norm_g[0]), w_out_a_b,
                                row(ln_g[0]), row(ln_b[0]), layer=0, bblk=bblk, tt=tt, chunk=chunk)
        cos2, sin2 = _rope_tables(pos0 + jnp.arange(t, dtype=jnp.int32))
        proj_outs = _mla_proj(
            x1, cos2, sin2, w_dkv_ext, row(kv_norm_g), w_in_b_b, row(q_norm_g[0]), w_uq_ext, w_ukt,
            bblk=bblk, tt=tm, transposed=prompt, tag="prompt" if prompt else "sample")
        if prompt:
            ckv, krope, kcat, kt, qt, gate = proj_outs
            y = _attn_prompt(qt, kcat, kt, gate, x1, w_uvt_h, w_out_b_b, row(ln_g[1]), row(ln_b[1]), rc=512)
        else:
            ckv, krope, kcat, qcat, gate = proj_outs
            y = _attn_sample(qcat, cache_ckv, cache_krope, kcat, gate, x1, w_uv_h, w_out_b_b,
                             row(ln_g[1]), row(ln_b[1]), tk=512)
        outs.append((y, s_fin[None], ckv, krope))
    (y_p, s_p, c_p, kr_p), (y_s, s_s, c_s, kr_s) = outs
    return (y_p, y_s, s_p, c_p, kr_p, s_s, c_s, kr_s)
```

```python
import functools

import jax
import jax.numpy as jnp
from jax import lax
from jax.experimental import pallas as pl
from jax.experimental.pallas import tpu as pltpu

F32 = jnp.float32
BF16 = jnp.bfloat16

D_MODEL = 1024
DEPTH = 2
CHUNK = 64
CHUNK_SHIFT = 6
N_A = DEPTH // 2
A_HEADS = 8
A_DK = 128
A_DV = 128
A_WIDTH = A_HEADS * A_DV
B_HEADS = 16
Q_LORA = 512
KV_LORA = 256
NOPE_DIM = 128
ROPE_DIM = 64
B_DV = 128
B_WIDTH = B_HEADS * B_DV
ROPE_BASE = 10000.0
EPS = 1e-6
NEG_INF = -1e30
ATTN_SCALE = (NOPE_DIM + ROPE_DIM) ** -0.5
LOG2_E = 1.4426950408889634
Q_SCALE = ATTN_SCALE * LOG2_E
ALPHA = (2 * DEPTH) ** 0.25

LANES = 128
KCAT = KV_LORA + 2 * ROPE_DIM
VMEM_LIMIT = 56 * 1024 * 1024

_NT = (((1,), (1,)), ((), ()))
_TN = (((0,), (0,)), ((), ()))


def _dot(a, b):
    return jnp.dot(a, b, preferred_element_type=F32)


def _layer_norm(x, g, b):
    mu = jnp.mean(x, axis=-1, keepdims=True)
    xc = x - mu
    var = jnp.mean(xc * xc, axis=-1, keepdims=True)
    return xc * lax.rsqrt(var + EPS) * g + b


def _rms_norm(x, g):
    return x * lax.rsqrt(jnp.mean(x * x, axis=-1, keepdims=True) + EPS) * g


def _sigmoid_pair(z):
    e = jnp.exp(-jnp.abs(z))
    r = 1.0 / (1.0 + e)
    er = e * r
    pos = z >= 0
    return jnp.where(pos, r, er), jnp.where(pos, er, r)


def _split3(x):
    hi = x.astype(BF16)
    r1 = x - hi.astype(F32)
    mid = r1.astype(BF16)
    lo = (r1 - mid.astype(F32)).astype(BF16)
    return hi, mid, lo


def _hgrn_kernel(x_ref, s0_ref, w_in_ref, lbg_ref, ng_ref, w_out_ref, lng_ref, lnb_ref, tri_ref,
                 y_ref, sfin_ref, st_ref, o_ref, *, layer, bblk, tt, chunk, zero_init):
    t = pl.program_id(1)
    rows = bblk * tt
    mid = chunk // 2

    @pl.when(t == 0)
    def _():
        for bb in range(bblk):
            for h in range(A_HEADS):
                if zero_init:
                    st_ref[bb, h] = jnp.zeros((A_DV, A_DK), F32)
                else:
                    st_ref[bb, h] = s0_ref[bb, h].T

    x = x_ref[...].reshape(rows, D_MODEL)
    proj = _dot(x.astype(BF16), w_in_ref[...])
    q = proj[:, 0:A_WIDTH]
    z = proj[:, A_WIDTH:2 * A_WIDTH]
    v = proj[:, 2 * A_WIDTH:3 * A_WIDTH].astype(BF16)
    g = proj[:, 3 * A_WIDTH:4 * A_WIDTH]

    lbg = lbg_ref[...]
    e = jnp.exp(lbg - jnp.max(lbg, axis=0, keepdims=True))
    lb = jnp.sum(e[0:layer + 1], axis=0, keepdims=True) / jnp.sum(e, axis=0, keepdims=True)
    one_m_lb = 1.0 - lb

    sig_pos, sig_neg = _sigmoid_pair(z)
    logf = jnp.log(lb + one_m_lb * sig_pos)
    k = one_m_lb * sig_neg

    hi, m_, lo = _split3(logf)
    b = _dot(tri_ref[...], jnp.concatenate([hi, m_, lo], axis=0))

    silu_g = g * _sigmoid_pair(g)[0]
    ng = ng_ref[...]
    li = lax.broadcasted_iota(jnp.int32, (chunk, chunk), 0)
    si = lax.broadcasted_iota(jnp.int32, (chunk, chunk), 1)
    causal = li >= si

    for bb in range(bblk):
        for c in range(tt // chunk):
            r0 = bb * tt + c * chunk
            rs = slice(r0, r0 + chunk)
            bc = b[rs]
            bmid = bc[mid:mid + 1]
            blast = bc[chunk - 1:chunk]
            qc = q[rs]
            kc = k[rs]
            q_in = (qc * jnp.exp(bc - bmid)).astype(BF16)
            k_in = (kc * jnp.exp(bmid - bc)).astype(BF16)
            q_ex = (qc * jnp.exp(bc)).astype(BF16)
            k_dec = (kc * jnp.exp(blast - bc)).astype(BF16)
            decay = jnp.exp(blast)
            vc = v[rs]
            for h in range(A_HEADS):
                hs = slice(h * A_DK, (h + 1) * A_DK)
                st = st_ref[bb, h]
                sc = lax.dot_general(q_in[:, hs], k_in[:, hs], _NT, preferred_element_type=F32)
                sc = jnp.where(causal, sc, 0.0).astype(BF16)
                o = _dot(sc, vc[:, hs]) + lax.dot_general(
                    q_ex[:, hs], st.astype(BF16), _NT, preferred_element_type=F32)
                st_ref[bb, h] = st * decay[:, hs] + lax.dot_general(
                    vc[:, hs], k_dec[:, hs], _TN, preferred_element_type=F32)
                o = _rms_norm(o, ng) * silu_g[rs, hs]
                o_ref[rs, hs] = o.astype(BF16)

    out = _dot(o_ref[...], w_out_ref[...])
    y = _layer_norm(ALPHA * x + out, lng_ref[...], lnb_ref[...])
    y_ref[...] = y.reshape(bblk, tt, D_MODEL)

    @pl.when(t == pl.num_programs(1) - 1)
    def _():
        for bb in range(bblk):
            for h in range(A_HEADS):
                sfin_ref[bb, h] = st_ref[bb, h].T


def _hgrn_layer(x, s0, w_in, lb_gamma, norm_g, w_out, ln_g, ln_b, *, layer, bblk, tt, chunk):
    bsz, t, _ = x.shape
    rows = bblk * tt
    zero_init = s0 is None
    if zero_init:
        s0 = jnp.zeros((bblk, A_HEADS, 8, LANES), F32)
        s0_spec = pl.BlockSpec((bblk, A_HEADS, 8, LANES), lambda b, i: (0, 0, 0, 0))
    else:
        s0_spec = pl.BlockSpec((bblk, A_HEADS, A_DK, A_DV), lambda b, i: (b, 0, 0, 0))
    r = jnp.arange(rows)
    tri = ((r[:, None] // chunk == r[None, :] // chunk) & (r[:, None] >= r[None, :])).astype(BF16)
    tri3 = jnp.concatenate([tri, tri, tri], axis=1)
    const = lambda b, i: (0, 0)
    kern = functools.partial(_hgrn_kernel, layer=layer, bblk=bblk, tt=tt, chunk=chunk, zero_init=zero_init)
    return pl.pallas_call(
        kern,
        grid=(bsz // bblk, t // tt),
        in_specs=[
            pl.BlockSpec((bblk, tt, D_MODEL), lambda b, i: (b, i, 0)),
            s0_spec,
            pl.BlockSpec((D_MODEL, 4 * A_WIDTH), const),
            pl.BlockSpec(lb_gamma.shape, const),
            pl.BlockSpec((1, A_DV), const),
            pl.BlockSpec((A_WIDTH, D_MODEL), const),
            pl.BlockSpec((1, D_MODEL), const),
            pl.BlockSpec((1, D_MODEL), const),
            pl.BlockSpec((rows, 3 * rows), const),
        ],
        out_specs=[
            pl.BlockSpec((bblk, tt, D_MODEL), lambda b, i: (b, i, 0)),
            pl.BlockSpec((bblk, A_HEADS, A_DK, A_DV), lambda b, i: (b, 0, 0, 0)),
        ],
        out_shape=[
            jax.ShapeDtypeStruct((bsz, t, D_MODEL), F32),
            jax.ShapeDtypeStruct((bsz, A_HEADS, A_DK, A_DV), F32),
        ],
        scratch_shapes=[
            pltpu.VMEM((bblk, A_HEADS, A_DV, A_DK), F32),
            pltpu.VMEM((rows, A_WIDTH), BF16),
        ],
        compiler_params=pltpu.CompilerParams(
            dimension_semantics=("arbitrary", "arbitrary"), vmem_limit_bytes=VMEM_LIMIT),
        name=f"hgrn_layer_{'prompt' if zero_init else 'sample'}",
    )(x, s0, w_in, lb_gamma, norm_g, w_out, ln_g, ln_b, tri3)


def _latent_kv(xb, cos2, sin2, w_dkv_ref, kvg_ref, ckv_ref, krope_ref, kcat_ref, bblk, tt):
    kv = _dot(xb, w_dkv_ref[...])
    c = _rms_norm(kv[:, :KV_LORA], kvg_ref[...])
    krd = kv[:, KV_LORA:KV_LORA + LANES] * cos2 + kv[:, KV_LORA + LANES:KV_LORA + 2 * LANES] * sin2
    ckv_ref[...] = c.reshape(bblk, tt, KV_LORA)
    krope_ref[...] = krd[:, :ROPE_DIM].reshape(bblk, tt, ROPE_DIM)
    kcat = jnp.concatenate([c, krd], axis=1)
    kcat_ref[...] = kcat.astype(BF16).reshape(bblk, tt, KCAT)
    return kcat


def _query_latent(xb, w_in_ref, qg_ref, gate_ref, bblk, tt):
    proj = _dot(xb, w_in_ref[...])
    cq = _rms_norm(proj[:, :Q_LORA], qg_ref[...])
    gt = proj[:, Q_LORA:]
    gate_ref[...] = (gt * _sigmoid_pair(gt)[0]).reshape(bblk, tt, B_WIDTH)
    return cq


def _mla_proj_rows_kernel(x_ref, cos_ref, sin_ref, w_dkv_ref, kvg_ref, w_in_ref, qg_ref, w_uq_ref, w_ukt_ref,
                          ckv_ref, krope_ref, kcat_ref, q_ref, gate_ref, *, bblk, tt):
    rows = bblk * tt
    xb = x_ref[...].reshape(rows, D_MODEL).astype(BF16)
    cos2 = jnp.concatenate([cos_ref[...]] * bblk, axis=0)
    sin2 = jnp.concatenate([sin_ref[...]] * bblk, axis=0)
    _latent_kv(xb, cos2, sin2, w_dkv_ref, kvg_ref, ckv_ref, krope_ref, kcat_ref, bblk, tt)
    cq = _query_latent(xb, w_in_ref, qg_ref, gate_ref, bblk, tt)
    qall = _dot(cq.astype(BF16), w_uq_ref[...])
    nope_w = B_HEADS * NOPE_DIM
    pair_w = (B_HEADS // 2) * LANES
    lane = lax.broadcasted_iota(jnp.int32, (rows, LANES), 1)
    for j in range(B_HEADS // 2):
        raw = qall[:, nope_w + j * LANES:nope_w + (j + 1) * LANES]
        rot = qall[:, nope_w + pair_w + j * LANES:nope_w + pair_w + (j + 1) * LANES]
        rp = (raw * cos2 + rot * sin2) * Q_SCALE
        for e in range(2):
            h = 2 * j + e
            ql = _dot(qall[:, h * NOPE_DIM:(h + 1) * NOPE_DIM].astype(BF16), w_ukt_ref[h]) * Q_SCALE
            keep = (lane < ROPE_DIM) if e == 0 else (lane >= ROPE_DIM)
            qc = jnp.concatenate([ql, jnp.where(keep, rp, 0.0)], axis=1).astype(BF16)
            for bb in range(bblk):
                q_ref[bb, h] = qc[bb * tt:(bb + 1) * tt]


def _mla_proj_cols_kernel(x_ref, cos_ref, sin_ref, cost_ref, sint_ref, w_dkv_ref, kvg_ref, w_in_ref, qg_ref,
                          w_uqt_ref, w_uk_ref, ckv_ref, krope_ref, kcat_ref, kt_ref, q_ref, gate_ref, *, tt):
    xb = x_ref[0].astype(BF16)
    kcat = _latent_kv(xb, cos_ref[...], sin_ref[...], w_dkv_ref, kvg_ref, ckv_ref, krope_ref, kcat_ref, 1, tt)
    kt_ref[0, 0] = kcat.T.astype(BF16)
    cq = _query_latent(xb, w_in_ref, qg_ref, gate_ref, 1, tt)
    qall_t = _dot(w_uqt_ref[...], cq.T.astype(BF16))
    nope_w = B_HEADS * NOPE_DIM
    half = ROPE_DIM // 2
    cos_t = cost_ref[...]
    sin_t = sint_ref[...]
    zeros = jnp.zeros((ROPE_DIM, tt), F32)
    for j in range(B_HEADS // 2):
        raw = qall_t[nope_w + j * LANES:nope_w + (j + 1) * LANES]
        rot = jnp.concatenate([-raw[half:2 * half], raw[0:half], -raw[3 * half:4 * half], raw[2 * half:3 * half]],
                              axis=0)
        rp = (raw * cos_t + rot * sin_t) * Q_SCALE
        for e in range(2):
            h = 2 * j + e
            ql = _dot(w_uk_ref[h], qall_t[h * NOPE_DIM:(h + 1) * NOPE_DIM].astype(BF16)) * Q_SCALE
            rope_rows = [rp[:ROPE_DIM], zeros] if e == 0 else [zeros, rp[ROPE_DIM:]]
            q_ref[0, 0, :, h * tt:(h + 1) * tt] = jnp.concatenate([ql] + rope_rows, axis=0).astype(BF16)


def _mla_proj(x, rope_tabs, w_dkv_ext, kv_g, w_in, q_g, w_q, w_k, *, bblk, tt, transposed, tag):
    bsz, t, _ = x.shape
    const2 = lambda b, i: (0, 0)
    const3 = lambda b, i: (0, 0, 0)
    tok = lambda w: pl.BlockSpec((bblk, tt, w), lambda b, i: (b, i, 0))
    tab = pl.BlockSpec((tt, LANES), lambda b, i: (i, 0))
    in_specs = [tok(D_MODEL), tab, tab]
    out_specs = [tok(KV_LORA), tok(ROPE_DIM), tok(KCAT)]
    out_shape = [
        jax.ShapeDtypeStruct((bsz, t, KV_LORA), F32),
        jax.ShapeDtypeStruct((bsz, t, ROPE_DIM), F32),
        jax.ShapeDtypeStruct((bsz, t, KCAT), BF16),
    ]
    if transposed:
        assert bblk == 1
        tab_t = pl.BlockSpec((LANES, tt), lambda b, i: (0, i))
        in_specs += [tab_t, tab_t]
        out_specs += [pl.BlockSpec((1, 1, KCAT, tt), lambda b, i: (b, i, 0, 0)),
                      pl.BlockSpec((1, 1, KCAT, B_HEADS * tt), lambda b, i: (b, i, 0, 0))]
        out_shape += [jax.ShapeDtypeStruct((bsz, t // tt, KCAT, tt), BF16),
                      jax.ShapeDtypeStruct((bsz, t // tt, KCAT, B_HEADS * tt), BF16)]
        kern = functools.partial(_mla_proj_cols_kernel, tt=tt)
    else:
        out_specs += [pl.BlockSpec((bblk, B_HEADS, tt, KCAT), lambda b, i: (b, 0, i, 0))]
        out_shape += [jax.ShapeDtypeStruct((bsz, B_HEADS, t, KCAT), BF16)]
        kern = functools.partial(_mla_proj_rows_kernel, bblk=bblk, tt=tt)
    in_specs += [
        pl.BlockSpec(w_dkv_ext.shape, const2),
        pl.BlockSpec((1, KV_LORA), const2),
        pl.BlockSpec(w_in.shape, const2),
        pl.BlockSpec((1, Q_LORA), const2),
        pl.BlockSpec(w_q.shape, const2),
        pl.BlockSpec(w_k.shape, const3),
    ]
    out_specs += [tok(B_WIDTH)]
    out_shape += [jax.ShapeDtypeStruct((bsz, t, B_WIDTH), F32)]
    return pl.pallas_call(
        kern,
        grid=(bsz // bblk, t // tt),
        in_specs=in_specs,
        out_specs=out_specs,
        out_shape=out_shape,
        compiler_params=pltpu.CompilerParams(
            dimension_semantics=("arbitrary", "arbitrary"), vmem_limit_bytes=VMEM_LIMIT),
        name=f"mla_proj_{tag}",
    )(x, *rope_tabs, w_dkv_ext, kv_g, w_in, q_g, w_q, w_k)


def _softmax_step(s, v, m_ref, l_ref, acc_ref):
    m_prev = m_ref[...]
    m_new = jnp.maximum(m_prev, jnp.max(s, axis=1, keepdims=True))
    a = jnp.exp2(m_prev - m_new)
    p = jnp.exp2(s - m_new)
    l_ref[...] = a * l_ref[...] + jnp.sum(p, axis=1, keepdims=True)
    acc_ref[...] = a * acc_ref[...] + _dot(p.astype(BF16), v)
    m_ref[...] = m_new


def _attn_output(acc_ref, l_ref, o_ref, gate, x, w_uv_ref, w_out_ref, lng_ref, lnb_ref, tq):
    for h in range(B_HEADS):
        hs = slice(h * tq, (h + 1) * tq)
        o_lat = acc_ref[hs, :] / l_ref[hs, :]
        oh = _dot(o_lat.astype(BF16), w_uv_ref[h])
        o_ref[:, h * B_DV:(h + 1) * B_DV] = (oh * gate[:, h * B_DV:(h + 1) * B_DV]).astype(BF16)
    out = _dot(o_ref[...], w_out_ref[...])
    return _layer_norm(ALPHA * x + out, lng_ref[...], lnb_ref[...])


def _attn_prompt_kernel(qt_ref, kcat_ref, kt_ref, gate_ref, x_ref, w_uvt_ref, w_out_ref, lng_ref, lnb_ref,
                        y_ref, m_ref, l_ref, acc_ref, o_ref, *, tq, rc):
    i = pl.program_id(1)
    rows = B_HEADS * tq
    m_ref[...] = jnp.full((1, rows), NEG_INF, F32)
    l_ref[...] = jnp.zeros((1, rows), F32)
    acc_ref[...] = jnp.zeros((KV_LORA, rows), F32)

    def kv_step(j, visible):
        kb = kcat_ref[0, pl.ds(pl.multiple_of(j * tq, tq), tq), :]
        ct = kt_ref[0, j, 0:KV_LORA, :]
        n_rc = rows // rc

        def scores(r):
            return _dot(kb, qt_ref[0, 0, :, r * rc:(r + 1) * rc])

        def softmax(r, st):
            ls = slice(r * rc, (r + 1) * rc)
            if visible is not None:
                st = jnp.where(visible, st, NEG_INF)
            m_prev = m_ref[:, ls]
            m_new = jnp.maximum(m_prev, jnp.max(st, axis=0, keepdims=True))
            a = jnp.exp2(m_prev - m_new)
            pt = jnp.exp2(st - m_new)
            l_ref[:, ls] = a * l_ref[:, ls] + jnp.sum(pt, axis=0, keepdims=True)
            m_ref[:, ls] = m_new
            return a, pt.astype(BF16)

        st_cur = scores(0)
        pend = None
        for s in range(n_rc + 1):
            st_next = scores(s + 1) if s + 1 < n_rc else None
            if pend is not None:
                a_prev, pt_prev = pend
                pv = _dot(ct, pt_prev)
            new_pend = softmax(s, st_cur) if s < n_rc else None
            if pend is not None:
                ls = slice((s - 1) * rc, s * rc)
                acc_ref[:, ls] = a_prev * acc_ref[:, ls] + pv
            pend = new_pend
            st_cur = st_next

    def body(j, carry):
        kv_step(j, None)
        return carry

    lax.fori_loop(0, i, body, 0)

    key_chunk = lax.broadcasted_iota(jnp.int32, (tq, rc), 0) >> CHUNK_SHIFT
    q_chunk = (lax.broadcasted_iota(jnp.int32, (tq, rc), 1) & (tq - 1)) >> CHUNK_SHIFT
    visible = key_chunk <= q_chunk

    def body_diag(j, carry):
        kv_step(j, visible)
        return carry

    lax.fori_loop(i, i + 1, body_diag, 0)

    gate = gate_ref[0]
    for h in range(B_HEADS):
        hs = slice(h * tq, (h + 1) * tq)
        o_lat = acc_ref[:, hs] / l_ref[:, hs]
        oh = _dot(w_uvt_ref[h], o_lat.astype(BF16)).T
        o_ref[:, h * B_DV:(h + 1) * B_DV] = (oh * gate[:, h * B_DV:(h + 1) * B_DV]).astype(BF16)
    out = _dot(o_ref[...], w_out_ref[...])
    y_ref[0] = _layer_norm(ALPHA * x_ref[0] + out, lng_ref[...], lnb_ref[...])


def _attn_prompt(qt, kcat, kt, gate, x, w_uvt_h, w_out, ln_g, ln_b, *, rc):
    bsz, nq, _, rows = qt.shape
    tq = rows // B_HEADS
    t = kcat.shape[1]
    assert kt.shape[-1] == tq and tq & (tq - 1) == 0 and tq % CHUNK == 0 and rc % tq == 0
    const2 = lambda b, i: (0, 0)
    kern = functools.partial(_attn_prompt_kernel, tq=tq, rc=rc)
    return pl.pallas_call(
        kern,
        grid=(bsz, nq),
        in_specs=[
            pl.BlockSpec((1, 1, KCAT, rows), lambda b, i: (b, i, 0, 0)),
            pl.BlockSpec((1, t, KCAT), lambda b, i: (b, 0, 0)),
            pl.BlockSpec((1, nq, KCAT, tq), lambda b, i: (b, 0, 0, 0)),
            pl.BlockSpec((1, tq, B_WIDTH), lambda b, i: (b, i, 0)),
            pl.BlockSpec((1, tq, D_MODEL), lambda b, i: (b, i, 0)),
            pl.BlockSpec(w_uvt_h.shape, lambda b, i: (0, 0, 0)),
            pl.BlockSpec(w_out.shape, const2),
            pl.BlockSpec((1, D_MODEL), const2),
            pl.BlockSpec((1, D_MODEL), const2),
        ],
        out_specs=pl.BlockSpec((1, tq, D_MODEL), lambda b, i: (b, i, 0)),
        out_shape=jax.ShapeDtypeStruct((bsz, t, D_MODEL), F32),
        scratch_shapes=[
            pltpu.VMEM((1, rows), F32),
            pltpu.VMEM((1, rows), F32),
            pltpu.VMEM((KV_LORA, rows), F32),
            pltpu.VMEM((tq, B_WIDTH), BF16),
        ],
        compiler_params=pltpu.CompilerParams(
            dimension_semantics=("arbitrary", "arbitrary"), vmem_limit_bytes=VMEM_LIMIT),
        name="mla_attend_prompt",
    )(qt, kcat, kt, gate, x, w_uvt_h, w_out, ln_g, ln_b)


def _attn_sample_kernel(q_ref, cc_ref, ckr_ref, kn_ref, gate_ref, x_ref, w_uv_ref, w_out_ref,
                        lng_ref, lnb_ref, y_ref, m_ref, l_ref, acc_ref, o_ref, *, tq, tk):
    rows = B_HEADS * tq
    past = cc_ref.shape[1]
    q = q_ref[0].reshape(rows, KCAT)
    q_lat = q[:, :KV_LORA]
    q3 = q[:, KV_LORA:].astype(F32)
    q_rope = (q3[:, :ROPE_DIM] + q3[:, ROPE_DIM:]).astype(BF16)
    m_ref[...] = jnp.full((rows, 1), NEG_INF, F32)
    l_ref[...] = jnp.zeros((rows, 1), F32)
    acc_ref[...] = jnp.zeros((rows, KV_LORA), F32)

    for j in range(past // tk):
        cb = cc_ref[0, j * tk:(j + 1) * tk, :].astype(BF16)
        krb = ckr_ref[0, j * tk:(j + 1) * tk, :].astype(BF16)
        s = (lax.dot_general(q_lat, cb, _NT, preferred_element_type=F32)
             + lax.dot_general(q_rope, krb, _NT, preferred_element_type=F32))
        _softmax_step(s, cb, m_ref, l_ref, acc_ref)

    kn = kn_ref[0]
    s = lax.dot_general(q, kn, _NT, preferred_element_type=F32)
    _softmax_step(s, kn[:, :KV_LORA], m_ref, l_ref, acc_ref)

    y_ref[0] = _attn_output(acc_ref, l_ref, o_ref, gate_ref[0], x_ref[0],
                            w_uv_ref, w_out_ref, lng_ref, lnb_ref, tq)


def _attn_sample(qcat, cache_c, cache_kr, kcat_new, gate, x, w_uv_h, w_out, ln_g, ln_b, *, tk):
    bsz, _, tq, _ = qcat.shape
    past = cache_c.shape[1]
    rows = B_HEADS * tq
    const2 = lambda b: (0, 0)
    per_b = lambda shape: pl.BlockSpec((1,) + shape, lambda b: (b,) + (0,) * len(shape))
    kern = functools.partial(_attn_sample_kernel, tq=tq, tk=tk)
    return pl.pallas_call(
        kern,
        grid=(bsz,),
        in_specs=[
            per_b((B_HEADS, tq, KCAT)),
            per_b((past, KV_LORA)),
            per_b((past, ROPE_DIM)),
            per_b((tq, KCAT)),
            per_b((tq, B_WIDTH)),
            per_b((tq, D_MODEL)),
            pl.BlockSpec(w_uv_h.shape, lambda b: (0, 0, 0)),
            pl.BlockSpec(w_out.shape, const2),
            pl.BlockSpec((1, D_MODEL), const2),
            pl.BlockSpec((1, D_MODEL), const2),
        ],
        out_specs=per_b((tq, D_MODEL)),
        out_shape=jax.ShapeDtypeStruct((bsz, tq, D_MODEL), F32),
        scratch_shapes=[
            pltpu.VMEM((rows, 1), F32),
            pltpu.VMEM((rows, 1), F32),
            pltpu.VMEM((rows, KV_LORA), F32),
            pltpu.VMEM((tq, B_WIDTH), BF16),
        ],
        compiler_params=pltpu.CompilerParams(
            dimension_semantics=("arbitrary",), vmem_limit_bytes=VMEM_LIMIT),
        name="mla_attend_sample",
    )(qcat, cache_c, cache_kr, kcat_new, gate, x, w_uv_h, w_out, ln_g, ln_b)


def _rot_cols(w):
    half = ROPE_DIM // 2
    return jnp.concatenate([-w[..., half:], w[..., :half]], axis=-1)


def _rope_tables(pos):
    half = ROPE_DIM // 2
    inv = jnp.power(ROPE_BASE, -jnp.arange(half, dtype=F32) / half)
    ang = pos.astype(F32)[:, None] * inv[None, :]
    cos, sin = jnp.cos(ang), jnp.sin(ang)
    return jnp.concatenate([cos] * 4, axis=1), jnp.concatenate([sin] * 4, axis=1)


def kernel(x_prompt, x_sample, state_hgrn, cache_ckv, cache_krope, w_in_a, lb_gamma, a_norm_g, w_out_a,
           w_dkv, kv_norm_g, w_uk, w_uv, w_in_b, q_norm_g, w_uq, w_out_b, ln_g, ln_b):
    assert N_A == 1 and DEPTH == 2
    t_p = x_prompt.shape[1]
    t_s = x_sample.shape[1]
    past = cache_ckv.shape[1]
    row = lambda a: a.reshape(1, -1)

    w_in_a_b = w_in_a[0].astype(BF16)
    w_out_a_b = w_out_a[0].astype(BF16)
    kr_w = w_dkv[:, KV_LORA:]
    w_dkv_ext = jnp.concatenate(
        [w_dkv[:, :KV_LORA], kr_w, kr_w, _rot_cols(kr_w), _rot_cols(kr_w)], axis=1).astype(BF16)
    w_uq3 = w_uq[0].reshape(Q_LORA, B_HEADS, NOPE_DIM + ROPE_DIM)
    q_rope_w = w_uq3[:, :, NOPE_DIM:]
    w_uq_ext = jnp.concatenate([
        w_uq3[:, :, :NOPE_DIM].reshape(Q_LORA, B_HEADS * NOPE_DIM),
        q_rope_w.reshape(Q_LORA, B_HEADS * ROPE_DIM),
        _rot_cols(q_rope_w).reshape(Q_LORA, B_HEADS * ROPE_DIM)], axis=1).astype(BF16)
    w_uqt = jnp.concatenate([
        w_uq3[:, :, :NOPE_DIM].reshape(Q_LORA, B_HEADS * NOPE_DIM),
        q_rope_w.reshape(Q_LORA, B_HEADS * ROPE_DIM)], axis=1).T.astype(BF16)
    w_ukt = jnp.transpose(w_uk, (1, 2, 0)).astype(BF16)
    w_uk_h = jnp.transpose(w_uk, (1, 0, 2)).astype(BF16)
    w_uv_h = jnp.transpose(w_uv, (1, 0, 2)).astype(BF16)
    w_uvt_h = jnp.transpose(w_uv, (1, 2, 0)).astype(BF16)
    w_in_b_b = w_in_b[0].astype(BF16)
    w_out_b_b = w_out_b[0].astype(BF16)

    outs = []
    for x, s0, pos0, prompt in ((x_prompt, None, 0, True), (x_sample, state_hgrn[0], past, False)):
        bsz, t, _ = x.shape
        if prompt:
            bblk, tt, chunk, tm = 1, 256, CHUNK, 256
        else:
            bblk, tt, chunk, tm = bsz, t, t, t
        x1, s_fin = _hgrn_layer(x, s0, w_in_a_b, lb_gamma, row(a_norm_g[0]), w_out_a_b,
                                row(ln_g[0]), row(ln_b[0]), layer=0, bblk=bblk, tt=tt, chunk=chunk)
        cos2, sin2 = _rope_tables(pos0 + jnp.arange(t, dtype=jnp.int32))
        tabs = (cos2, sin2, cos2.T, sin2.T) if prompt else (cos2, sin2)
        proj_outs = _mla_proj(
            x1, tabs, w_dkv_ext, row(kv_norm_g), w_in_b_b, row(q_norm_g[0]),
            w_uqt if prompt else w_uq_ext, w_uk_h if prompt else w_ukt,
            bblk=bblk, tt=tm, transposed=prompt, tag="prompt" if prompt else "sample")
        if prompt:
            ckv, krope, kcat, kt, qt, gate = proj_outs
            y = _attn_prompt(qt, kcat, kt, gate, x1, w_uvt_h, w_out_b_b, row(ln_g[1]), row(ln_b[1]), rc=512)
        else:
            ckv, krope, kcat, qcat, gate = proj_outs
            y = _attn_sample(qcat, cache_ckv, cache_krope, kcat, gate, x1, w_uv_h, w_out_b_b,
                             row(ln_g[1]), row(ln_b[1]), tk=512)
        outs.append((y, s_fin[None], ckv, krope))
    (y_p, s_p, c_p, kr_p), (y_s, s_s, c_s, kr_s) = outs
    return (y_p, y_s, s_p, c_p, kr_p, s_s, c_s, kr_s)
```

```python
import functools

import jax
import jax.numpy as jnp
from jax import lax
from jax.experimental import pallas as pl
from jax.experimental.pallas import tpu as pltpu

F32 = jnp.float32
BF16 = jnp.bfloat16

D_MODEL = 1024
DEPTH = 2
CHUNK = 64
CHUNK_SHIFT = 6
N_A = DEPTH // 2
A_HEADS = 8
A_DK = 128
A_DV = 128
A_WIDTH = A_HEADS * A_DV
B_HEADS = 16
Q_LORA = 512
KV_LORA = 256
NOPE_DIM = 128
ROPE_DIM = 64
B_DV = 128
B_WIDTH = B_HEADS * B_DV
ROPE_BASE = 10000.0
EPS = 1e-6
NEG_INF = -1e30
ATTN_SCALE = (NOPE_DIM + ROPE_DIM) ** -0.5
LOG2_E = 1.4426950408889634
Q_SCALE = ATTN_SCALE * LOG2_E
ALPHA = (2 * DEPTH) ** 0.25

LANES = 128
KCAT = KV_LORA + 2 * ROPE_DIM
VMEM_LIMIT = 56 * 1024 * 1024

_NT = (((1,), (1,)), ((), ()))
_TN = (((0,), (0,)), ((), ()))


def _dot(a, b):
    return jnp.dot(a, b, preferred_element_type=F32)


def _layer_norm(x, g, b):
    mu = jnp.mean(x, axis=-1, keepdims=True)
    xc = x - mu
    var = jnp.mean(xc * xc, axis=-1, keepdims=True)
    return xc * lax.rsqrt(var + EPS) * g + b


def _rms_norm(x, g):
    return x * lax.rsqrt(jnp.mean(x * x, axis=-1, keepdims=True) + EPS) * g


def _sigmoid_pair(z):
    e = jnp.exp(-jnp.abs(z))
    r = 1.0 / (1.0 + e)
    er = e * r
    pos = z >= 0
    return jnp.where(pos, r, er), jnp.where(pos, er, r)


def _split3(x):
    hi = x.astype(BF16)
    r1 = x - hi.astype(F32)
    mid = r1.astype(BF16)
    lo = (r1 - mid.astype(F32)).astype(BF16)
    return hi, mid, lo


def _hgrn_kernel(x_ref, s0_ref, w_in_ref, lbg_ref, ng_ref, w_out_ref, lng_ref, lnb_ref, tri_ref,
                 y_ref, sfin_ref, st_ref, o_ref, *, layer, bblk, tt, chunk, sub, zero_init):
    t = pl.program_id(1)
    rows = bblk * tt
    mid = chunk // 2

    @pl.when(t == 0)
    def _():
        for bb in range(bblk):
            for h in range(A_HEADS):
                if zero_init:
                    st_ref[bb, h] = jnp.zeros((A_DV, A_DK), F32)
                else:
                    st_ref[bb, h] = s0_ref[bb, h].T

    lbg = lbg_ref[...]
    e = jnp.exp(lbg - jnp.max(lbg, axis=0, keepdims=True))
    lb = jnp.sum(e[0:layer + 1], axis=0, keepdims=True) / jnp.sum(e, axis=0, keepdims=True)
    one_m_lb = 1.0 - lb
    ng = ng_ref[...]
    li = lax.broadcasted_iota(jnp.int32, (chunk, chunk), 0)
    si = lax.broadcasted_iota(jnp.int32, (chunk, chunk), 1)
    causal = li >= si

    x = x_ref[...].reshape(rows, D_MODEL)
    n_sub = rows // sub
    n_chunks = sub // chunk
    n_parts = 4
    tri = tri_ref[...]
    heads = [slice(h * A_DK, (h + 1) * A_DK) for h in range(A_HEADS)]

    def project(s_idx, part):
        xs = x[s_idx * sub:(s_idx + 1) * sub].astype(BF16)
        return _dot(xs, w_in_ref[:, part * A_WIDTH:(part + 1) * A_WIDTH])

    parts = [project(0, p) for p in range(n_parts)]
    for s_idx in range(n_sub):
        more = s_idx + 1 < n_sub
        next_parts = []
        chunks = range(n_chunks)
        rows_of = [slice(c * chunk, (c + 1) * chunk) for c in chunks]

        k_all, b_all = [], []
        for c in chunks:
            z = parts[1][rows_of[c]]
            sig_pos = 1.0 / (1.0 + jnp.exp(-z))
            sig_neg = 1.0 / (1.0 + jnp.exp(z))
            logf = jnp.log(lb + one_m_lb * sig_pos)
            k_all.append(one_m_lb * sig_neg)
            hi, m_, lo = _split3(logf)
            b_all.append(_dot(tri, jnp.concatenate([hi, m_, lo], axis=0)))
        if more:
            next_parts.append(project(s_idx + 1, 0))

        q_ex, decay, v_all, sc_all, inc_all = [], [], [], [], []
        for c in chunks:
            bc = b_all[c]
            bmid = bc[mid:mid + 1]
            blast = bc[chunk - 1:chunk]
            qc = parts[0][rows_of[c]]
            q_mid = qc * jnp.exp(bc - bmid)
            k_mid = k_all[c] * jnp.exp(bmid - bc)
            q_in = q_mid.astype(BF16)
            k_in = k_mid.astype(BF16)
            k_dec = (k_mid * jnp.exp(blast - bmid)).astype(BF16)
            q_ex.append((q_mid * jnp.exp(bmid)).astype(BF16))
            decay.append(jnp.exp(blast))
            vc = parts[2][rows_of[c]].astype(BF16)
            v_all.append(vc)
            sc_all.append([lax.dot_general(q_in[:, hs], k_in[:, hs], _NT, preferred_element_type=F32)
                           for hs in heads])
            inc_all.append([lax.dot_general(vc[:, hs], k_dec[:, hs], _TN, preferred_element_type=F32)
                            for hs in heads])
        if more:
            next_parts.append(project(s_idx + 1, 1))

        st_in = []
        for c in chunks:
            bb = (s_idx * sub + c * chunk) // tt
            st_in.append([])
            for h, hs in enumerate(heads):
                st = st_ref[bb, h]
                st_in[c].append(st.astype(BF16))
                st_ref[bb, h] = st * decay[c][:, hs] + inc_all[c][h]
        if more:
            next_parts.append(project(s_idx + 1, 2))

        for c in chunks:
            r0 = s_idx * sub + c * chunk
            g = parts[3][rows_of[c]]
            silu_g = g / (1.0 + jnp.exp(-g))
            for h, hs in enumerate(heads):
                sc = jnp.where(causal, sc_all[c][h], 0.0).astype(BF16)
                o = _dot(sc, v_all[c][:, hs]) + lax.dot_general(
                    q_ex[c][:, hs], st_in[c][h], _NT, preferred_element_type=F32)
                o = _rms_norm(o, ng) * silu_g[:, hs]
                o_ref[r0:r0 + chunk, hs] = o.astype(BF16)
        if more:
            next_parts.append(project(s_idx + 1, 3))
        parts = next_parts

        ss = slice(s_idx * sub, (s_idx + 1) * sub)
        out = _dot(o_ref[ss, :], w_out_ref[...])
        y = _layer_norm(ALPHA * x[ss] + out, lng_ref[...], lnb_ref[...])
        if n_sub == 1:
            y_ref[...] = y.reshape(bblk, tt, D_MODEL)
        else:
            y_ref[0, ss, :] = y

    @pl.when(t == pl.num_programs(1) - 1)
    def _():
        for bb in range(bblk):
            for h in range(A_HEADS):
                sfin_ref[bb, h] = st_ref[bb, h].T


def _hgrn_layer(x, s0, w_in, lb_gamma, norm_g, w_out, ln_g, ln_b, *, layer, bblk, tt, chunk, sub):
    bsz, t, _ = x.shape
    rows = bblk * tt
    zero_init = s0 is None
    if zero_init:
        s0 = jnp.zeros((bblk, A_HEADS, 8, LANES), F32)
        s0_spec = pl.BlockSpec((bblk, A_HEADS, 8, LANES), lambda b, i: (0, 0, 0, 0))
    else:
        s0_spec = pl.BlockSpec((bblk, A_HEADS, A_DK, A_DV), lambda b, i: (b, 0, 0, 0))
    assert rows % sub == 0 and sub % chunk == 0 and tt % chunk == 0
    r = jnp.arange(chunk)
    tri = (r[:, None] >= r[None, :]).astype(BF16)
    tri3 = jnp.concatenate([tri, tri, tri], axis=1)
    const = lambda b, i: (0, 0)
    kern = functools.partial(_hgrn_kernel, layer=layer, bblk=bblk, tt=tt, chunk=chunk, sub=sub,
                             zero_init=zero_init)
    return pl.pallas_call(
        kern,
        grid=(bsz // bblk, t // tt),
        in_specs=[
            pl.BlockSpec((bblk, tt, D_MODEL), lambda b, i: (b, i, 0)),
            s0_spec,
            pl.BlockSpec((D_MODEL, 4 * A_WIDTH), const),
            pl.BlockSpec(lb_gamma.shape, const),
            pl.BlockSpec((1, A_DV), const),
            pl.BlockSpec((A_WIDTH, D_MODEL), const),
            pl.BlockSpec((1, D_MODEL), const),
            pl.BlockSpec((1, D_MODEL), const),
            pl.BlockSpec((chunk, 3 * chunk), const),
        ],
        out_specs=[
            pl.BlockSpec((bblk, tt, D_MODEL), lambda b, i: (b, i, 0)),
            pl.BlockSpec((bblk, A_HEADS, A_DK, A_DV), lambda b, i: (b, 0, 0, 0)),
        ],
        out_shape=[
            jax.ShapeDtypeStruct((bsz, t, D_MODEL), F32),
            jax.ShapeDtypeStruct((bsz, A_HEADS, A_DK, A_DV), F32),
        ],
        scratch_shapes=[
            pltpu.VMEM((bblk, A_HEADS, A_DV, A_DK), F32),
            pltpu.VMEM((rows, A_WIDTH), BF16),
        ],
        compiler_params=pltpu.CompilerParams(
            dimension_semantics=("arbitrary", "arbitrary"), vmem_limit_bytes=VMEM_LIMIT),
        name=f"hgrn_layer_{'prompt' if zero_init else 'sample'}",
    )(x, s0, w_in, lb_gamma, norm_g, w_out, ln_g, ln_b, tri3)


def _latent_kv(xb, cos2, sin2, w_dkv_ref, kvg_ref, ckv_ref, krope_ref, kcat_ref, bblk, tt):
    kv = _dot(xb, w_dkv_ref[...])
    c = _rms_norm(kv[:, :KV_LORA], kvg_ref[...])
    krd = kv[:, KV_LORA:KV_LORA + LANES] * cos2 + kv[:, KV_LORA + LANES:KV_LORA + 2 * LANES] * sin2
    ckv_ref[...] = c.reshape(bblk, tt, KV_LORA)
    krope_ref[...] = krd[:, :ROPE_DIM].reshape(bblk, tt, ROPE_DIM)
    kcat = jnp.concatenate([c, krd], axis=1)
    kcat_ref[...] = kcat.astype(BF16).reshape(bblk, tt, KCAT)
    return kcat


def _query_latent(xb, w_in_ref, qg_ref, gate_ref, bblk, tt):
    proj = _dot(xb, w_in_ref[...])
    cq = _rms_norm(proj[:, :Q_LORA], qg_ref[...])
    gt = proj[:, Q_LORA:]
    gate_ref[...] = (gt * _sigmoid_pair(gt)[0]).reshape(bblk, tt, B_WIDTH)
    return cq


def _mla_proj_rows_kernel(x_ref, cos_ref, sin_ref, w_dkv_ref, kvg_ref, w_in_ref, qg_ref, w_uq_ref, w_ukt_ref,
                          ckv_ref, krope_ref, kcat_ref, q_ref, gate_ref, *, bblk, tt):
    rows = bblk * tt
    xb = x_ref[...].reshape(rows, D_MODEL).astype(BF16)
    cos2 = jnp.concatenate([cos_ref[...]] * bblk, axis=0)
    sin2 = jnp.concatenate([sin_ref[...]] * bblk, axis=0)
    _latent_kv(xb, cos2, sin2, w_dkv_ref, kvg_ref, ckv_ref, krope_ref, kcat_ref, bblk, tt)
    cq = _query_latent(xb, w_in_ref, qg_ref, gate_ref, bblk, tt)
    qall = _dot(cq.astype(BF16), w_uq_ref[...])
    nope_w = B_HEADS * NOPE_DIM
    pair_w = (B_HEADS // 2) * LANES
    lane = lax.broadcasted_iota(jnp.int32, (rows, LANES), 1)
    for j in range(B_HEADS // 2):
        raw = qall[:, nope_w + j * LANES:nope_w + (j + 1) * LANES]
        rot = qall[:, nope_w + pair_w + j * LANES:nope_w + pair_w + (j + 1) * LANES]
        rp = (raw * cos2 + rot * sin2) * Q_SCALE
        for e in range(2):
            h = 2 * j + e
            ql = _dot(qall[:, h * NOPE_DIM:(h + 1) * NOPE_DIM].astype(BF16), w_ukt_ref[h]) * Q_SCALE
            keep = (lane < ROPE_DIM) if e == 0 else (lane >= ROPE_DIM)
            qc = jnp.concatenate([ql, jnp.where(keep, rp, 0.0)], axis=1).astype(BF16)
            for bb in range(bblk):
                q_ref[bb, h] = qc[bb * tt:(bb + 1) * tt]


def _mla_proj_cols_kernel(x_ref, cos_ref, sin_ref, cost_ref, sint_ref, w_dkv_ref, kvg_ref, w_in_ref, qg_ref,
                          w_uqt_ref, w_uk_ref, ckv_ref, krope_ref, kcat_ref, kt_ref, q_ref, gate_ref, *, tt):
    xb = x_ref[0].astype(BF16)
    kcat = _latent_kv(xb, cos_ref[...], sin_ref[...], w_dkv_ref, kvg_ref, ckv_ref, krope_ref, kcat_ref, 1, tt)
    kt_ref[0, 0] = kcat.T.astype(BF16)
    cq = _query_latent(xb, w_in_ref, qg_ref, gate_ref, 1, tt)
    qall_t = _dot(w_uqt_ref[...], cq.T.astype(BF16))
    nope_w = B_HEADS * NOPE_DIM
    half = ROPE_DIM // 2
    cos_t = cost_ref[...]
    sin_t = sint_ref[...]
    zeros = jnp.zeros((ROPE_DIM, tt), F32)
    for j in range(B_HEADS // 2):
        raw = qall_t[nope_w + j * LANES:nope_w + (j + 1) * LANES]
        rot = jnp.concatenate([-raw[half:2 * half], raw[0:half], -raw[3 * half:4 * half], raw[2 * half:3 * half]],
                              axis=0)
        rp = (raw * cos_t + rot * sin_t) * Q_SCALE
        for e in range(2):
            h = 2 * j + e
            ql = _dot(w_uk_ref[h], qall_t[h * NOPE_DIM:(h + 1) * NOPE_DIM].astype(BF16)) * Q_SCALE
            rope_rows = [rp[:ROPE_DIM], zeros] if e == 0 else [zeros, rp[ROPE_DIM:]]
            q_ref[0, 0, :, h * tt:(h + 1) * tt] = jnp.concatenate([ql] + rope_rows, axis=0).astype(BF16)


def _mla_proj(x, rope_tabs, w_dkv_ext, kv_g, w_in, q_g, w_q, w_k, *, bblk, tt, transposed, tag):
    bsz, t, _ = x.shape
    const2 = lambda b, i: (0, 0)
    const3 = lambda b, i: (0, 0, 0)
    tok = lambda w: pl.BlockSpec((bblk, tt, w), lambda b, i: (b, i, 0))
    tab = pl.BlockSpec((tt, LANES), lambda b, i: (i, 0))
    in_specs = [tok(D_MODEL), tab, tab]
    out_specs = [tok(KV_LORA), tok(ROPE_DIM), tok(KCAT)]
    out_shape = [
        jax.ShapeDtypeStruct((bsz, t, KV_LORA), F32),
        jax.ShapeDtypeStruct((bsz, t, ROPE_DIM), F32),
        jax.ShapeDtypeStruct((bsz, t, KCAT), BF16),
    ]
    if transposed:
        assert bblk == 1
        tab_t = pl.BlockSpec((LANES, tt), lambda b, i: (0, i))
        in_specs += [tab_t, tab_t]
        out_specs += [pl.BlockSpec((1, 1, KCAT, tt), lambda b, i: (b, i, 0, 0)),
                      pl.BlockSpec((1, 1, KCAT, B_HEADS * tt), lambda b, i: (b, i, 0, 0))]
        out_shape += [jax.ShapeDtypeStruct((bsz, t // tt, KCAT, tt), BF16),
                      jax.ShapeDtypeStruct((bsz, t // tt, KCAT, B_HEADS * tt), BF16)]
        kern = functools.partial(_mla_proj_cols_kernel, tt=tt)
    else:
        out_specs += [pl.BlockSpec((bblk, B_HEADS, tt, KCAT), lambda b, i: (b, 0, i, 0))]
        out_shape += [jax.ShapeDtypeStruct((bsz, B_HEADS, t, KCAT), BF16)]
        kern = functools.partial(_mla_proj_rows_kernel, bblk=bblk, tt=tt)
    in_specs += [
        pl.BlockSpec(w_dkv_ext.shape, const2),
        pl.BlockSpec((1, KV_LORA), const2),
        pl.BlockSpec(w_in.shape, const2),
        pl.BlockSpec((1, Q_LORA), const2),
        pl.BlockSpec(w_q.shape, const2),
        pl.BlockSpec(w_k.shape, const3),
    ]
    out_specs += [tok(B_WIDTH)]
    out_shape += [jax.ShapeDtypeStruct((bsz, t, B_WIDTH), F32)]
    return pl.pallas_call(
        kern,
        grid=(bsz // bblk, t // tt),
        in_specs=in_specs,
        out_specs=out_specs,
        out_shape=out_shape,
        compiler_params=pltpu.CompilerParams(
            dimension_semantics=("arbitrary", "arbitrary"), vmem_limit_bytes=VMEM_LIMIT),
        name=f"mla_proj_{tag}",
    )(x, *rope_tabs, w_dkv_ext, kv_g, w_in, q_g, w_q, w_k)


def _softmax_step(s, v, m_ref, l_ref, acc_ref):
    m_prev = m_ref[...]
    m_new = jnp.maximum(m_prev, jnp.max(s, axis=1, keepdims=True))
    a = jnp.exp2(m_prev - m_new)
    p = jnp.exp2(s - m_new)
    l_ref[...] = a * l_ref[...] + jnp.sum(p, axis=1, keepdims=True)
    acc_ref[...] = a * acc_ref[...] + _dot(p.astype(BF16), v)
    m_ref[...] = m_new


def _attn_output(acc_ref, l_ref, o_ref, gate, x, w_uv_ref, w_out_ref, lng_ref, lnb_ref, tq):
    for h in range(B_HEADS):
        hs = slice(h * tq, (h + 1) * tq)
        o_lat = acc_ref[hs, :] / l_ref[hs, :]
        oh = _dot(o_lat.astype(BF16), w_uv_ref[h])
        o_ref[:, h * B_DV:(h + 1) * B_DV] = (oh * gate[:, h * B_DV:(h + 1) * B_DV]).astype(BF16)
    out = _dot(o_ref[...], w_out_ref[...])
    return _layer_norm(ALPHA * x + out, lng_ref[...], lnb_ref[...])


def _attn_prompt_kernel(qt_ref, kcat_ref, kt_ref, gate_ref, x_ref, w_uvt_ref, w_out_ref, lng_ref, lnb_ref,
                        y_ref, m_ref, l_ref, acc_ref, o_ref, *, tq, rc):
    i = pl.program_id(1)
    rows = B_HEADS * tq
    m_ref[...] = jnp.full((1, rows), NEG_INF, F32)
    l_ref[...] = jnp.zeros((1, rows), F32)
    acc_ref[...] = jnp.zeros((KV_LORA, rows), F32)

    def kv_step(j, visible):
        kb = kcat_ref[0, pl.ds(pl.multiple_of(j * tq, tq), tq), :]
        ct = kt_ref[0, j, 0:KV_LORA, :]
        n_rc = rows // rc

        def scores(r):
            return _dot(kb, qt_ref[0, 0, :, r * rc:(r + 1) * rc])

        def softmax(r, st):
            ls = slice(r * rc, (r + 1) * rc)
            if visible is not None:
                st = jnp.where(visible, st, NEG_INF)
            m_prev = m_ref[:, ls]
            m_new = jnp.maximum(m_prev, jnp.max(st, axis=0, keepdims=True))
            a = jnp.exp2(m_prev - m_new)
            pt = jnp.exp2(st - m_new)
            l_ref[:, ls] = a * l_ref[:, ls] + jnp.sum(pt, axis=0, keepdims=True)
            m_ref[:, ls] = m_new
            return a, pt.astype(BF16)

        st_cur = scores(0)
        pend = None
        for s in range(n_rc + 1):
            st_next = scores(s + 1) if s + 1 < n_rc else None
            if pend is not None:
                a_prev, pt_prev = pend
                pv = _dot(ct, pt_prev)
            new_pend = softmax(s, st_cur) if s < n_rc else None
            if pend is not None:
                ls = slice((s - 1) * rc, s * rc)
                acc_ref[:, ls] = a_prev * acc_ref[:, ls] + pv
            pend = new_pend
            st_cur = st_next

    def body(j, carry):
        kv_step(j, None)
        return carry

    lax.fori_loop(0, i, body, 0)

    key_chunk = lax.broadcasted_iota(jnp.int32, (tq, rc), 0) >> CHUNK_SHIFT
    q_chunk = (lax.broadcasted_iota(jnp.int32, (tq, rc), 1) & (tq - 1)) >> CHUNK_SHIFT
    visible = key_chunk <= q_chunk

    def body_diag(j, carry):
        kv_step(j, visible)
        return carry

    lax.fori_loop(i, i + 1, body_diag, 0)

    gate = gate_ref[0]
    for h in range(B_HEADS):
        hs = slice(h * tq, (h + 1) * tq)
        o_lat = acc_ref[:, hs] / l_ref[:, hs]
        oh = _dot(w_uvt_ref[h], o_lat.astype(BF16)).T
        o_ref[:, h * B_DV:(h + 1) * B_DV] = (oh * gate[:, h * B_DV:(h + 1) * B_DV]).astype(BF16)
    out = _dot(o_ref[...], w_out_ref[...])
    y_ref[0] = _layer_norm(ALPHA * x_ref[0] + out, lng_ref[...], lnb_ref[...])


def _attn_prompt(qt, kcat, kt, gate, x, w_uvt_h, w_out, ln_g, ln_b, *, rc):
    bsz, nq, _, rows = qt.shape
    tq = rows // B_HEADS
    t = kcat.shape[1]
    assert kt.shape[-1] == tq and tq & (tq - 1) == 0 and tq % CHUNK == 0 and rc % tq == 0
    const2 = lambda b, i: (0, 0)
    kern = functools.partial(_attn_prompt_kernel, tq=tq, rc=rc)
    return pl.pallas_call(
        kern,
        grid=(bsz, nq),
        in_specs=[
            pl.BlockSpec((1, 1, KCAT, rows), lambda b, i: (b, i, 0, 0)),
            pl.BlockSpec((1, t, KCAT), lambda b, i: (b, 0, 0)),
            pl.BlockSpec((1, nq, KCAT, tq), lambda b, i: (b, 0, 0, 0)),
            pl.BlockSpec((1, tq, B_WIDTH), lambda b, i: (b, i, 0)),
            pl.BlockSpec((1, tq, D_MODEL), lambda b, i: (b, i, 0)),
            pl.BlockSpec(w_uvt_h.shape, lambda b, i: (0, 0, 0)),
            pl.BlockSpec(w_out.shape, const2),
            pl.BlockSpec((1, D_MODEL), const2),
            pl.BlockSpec((1, D_MODEL), const2),
        ],
        out_specs=pl.BlockSpec((1, tq, D_MODEL), lambda b, i: (b, i, 0)),
        out_shape=jax.ShapeDtypeStruct((bsz, t, D_MODEL), F32),
        scratch_shapes=[
            pltpu.VMEM((1, rows), F32),
            pltpu.VMEM((1, rows), F32),
            pltpu.VMEM((KV_LORA, rows), F32),
            pltpu.VMEM((tq, B_WIDTH), BF16),
        ],
        compiler_params=pltpu.CompilerParams(
            dimension_semantics=("arbitrary", "arbitrary"), vmem_limit_bytes=VMEM_LIMIT),
        name="mla_attend_prompt",
    )(qt, kcat, kt, gate, x, w_uvt_h, w_out, ln_g, ln_b)


def _attn_sample_kernel(q_ref, cc_ref, ckr_ref, kn_ref, gate_ref, x_ref, w_uv_ref, w_out_ref,
                        lng_ref, lnb_ref, y_ref, m_ref, l_ref, acc_ref, o_ref, *, tq, tk):
    rows = B_HEADS * tq
    past = cc_ref.shape[1]
    q = q_ref[0].reshape(rows, KCAT)
    q_lat = q[:, :KV_LORA]
    q3 = q[:, KV_LORA:].astype(F32)
    q_rope = (q3[:, :ROPE_DIM] + q3[:, ROPE_DIM:]).astype(BF16)
    m_ref[...] = jnp.full((rows, 1), NEG_INF, F32)
    l_ref[...] = jnp.zeros((rows, 1), F32)
    acc_ref[...] = jnp.zeros((rows, KV_LORA), F32)

    for j in range(past // tk):
        cb = cc_ref[0, j * tk:(j + 1) * tk, :].astype(BF16)
        krb = ckr_ref[0, j * tk:(j + 1) * tk, :].astype(BF16)
        s = (lax.dot_general(q_lat, cb, _NT, preferred_element_type=F32)
             + lax.dot_general(q_rope, krb, _NT, preferred_element_type=F32))
        _softmax_step(s, cb, m_ref, l_ref, acc_ref)

    kn = kn_ref[0]
    s = lax.dot_general(q, kn, _NT, preferred_element_type=F32)
    _softmax_step(s, kn[:, :KV_LORA], m_ref, l_ref, acc_ref)

    y_ref[0] = _attn_output(acc_ref, l_ref, o_ref, gate_ref[0], x_ref[0],
                            w_uv_ref, w_out_ref, lng_ref, lnb_ref, tq)


def _attn_sample(qcat, cache_c, cache_kr, kcat_new, gate, x, w_uv_h, w_out, ln_g, ln_b, *, tk):
    bsz, _, tq, _ = qcat.shape
    past = cache_c.shape[1]
    rows = B_HEADS * tq
    const2 = lambda b: (0, 0)
    per_b = lambda shape: pl.BlockSpec((1,) + shape, lambda b: (b,) + (0,) * len(shape))
    kern = functools.partial(_attn_sample_kernel, tq=tq, tk=tk)
    return pl.pallas_call(
        kern,
        grid=(bsz,),
        in_specs=[
            per_b((B_HEADS, tq, KCAT)),
            per_b((past, KV_LORA)),
            per_b((past, ROPE_DIM)),
            per_b((tq, KCAT)),
            per_b((tq, B_WIDTH)),
            per_b((tq, D_MODEL)),
            pl.BlockSpec(w_uv_h.shape, lambda b: (0, 0, 0)),
            pl.BlockSpec(w_out.shape, const2),
            pl.BlockSpec((1, D_MODEL), const2),
            pl.BlockSpec((1, D_MODEL), const2),
        ],
        out_specs=per_b((tq, D_MODEL)),
        out_shape=jax.ShapeDtypeStruct((bsz, tq, D_MODEL), F32),
        scratch_shapes=[
            pltpu.VMEM((rows, 1), F32),
            pltpu.VMEM((rows, 1), F32),
            pltpu.VMEM((rows, KV_LORA), F32),
            pltpu.VMEM((tq, B_WIDTH), BF16),
        ],
        compiler_params=pltpu.CompilerParams(
            dimension_semantics=("arbitrary",), vmem_limit_bytes=VMEM_LIMIT),
        name="mla_attend_sample",
    )(qcat, cache_c, cache_kr, kcat_new, gate, x, w_uv_h, w_out, ln_g, ln_b)


def _rot_cols(w):
    half = ROPE_DIM // 2
    return jnp.concatenate([-w[..., half:], w[..., :half]], axis=-1)


def _rope_tables(pos):
    half = ROPE_DIM // 2
    inv = jnp.power(ROPE_BASE, -jnp.arange(half, dtype=F32) / half)
    ang = pos.astype(F32)[:, None] * inv[None, :]
    cos, sin = jnp.cos(ang), jnp.sin(ang)
    return jnp.concatenate([cos] * 4, axis=1), jnp.concatenate([sin] * 4, axis=1)


def kernel(x_prompt, x_sample, state_hgrn, cache_ckv, cache_krope, w_in_a, lb_gamma, a_norm_g, w_out_a,
           w_dkv, kv_norm_g, w_uk, w_uv, w_in_b, q_norm_g, w_uq, w_out_b, ln_g, ln_b):
    assert N_A == 1 and DEPTH == 2
    t_p = x_prompt.shape[1]
    t_s = x_sample.shape[1]
    past = cache_ckv.shape[1]
    row = lambda a: a.reshape(1, -1)

    w_in_a_b = w_in_a[0].astype(BF16)
    w_out_a_b = w_out_a[0].astype(BF16)
    kr_w = w_dkv[:, KV_LORA:]
    w_dkv_ext = jnp.concatenate(
        [w_dkv[:, :KV_LORA], kr_w, kr_w, _rot_cols(kr_w), _rot_cols(kr_w)], axis=1).astype(BF16)
    w_uq3 = w_uq[0].reshape(Q_LORA, B_HEADS, NOPE_DIM + ROPE_DIM)
    q_rope_w = w_uq3[:, :, NOPE_DIM:]
    w_uq_ext = jnp.concatenate([
        w_uq3[:, :, :NOPE_DIM].reshape(Q_LORA, B_HEADS * NOPE_DIM),
        q_rope_w.reshape(Q_LORA, B_HEADS * ROPE_DIM),
        _rot_cols(q_rope_w).reshape(Q_LORA, B_HEADS * ROPE_DIM)], axis=1).astype(BF16)
    w_uqt = jnp.concatenate([
        w_uq3[:, :, :NOPE_DIM].reshape(Q_LORA, B_HEADS * NOPE_DIM),
        q_rope_w.reshape(Q_LORA, B_HEADS * ROPE_DIM)], axis=1).T.astype(BF16)
    w_ukt = jnp.transpose(w_uk, (1, 2, 0)).astype(BF16)
    w_uk_h = jnp.transpose(w_uk, (1, 0, 2)).astype(BF16)
    w_uv_h = jnp.transpose(w_uv, (1, 0, 2)).astype(BF16)
    w_uvt_h = jnp.transpose(w_uv, (1, 2, 0)).astype(BF16)
    w_in_b_b = w_in_b[0].astype(BF16)
    w_out_b_b = w_out_b[0].astype(BF16)

    outs = []
    for x, s0, pos0, prompt in ((x_prompt, None, 0, True), (x_sample, state_hgrn[0], past, False)):
        bsz, t, _ = x.shape
        if prompt:
            bblk, tt, chunk, sub, tm = 1, 1024, CHUNK, 256, 256
        else:
            bblk, tt, chunk, sub, tm = bsz, t, t, bsz * t, t
        x1, s_fin = _hgrn_layer(x, s0, w_in_a_b, lb_gamma, row(a_norm_g[0]), w_out_a_b,
                                row(ln_g[0]), row(ln_b[0]), layer=0, bblk=bblk, tt=tt, chunk=chunk, sub=sub)
        cos2, sin2 = _rope_tables(pos0 + jnp.arange(t, dtype=jnp.int32))
        tabs = (cos2, sin2, cos2.T, sin2.T) if prompt else (cos2, sin2)
        proj_outs = _mla_proj(
            x1, tabs, w_dkv_ext, row(kv_norm_g), w_in_b_b, row(q_norm_g[0]),
            w_uqt if prompt else w_uq_ext, w_uk_h if prompt else w_ukt,
            bblk=bblk, tt=tm, transposed=prompt, tag="prompt" if prompt else "sample")
        if prompt:
            ckv, krope, kcat, kt, qt, gate = proj_outs
            y = _attn_prompt(qt, kcat, kt, gate, x1, w_uvt_h, w_out_b_b, row(ln_g[1]), row(ln_b[1]), rc=512)
        else:
            ckv, krope, kcat, qcat, gate = proj_outs
            y = _attn_sample(qcat, cache_ckv, cache_krope, kcat, gate, x1, w_uv_h, w_out_b_b,
                             row(ln_g[1]), row(ln_b[1]), tk=512)
        outs.append((y, s_fin[None], ckv, krope))
    (y_p, s_p, c_p, kr_p), (y_s, s_s, c_s, kr_s) = outs
    return (y_p, y_s, s_p, c_p, kr_p, s_s, c_s, kr_s)
```

```python
import functools

import jax
import jax.numpy as jnp
from jax import lax
from jax.experimental import pallas as pl
from jax.experimental.pallas import tpu as pltpu

F32 = jnp.float32
BF16 = jnp.bfloat16

D_MODEL = 1024
DEPTH = 2
CHUNK = 64
CHUNK_SHIFT = 6
N_A = DEPTH // 2
A_HEADS = 8
A_DK = 128
A_DV = 128
A_WIDTH = A_HEADS * A_DV
B_HEADS = 16
Q_LORA = 512
KV_LORA = 256
NOPE_DIM = 128
ROPE_DIM = 64
B_DV = 128
B_WIDTH = B_HEADS * B_DV
ROPE_BASE = 10000.0
EPS = 1e-6
NEG_INF = -1e30
ATTN_SCALE = (NOPE_DIM + ROPE_DIM) ** -0.5
LOG2_E = 1.4426950408889634
Q_SCALE = ATTN_SCALE * LOG2_E
ALPHA = (2 * DEPTH) ** 0.25

LANES = 128
KCAT = KV_LORA + 2 * ROPE_DIM
VEXT = KV_LORA + 16
VMEM_LIMIT = 56 * 1024 * 1024

_NT = (((1,), (1,)), ((), ()))
_TN = (((0,), (0,)), ((), ()))


def _dot(a, b):
    return jnp.dot(a, b, preferred_element_type=F32)


def _layer_norm(x, g, b):
    mu = jnp.mean(x, axis=-1, keepdims=True)
    xc = x - mu
    var = jnp.mean(xc * xc, axis=-1, keepdims=True)
    return xc * lax.rsqrt(var + EPS) * g + b


def _rms_norm(x, g):
    return x * lax.rsqrt(jnp.mean(x * x, axis=-1, keepdims=True) + EPS) * g


def _sigmoid_pair(z):
    e = jnp.exp(-jnp.abs(z))
    r = 1.0 / (1.0 + e)
    er = e * r
    pos = z >= 0
    return jnp.where(pos, r, er), jnp.where(pos, er, r)


def _split3(x):
    hi = x.astype(BF16)
    r1 = x - hi.astype(F32)
    mid = r1.astype(BF16)
    lo = (r1 - mid.astype(F32)).astype(BF16)
    return hi, mid, lo


def _hgrn_kernel(x_ref, s0_ref, w_in_ref, lbg_ref, ng_ref, w_out_ref, lng_ref, lnb_ref, tri_ref,
                 y_ref, sfin_ref, st_ref, o_ref, *, layer, bblk, tt, chunk, sub, zero_init):
    t = pl.program_id(1)
    rows = bblk * tt
    mid = chunk // 2

    @pl.when(t == 0)
    def _():
        for bb in range(bblk):
            for h in range(A_HEADS):
                if zero_init:
                    st_ref[bb, h] = jnp.zeros((A_DV, A_DK), F32)
                else:
                    st_ref[bb, h] = s0_ref[bb, h].T

    lbg = lbg_ref[...]
    e = jnp.exp(lbg - jnp.max(lbg, axis=0, keepdims=True))
    lb = jnp.sum(e[0:layer + 1], axis=0, keepdims=True) / jnp.sum(e, axis=0, keepdims=True)
    one_m_lb = 1.0 - lb
    ng = ng_ref[...]
    li = lax.broadcasted_iota(jnp.int32, (chunk, chunk), 0)
    si = lax.broadcasted_iota(jnp.int32, (chunk, chunk), 1)
    causal = li >= si

    x = x_ref[...].reshape(rows, D_MODEL)
    n_sub = rows // sub
    n_chunks = sub // chunk
    n_parts = 4
    tri = tri_ref[...]
    heads = [slice(h * A_DK, (h + 1) * A_DK) for h in range(A_HEADS)]

    def project(s_idx, part):
        xs = x[s_idx * sub:(s_idx + 1) * sub].astype(BF16)
        return _dot(xs, w_in_ref[:, part * A_WIDTH:(part + 1) * A_WIDTH])

    parts = [project(0, p) for p in range(n_parts)]
    for s_idx in range(n_sub):
        more = s_idx + 1 < n_sub
        next_parts = []
        chunks = range(n_chunks)
        rows_of = [slice(c * chunk, (c + 1) * chunk) for c in chunks]

        k_all, b_all = [], []
        for c in chunks:
            z = parts[1][rows_of[c]]
            sig_pos = 1.0 / (1.0 + jnp.exp(-z))
            sig_neg = 1.0 / (1.0 + jnp.exp(z))
            logf = jnp.log(lb + one_m_lb * sig_pos)
            k_all.append(one_m_lb * sig_neg)
            hi, m_, lo = _split3(logf)
            b_all.append(_dot(tri, jnp.concatenate([hi, m_, lo], axis=0)))
        if more:
            next_parts.append(project(s_idx + 1, 0))

        q_ex, decay, v_all, sc_all, inc_all = [], [], [], [], []
        for c in chunks:
            bc = b_all[c]
            bmid = bc[mid:mid + 1]
            blast = bc[chunk - 1:chunk]
            qc = parts[0][rows_of[c]]
            q_mid = qc * jnp.exp(bc - bmid)
            k_mid = k_all[c] * jnp.exp(bmid - bc)
            q_in = q_mid.astype(BF16)
            k_in = k_mid.astype(BF16)
            k_dec = (k_mid * jnp.exp(blast - bmid)).astype(BF16)
            q_ex.append((q_mid * jnp.exp(bmid)).astype(BF16))
            decay.append(jnp.exp(blast))
            vc = parts[2][rows_of[c]].astype(BF16)
            v_all.append(vc)
            sc_all.append([lax.dot_general(q_in[:, hs], k_in[:, hs], _NT, preferred_element_type=F32)
                           for hs in heads])
            inc_all.append([lax.dot_general(vc[:, hs], k_dec[:, hs], _TN, preferred_element_type=F32)
                            for hs in heads])
        if more:
            next_parts.append(project(s_idx + 1, 1))

        st_in = []
        for c in chunks:
            bb = (s_idx * sub + c * chunk) // tt
            st_in.append([])
            for h, hs in enumerate(heads):
                st = st_ref[bb, h]
                st_in[c].append(st.astype(BF16))
                st_ref[bb, h] = st * decay[c][:, hs] + inc_all[c][h]
        if more:
            next_parts.append(project(s_idx + 1, 2))

        for c in chunks:
            r0 = s_idx * sub + c * chunk
            g = parts[3][rows_of[c]]
            silu_g = g / (1.0 + jnp.exp(-g))
            for h, hs in enumerate(heads):
                sc = jnp.where(causal, sc_all[c][h], 0.0).astype(BF16)
                o = _dot(sc, v_all[c][:, hs]) + lax.dot_general(
                    q_ex[c][:, hs], st_in[c][h], _NT, preferred_element_type=F32)
                o = _rms_norm(o, ng) * silu_g[:, hs]
                o_ref[r0:r0 + chunk, hs] = o.astype(BF16)
        if more:
            next_parts.append(project(s_idx + 1, 3))
        parts = next_parts

        ss = slice(s_idx * sub, (s_idx + 1) * sub)
        out = _dot(o_ref[ss, :], w_out_ref[...])
        y = _layer_norm(ALPHA * x[ss] + out, lng_ref[...], lnb_ref[...])
        if n_sub == 1:
            y_ref[...] = y.reshape(bblk, tt, D_MODEL)
        else:
            y_ref[0, ss, :] = y

    @pl.when(t == pl.num_programs(1) - 1)
    def _():
        for bb in range(bblk):
            for h in range(A_HEADS):
                sfin_ref[bb, h] = st_ref[bb, h].T


def _hgrn_layer(x, s0, w_in, lb_gamma, norm_g, w_out, ln_g, ln_b, *, layer, bblk, tt, chunk, sub):
    bsz, t, _ = x.shape
    rows = bblk * tt
    zero_init = s0 is None
    if zero_init:
        s0 = jnp.zeros((bblk, A_HEADS, 8, LANES), F32)
        s0_spec = pl.BlockSpec((bblk, A_HEADS, 8, LANES), lambda b, i: (0, 0, 0, 0))
    else:
        s0_spec = pl.BlockSpec((bblk, A_HEADS, A_DK, A_DV), lambda b, i: (b, 0, 0, 0))
    assert rows % sub == 0 and sub % chunk == 0 and tt % chunk == 0
    r = jnp.arange(chunk)
    tri = (r[:, None] >= r[None, :]).astype(BF16)
    tri3 = jnp.concatenate([tri, tri, tri], axis=1)
    const = lambda b, i: (0, 0)
    kern = functools.partial(_hgrn_kernel, layer=layer, bblk=bblk, tt=tt, chunk=chunk, sub=sub,
                             zero_init=zero_init)
    return pl.pallas_call(
        kern,
        grid=(bsz // bblk, t // tt),
        in_specs=[
            pl.BlockSpec((bblk, tt, D_MODEL), lambda b, i: (b, i, 0)),
            s0_spec,
            pl.BlockSpec((D_MODEL, 4 * A_WIDTH), const),
            pl.BlockSpec(lb_gamma.shape, const),
            pl.BlockSpec((1, A_DV), const),
            pl.BlockSpec((A_WIDTH, D_MODEL), const),
            pl.BlockSpec((1, D_MODEL), const),
            pl.BlockSpec((1, D_MODEL), const),
            pl.BlockSpec((chunk, 3 * chunk), const),
        ],
        out_specs=[
            pl.BlockSpec((bblk, tt, D_MODEL), lambda b, i: (b, i, 0)),
            pl.BlockSpec((bblk, A_HEADS, A_DK, A_DV), lambda b, i: (b, 0, 0, 0)),
        ],
        out_shape=[
            jax.ShapeDtypeStruct((bsz, t, D_MODEL), F32),
            jax.ShapeDtypeStruct((bsz, A_HEADS, A_DK, A_DV), F32),
        ],
        scratch_shapes=[
            pltpu.VMEM((bblk, A_HEADS, A_DV, A_DK), F32),
            pltpu.VMEM((rows, A_WIDTH), BF16),
        ],
        compiler_params=pltpu.CompilerParams(
            dimension_semantics=("arbitrary", "arbitrary"), vmem_limit_bytes=VMEM_LIMIT),
        name=f"hgrn_layer_{'prompt' if zero_init else 'sample'}",
    )(x, s0, w_in, lb_gamma, norm_g, w_out, ln_g, ln_b, tri3)


def _latent_kv(xb, cos2, sin2, w_dkv_ref, kvg_ref, ckv_ref, krope_ref, kcat_ref, bblk, tt):
    kv = _dot(xb, w_dkv_ref[...])
    c = _rms_norm(kv[:, :KV_LORA], kvg_ref[...])
    krd = kv[:, KV_LORA:KV_LORA + LANES] * cos2 + kv[:, KV_LORA + LANES:KV_LORA + 2 * LANES] * sin2
    ckv_ref[...] = c.reshape(bblk, tt, KV_LORA)
    krope_ref[...] = krd[:, :ROPE_DIM].reshape(bblk, tt, ROPE_DIM)
    kcat = jnp.concatenate([c, krd], axis=1)
    kcat_ref[...] = kcat.astype(BF16).reshape(bblk, tt, KCAT)
    return kcat


def _query_latent(xb, w_in_ref, qg_ref, gate_ref, bblk, tt):
    proj = _dot(xb, w_in_ref[...])
    cq = _rms_norm(proj[:, :Q_LORA], qg_ref[...])
    gt = proj[:, Q_LORA:]
    gate_ref[...] = (gt * _sigmoid_pair(gt)[0]).reshape(bblk, tt, B_WIDTH)
    return cq


def _mla_proj_rows_kernel(x_ref, cos_ref, sin_ref, w_dkv_ref, kvg_ref, w_in_ref, qg_ref, w_uq_ref, w_ukt_ref,
                          ckv_ref, krope_ref, kcat_ref, q_ref, gate_ref, *, bblk, tt):
    rows = bblk * tt
    xb = x_ref[...].reshape(rows, D_MODEL).astype(BF16)
    cos2 = jnp.concatenate([cos_ref[...]] * bblk, axis=0)
    sin2 = jnp.concatenate([sin_ref[...]] * bblk, axis=0)
    _latent_kv(xb, cos2, sin2, w_dkv_ref, kvg_ref, ckv_ref, krope_ref, kcat_ref, bblk, tt)
    cq = _query_latent(xb, w_in_ref, qg_ref, gate_ref, bblk, tt)
    qall = _dot(cq.astype(BF16), w_uq_ref[...])
    nope_w = B_HEADS * NOPE_DIM
    pair_w = (B_HEADS // 2) * LANES
    lane = lax.broadcasted_iota(jnp.int32, (rows, LANES), 1)
    for j in range(B_HEADS // 2):
        raw = qall[:, nope_w + j * LANES:nope_w + (j + 1) * LANES]
        rot = qall[:, nope_w + pair_w + j * LANES:nope_w + pair_w + (j + 1) * LANES]
        rp = (raw * cos2 + rot * sin2) * Q_SCALE
        for e in range(2):
            h = 2 * j + e
            ql = _dot(qall[:, h * NOPE_DIM:(h + 1) * NOPE_DIM].astype(BF16), w_ukt_ref[h]) * Q_SCALE
            keep = (lane < ROPE_DIM) if e == 0 else (lane >= ROPE_DIM)
            qc = jnp.concatenate([ql, jnp.where(keep, rp, 0.0)], axis=1).astype(BF16)
            for bb in range(bblk):
                q_ref[bb, h] = qc[bb * tt:(bb + 1) * tt]


def _mla_proj_cols_kernel(x_ref, cos_ref, sin_ref, cost_ref, sint_ref, w_dkv_ref, kvg_ref, w_in_ref, qg_ref,
                          w_uqt_ref, w_uk_ref, ckv_ref, krope_ref, kcat_ref, kt_ref, q_ref, gate_ref, *, tt):
    xb = x_ref[0].astype(BF16)
    kcat = _latent_kv(xb, cos_ref[...], sin_ref[...], w_dkv_ref, kvg_ref, ckv_ref, krope_ref, kcat_ref, 1, tt)
    c_t = kcat[:, :KV_LORA].T
    kt_ref[0, 0] = jnp.concatenate([c_t, jnp.ones((VEXT - KV_LORA, tt), F32)], axis=0).astype(BF16)
    cq = _query_latent(xb, w_in_ref, qg_ref, gate_ref, 1, tt)
    qall_t = _dot(w_uqt_ref[...], cq.T.astype(BF16))
    nope_w = B_HEADS * NOPE_DIM
    half = ROPE_DIM // 2
    cos_t = cost_ref[...]
    sin_t = sint_ref[...]
    zeros = jnp.zeros((ROPE_DIM, tt), F32)
    for j in range(B_HEADS // 2):
        raw = qall_t[nope_w + j * LANES:nope_w + (j + 1) * LANES]
        rot = jnp.concatenate([-raw[half:2 * half], raw[0:half], -raw[3 * half:4 * half], raw[2 * half:3 * half]],
                              axis=0)
        rp = (raw * cos_t + rot * sin_t) * Q_SCALE
        for e in range(2):
            h = 2 * j + e
            ql = _dot(w_uk_ref[h], qall_t[h * NOPE_DIM:(h + 1) * NOPE_DIM].astype(BF16)) * Q_SCALE
            rope_rows = [rp[:ROPE_DIM], zeros] if e == 0 else [zeros, rp[ROPE_DIM:]]
            q_ref[0, 0, :, h * tt:(h + 1) * tt] = jnp.concatenate([ql] + rope_rows, axis=0).astype(BF16)


def _mla_proj(x, rope_tabs, w_dkv_ext, kv_g, w_in, q_g, w_q, w_k, *, bblk, tt, transposed, tag):
    bsz, t, _ = x.shape
    const2 = lambda b, i: (0, 0)
    const3 = lambda b, i: (0, 0, 0)
    tok = lambda w: pl.BlockSpec((bblk, tt, w), lambda b, i: (b, i, 0))
    tab = pl.BlockSpec((tt, LANES), lambda b, i: (i, 0))
    in_specs = [tok(D_MODEL), tab, tab]
    out_specs = [tok(KV_LORA), tok(ROPE_DIM), tok(KCAT)]
    out_shape = [
        jax.ShapeDtypeStruct((bsz, t, KV_LORA), F32),
        jax.ShapeDtypeStruct((bsz, t, ROPE_DIM), F32),
        jax.ShapeDtypeStruct((bsz, t, KCAT), BF16),
    ]
    if transposed:
        assert bblk == 1
        tab_t = pl.BlockSpec((LANES, tt), lambda b, i: (0, i))
        in_specs += [tab_t, tab_t]
        out_specs += [pl.BlockSpec((1, 1, VEXT, tt), lambda b, i: (b, i, 0, 0)),
                      pl.BlockSpec((1, 1, KCAT, B_HEADS * tt), lambda b, i: (b, i, 0, 0))]
        out_shape += [jax.ShapeDtypeStruct((bsz, t // tt, VEXT, tt), BF16),
                      jax.ShapeDtypeStruct((bsz, t // tt, KCAT, B_HEADS * tt), BF16)]
        kern = functools.partial(_mla_proj_cols_kernel, tt=tt)
    else:
        out_specs += [pl.BlockSpec((bblk, B_HEADS, tt, KCAT), lambda b, i: (b, 0, i, 0))]
        out_shape += [jax.ShapeDtypeStruct((bsz, B_HEADS, t, KCAT), BF16)]
        kern = functools.partial(_mla_proj_rows_kernel, bblk=bblk, tt=tt)
    in_specs += [
        pl.BlockSpec(w_dkv_ext.shape, const2),
        pl.BlockSpec((1, KV_LORA), const2),
        pl.BlockSpec(w_in.shape, const2),
        pl.BlockSpec((1, Q_LORA), const2),
        pl.BlockSpec(w_q.shape, const2),
        pl.BlockSpec(w_k.shape, const3),
    ]
    out_specs += [tok(B_WIDTH)]
    out_shape += [jax.ShapeDtypeStruct((bsz, t, B_WIDTH), F32)]
    return pl.pallas_call(
        kern,
        grid=(bsz // bblk, t // tt),
        in_specs=in_specs,
        out_specs=out_specs,
        out_shape=out_shape,
        compiler_params=pltpu.CompilerParams(
            dimension_semantics=("arbitrary", "arbitrary"), vmem_limit_bytes=VMEM_LIMIT),
        name=f"mla_proj_{tag}",
    )(x, *rope_tabs, w_dkv_ext, kv_g, w_in, q_g, w_q, w_k)


def _softmax_step(s, v, m_ref, l_ref, acc_ref):
    m_prev = m_ref[...]
    m_new = jnp.maximum(m_prev, jnp.max(s, axis=1, keepdims=True))
    a = jnp.exp2(m_prev - m_new)
    p = jnp.exp2(s - m_new)
    l_ref[...] = a * l_ref[...] + jnp.sum(p, axis=1, keepdims=True)
    acc_ref[...] = a * acc_ref[...] + _dot(p.astype(BF16), v)
    m_ref[...] = m_new


def _attn_output(acc_ref, l_ref, o_ref, gate, x, w_uv_ref, w_out_ref, lng_ref, lnb_ref, tq):
    for h in range(B_HEADS):
        hs = slice(h * tq, (h + 1) * tq)
        o_lat = acc_ref[hs, :] / l_ref[hs, :]
        oh = _dot(o_lat.astype(BF16), w_uv_ref[h])
        o_ref[:, h * B_DV:(h + 1) * B_DV] = (oh * gate[:, h * B_DV:(h + 1) * B_DV]).astype(BF16)
    out = _dot(o_ref[...], w_out_ref[...])
    return _layer_norm(ALPHA * x + out, lng_ref[...], lnb_ref[...])


def _attn_prompt_kernel(qt_ref, kcat_ref, kt_ref, gate_ref, x_ref, w_uvt_ref, w_out_ref, lng_ref, lnb_ref,
                        y_ref, m_ref, acc_ref, o_ref, *, tq, rc):
    i = pl.program_id(1)
    rows = B_HEADS * tq
    n_rc = rows // rc

    def kv_step(j, visible):
        init = visible is not None
        kb = kcat_ref[0, pl.ds(pl.multiple_of(j * tq, tq), tq), :]
        ct = kt_ref[0, j]

        def scores(r):
            return _dot(kb, qt_ref[0, 0, :, r * rc:(r + 1) * rc])

        def softmax(r, st):
            ls = slice(r * rc, (r + 1) * rc)
            if init:
                st = jnp.where(visible, st, NEG_INF)
                m_new = jnp.max(st, axis=0, keepdims=True)
                a = None
            else:
                m_prev = m_ref[:, ls]
                m_new = jnp.maximum(m_prev, jnp.max(st, axis=0, keepdims=True))
                a = jnp.exp2(m_prev - m_new)
            pt = jnp.exp2(st - m_new)
            m_ref[:, ls] = m_new
            return a, pt.astype(BF16)

        st_cur = scores(0)
        pend = None
        for s in range(n_rc + 1):
            st_next = scores(s + 1) if s + 1 < n_rc else None
            if pend is not None:
                a_prev, pt_prev = pend
                pv = _dot(ct, pt_prev)
            new_pend = softmax(s, st_cur) if s < n_rc else None
            if pend is not None:
                ls = slice((s - 1) * rc, s * rc)
                acc_ref[:, ls] = pv if init else a_prev * acc_ref[:, ls] + pv
            pend = new_pend
            st_cur = st_next

    key_chunk = lax.broadcasted_iota(jnp.int32, (tq, rc), 0) >> CHUNK_SHIFT
    q_chunk = (lax.broadcasted_iota(jnp.int32, (tq, rc), 1) & (tq - 1)) >> CHUNK_SHIFT
    kv_step(i, key_chunk <= q_chunk)

    def body(j, carry):
        kv_step(j, None)
        return carry

    lax.fori_loop(0, i, body, 0)

    gate = gate_ref[0]
    for h in range(B_HEADS):
        hs = slice(h * tq, (h + 1) * tq)
        o_lat = acc_ref[0:KV_LORA, hs] / acc_ref[KV_LORA:KV_LORA + 1, hs]
        oh = _dot(w_uvt_ref[h], o_lat.astype(BF16)).T
        o_ref[:, h * B_DV:(h + 1) * B_DV] = (oh * gate[:, h * B_DV:(h + 1) * B_DV]).astype(BF16)
    out = _dot(o_ref[...], w_out_ref[...])
    y_ref[0] = _layer_norm(ALPHA * x_ref[0] + out, lng_ref[...], lnb_ref[...])


def _attn_prompt(qt, kcat, kt, gate, x, w_uvt_h, w_out, ln_g, ln_b, *, rc):
    bsz, nq, _, rows = qt.shape
    tq = rows // B_HEADS
    t = kcat.shape[1]
    assert kt.shape[-1] == tq and tq & (tq - 1) == 0 and tq % CHUNK == 0 and rc % tq == 0
    const2 = lambda b, i: (0, 0)
    kern = functools.partial(_attn_prompt_kernel, tq=tq, rc=rc)
    return pl.pallas_call(
        kern,
        grid=(bsz, nq),
        in_specs=[
            pl.BlockSpec((1, 1, KCAT, rows), lambda b, i: (b, i, 0, 0)),
            pl.BlockSpec((1, t, KCAT), lambda b, i: (b, 0, 0)),
            pl.BlockSpec((1, nq, VEXT, tq), lambda b, i: (b, 0, 0, 0)),
            pl.BlockSpec((1, tq, B_WIDTH), lambda b, i: (b, i, 0)),
            pl.BlockSpec((1, tq, D_MODEL), lambda b, i: (b, i, 0)),
            pl.BlockSpec(w_uvt_h.shape, lambda b, i: (0, 0, 0)),
            pl.BlockSpec(w_out.shape, const2),
            pl.BlockSpec((1, D_MODEL), const2),
            pl.BlockSpec((1, D_MODEL), const2),
        ],
        out_specs=pl.BlockSpec((1, tq, D_MODEL), lambda b, i: (b, i, 0)),
        out_shape=jax.ShapeDtypeStruct((bsz, t, D_MODEL), F32),
        scratch_shapes=[
            pltpu.VMEM((1, rows), F32),
            pltpu.VMEM((VEXT, rows), F32),
            pltpu.VMEM((tq, B_WIDTH), BF16),
        ],
        compiler_params=pltpu.CompilerParams(
            dimension_semantics=("arbitrary", "arbitrary"), vmem_limit_bytes=VMEM_LIMIT),
        name="mla_attend_prompt",
    )(qt, kcat, kt, gate, x, w_uvt_h, w_out, ln_g, ln_b)


def _attn_sample_kernel(q_ref, cc_ref, ckr_ref, kn_ref, gate_ref, x_ref, w_uv_ref, w_out_ref,
                        lng_ref, lnb_ref, y_ref, m_ref, l_ref, acc_ref, o_ref, *, tq, tk):
    rows = B_HEADS * tq
    past = cc_ref.shape[1]
    q = q_ref[0].reshape(rows, KCAT)
    q_lat = q[:, :KV_LORA]
    q3 = q[:, KV_LORA:].astype(F32)
    q_rope = (q3[:, :ROPE_DIM] + q3[:, ROPE_DIM:]).astype(BF16)
    m_ref[...] = jnp.full((rows, 1), NEG_INF, F32)
    l_ref[...] = jnp.zeros((rows, 1), F32)
    acc_ref[...] = jnp.zeros((rows, KV_LORA), F32)

    for j in range(past // tk):
        cb = cc_ref[0, j * tk:(j + 1) * tk, :].astype(BF16)
        krb = ckr_ref[0, j * tk:(j + 1) * tk, :].astype(BF16)
        s = (lax.dot_general(q_lat, cb, _NT, preferred_element_type=F32)
             + lax.dot_general(q_rope, krb, _NT, preferred_element_type=F32))
        _softmax_step(s, cb, m_ref, l_ref, acc_ref)

    kn = kn_ref[0]
    s = lax.dot_general(q, kn, _NT, preferred_element_type=F32)
    _softmax_step(s, kn[:, :KV_LORA], m_ref, l_ref, acc_ref)

    y_ref[0] = _attn_output(acc_ref, l_ref, o_ref, gate_ref[0], x_ref[0],
                            w_uv_ref, w_out_ref, lng_ref, lnb_ref, tq)


def _attn_sample(qcat, cache_c, cache_kr, kcat_new, gate, x, w_uv_h, w_out, ln_g, ln_b, *, tk):
    bsz, _, tq, _ = qcat.shape
    past = cache_c.shape[1]
    rows = B_HEADS * tq
    const2 = lambda b: (0, 0)
    per_b = lambda shape: pl.BlockSpec((1,) + shape, lambda b: (b,) + (0,) * len(shape))
    kern = functools.partial(_attn_sample_kernel, tq=tq, tk=tk)
    return pl.pallas_call(
        kern,
        grid=(bsz,),
        in_specs=[
            per_b((B_HEADS, tq, KCAT)),
            per_b((past, KV_LORA)),
            per_b((past, ROPE_DIM)),
            per_b((tq, KCAT)),
            per_b((tq, B_WIDTH)),
            per_b((tq, D_MODEL)),
            pl.BlockSpec(w_uv_h.shape, lambda b: (0, 0, 0)),
            pl.BlockSpec(w_out.shape, const2),
            pl.BlockSpec((1, D_MODEL), const2),
            pl.BlockSpec((1, D_MODEL), const2),
        ],
        out_specs=per_b((tq, D_MODEL)),
        out_shape=jax.ShapeDtypeStruct((bsz, tq, D_MODEL), F32),
        scratch_shapes=[
            pltpu.VMEM((rows, 1), F32),
            pltpu.VMEM((rows, 1), F32),
            pltpu.VMEM((rows, KV_LORA), F32),
            pltpu.VMEM((tq, B_WIDTH), BF16),
        ],
        compiler_params=pltpu.CompilerParams(
            dimension_semantics=("arbitrary",), vmem_limit_bytes=VMEM_LIMIT),
        name="mla_attend_sample",
    )(qcat, cache_c, cache_kr, kcat_new, gate, x, w_uv_h, w_out, ln_g, ln_b)


def _rot_cols(w):
    half = ROPE_DIM // 2
    return jnp.concatenate([-w[..., half:], w[..., :half]], axis=-1)


def _rope_tables(pos):
    half = ROPE_DIM // 2
    inv = jnp.power(ROPE_BASE, -jnp.arange(half, dtype=F32) / half)
    ang = pos.astype(F32)[:, None] * inv[None, :]
    cos, sin = jnp.cos(ang), jnp.sin(ang)
    return jnp.concatenate([cos] * 4, axis=1), jnp.concatenate([sin] * 4, axis=1)


def kernel(x_prompt, x_sample, state_hgrn, cache_ckv, cache_krope, w_in_a, lb_gamma, a_norm_g, w_out_a,
           w_dkv, kv_norm_g, w_uk, w_uv, w_in_b, q_norm_g, w_uq, w_out_b, ln_g, ln_b):
    assert N_A == 1 and DEPTH == 2
    t_p = x_prompt.shape[1]
    t_s = x_sample.shape[1]
    past = cache_ckv.shape[1]
    row = lambda a: a.reshape(1, -1)

    w_in_a_b = w_in_a[0].astype(BF16)
    w_out_a_b = w_out_a[0].astype(BF16)
    kr_w = w_dkv[:, KV_LORA:]
    w_dkv_ext = jnp.concatenate(
        [w_dkv[:, :KV_LORA], kr_w, kr_w, _rot_cols(kr_w), _rot_cols(kr_w)], axis=1).astype(BF16)
    w_uq3 = w_uq[0].reshape(Q_LORA, B_HEADS, NOPE_DIM + ROPE_DIM)
    q_rope_w = w_uq3[:, :, NOPE_DIM:]
    w_uq_ext = jnp.concatenate([
        w_uq3[:, :, :NOPE_DIM].reshape(Q_LORA, B_HEADS * NOPE_DIM),
        q_rope_w.reshape(Q_LORA, B_HEADS * ROPE_DIM),
        _rot_cols(q_rope_w).reshape(Q_LORA, B_HEADS * ROPE_DIM)], axis=1).astype(BF16)
    w_uqt = jnp.concatenate([
        w_uq3[:, :, :NOPE_DIM].reshape(Q_LORA, B_HEADS * NOPE_DIM),
        q_rope_w.reshape(Q_LORA, B_HEADS * ROPE_DIM)], axis=1).T.astype(BF16)
    w_ukt = jnp.transpose(w_uk, (1, 2, 0)).astype(BF16)
    w_uk_h = jnp.transpose(w_uk, (1, 0, 2)).astype(BF16)
    w_uv_h = jnp.transpose(w_uv, (1, 0, 2)).astype(BF16)
    w_uvt_h = jnp.transpose(w_uv, (1, 2, 0)).astype(BF16)
    w_in_b_b = w_in_b[0].astype(BF16)
    w_out_b_b = w_out_b[0].astype(BF16)

    outs = []
    for x, s0, pos0, prompt in ((x_prompt, None, 0, True), (x_sample, state_hgrn[0], past, False)):
        bsz, t, _ = x.shape
        if prompt:
            bblk, tt, chunk, sub, tm = 1, 1024, CHUNK, 256, 256
        else:
            bblk, tt, chunk, sub, tm = bsz, t, t, bsz * t, t
        x1, s_fin = _hgrn_layer(x, s0, w_in_a_b, lb_gamma, row(a_norm_g[0]), w_out_a_b,
                                row(ln_g[0]), row(ln_b[0]), layer=0, bblk=bblk, tt=tt, chunk=chunk, sub=sub)
        cos2, sin2 = _rope_tables(pos0 + jnp.arange(t, dtype=jnp.int32))
        tabs = (cos2, sin2, cos2.T, sin2.T) if prompt else (cos2, sin2)
        proj_outs = _mla_proj(
            x1, tabs, w_dkv_ext, row(kv_norm_g), w_in_b_b, row(q_norm_g[0]),
            w_uqt if prompt else w_uq_ext, w_uk_h if prompt else w_ukt,
            bblk=bblk, tt=tm, transposed=prompt, tag="prompt" if prompt else "sample")
        if prompt:
            ckv, krope, kcat, kt, qt, gate = proj_outs
            y = _attn_prompt(qt, kcat, kt, gate, x1, w_uvt_h, w_out_b_b, row(ln_g[1]), row(ln_b[1]), rc=512)
        else:
            ckv, krope, kcat, qcat, gate = proj_outs
            y = _attn_sample(qcat, cache_ckv, cache_krope, kcat, gate, x1, w_uv_h, w_out_b_b,
                             row(ln_g[1]), row(ln_b[1]), tk=512)
        outs.append((y, s_fin[None], ckv, krope))
    (y_p, s_p, c_p, kr_p), (y_s, s_s, c_s, kr_s) = outs
    return (y_p, y_s, s_p, c_p, kr_p, s_s, c_s, kr_s)
```

```python
import functools

import jax
import jax.numpy as jnp
from jax import lax
from jax.experimental import pallas as pl
from jax.experimental.pallas import tpu as pltpu

F32 = jnp.float32
BF16 = jnp.bfloat16

D_MODEL = 1024
DEPTH = 2
CHUNK = 64
CHUNK_SHIFT = 6
N_A = DEPTH // 2
A_HEADS = 8
A_DK = 128
A_DV = 128
A_WIDTH = A_HEADS * A_DV
B_HEADS = 16
Q_LORA = 512
KV_LORA = 256
NOPE_DIM = 128
ROPE_DIM = 64
B_DV = 128
B_WIDTH = B_HEADS * B_DV
ROPE_BASE = 10000.0
EPS = 1e-6
NEG_INF = -1e30
ATTN_SCALE = (NOPE_DIM + ROPE_DIM) ** -0.5
LOG2_E = 1.4426950408889634
Q_SCALE = ATTN_SCALE * LOG2_E
ALPHA = (2 * DEPTH) ** 0.25

LANES = 128
KCAT = KV_LORA + 2 * ROPE_DIM
VEXT = KV_LORA + 16
VMEM_LIMIT = 56 * 1024 * 1024

_NT = (((1,), (1,)), ((), ()))
_TN = (((0,), (0,)), ((), ()))


def _dot(a, b):
    return jnp.dot(a, b, preferred_element_type=F32)


def _layer_norm(x, g, b):
    mu = jnp.mean(x, axis=-1, keepdims=True)
    xc = x - mu
    var = jnp.mean(xc * xc, axis=-1, keepdims=True)
    return xc * lax.rsqrt(var + EPS) * g + b


def _rms_norm(x, g):
    return x * lax.rsqrt(jnp.mean(x * x, axis=-1, keepdims=True) + EPS) * g


def _sigmoid_pair(z):
    e = jnp.exp(-jnp.abs(z))
    r = 1.0 / (1.0 + e)
    er = e * r
    pos = z >= 0
    return jnp.where(pos, r, er), jnp.where(pos, er, r)


CUMSUM_TERMS = 3


def _split_bf16(x, n):
    terms = []
    for _ in range(n):
        t = x.astype(BF16)
        terms.append(t)
        x = x - t.astype(F32)
    return terms


def _hgrn_kernel(x_ref, s0_ref, w_in_ref, lbg_ref, ng_ref, w_out_ref, lng_ref, lnb_ref, tri_ref,
                 y_ref, sfin_ref, st_ref, o_ref, *, layer, bblk, tt, chunk, sub, zero_init):
    t = pl.program_id(1)
    rows = bblk * tt
    mid = chunk // 2

    @pl.when(t == 0)
    def _():
        for bb in range(bblk):
            for h in range(A_HEADS):
                if zero_init:
                    st_ref[bb, h] = jnp.zeros((A_DV, A_DK), F32)
                else:
                    st_ref[bb, h] = s0_ref[bb, h].T

    lbg = lbg_ref[...]
    e = jnp.exp(lbg - jnp.max(lbg, axis=0, keepdims=True))
    lb = jnp.sum(e[0:layer + 1], axis=0, keepdims=True) / jnp.sum(e, axis=0, keepdims=True)
    one_m_lb = 1.0 - lb
    ng = ng_ref[...]
    li = lax.broadcasted_iota(jnp.int32, (chunk, chunk), 0)
    si = lax.broadcasted_iota(jnp.int32, (chunk, chunk), 1)
    causal = li >= si

    x = x_ref[...].reshape(rows, D_MODEL)
    n_sub = rows // sub
    n_chunks = sub // chunk
    n_parts = 4
    tri = tri_ref[...]
    heads = [slice(h * A_DK, (h + 1) * A_DK) for h in range(A_HEADS)]

    def project(s_idx, part):
        xs = x[s_idx * sub:(s_idx + 1) * sub].astype(BF16)
        return _dot(xs, w_in_ref[:, part * A_WIDTH:(part + 1) * A_WIDTH])

    parts = [project(0, p) for p in range(n_parts)]
    for s_idx in range(n_sub):
        more = s_idx + 1 < n_sub
        next_parts = []
        chunks = range(n_chunks)
        rows_of = [slice(c * chunk, (c + 1) * chunk) for c in chunks]

        k_all, b_all = [], []
        for c in chunks:
            z = parts[1][rows_of[c]]
            sig_pos = 1.0 / (1.0 + jnp.exp(-z))
            sig_neg = 1.0 / (1.0 + jnp.exp(z))
            logf = jnp.log(lb + one_m_lb * sig_pos)
            k_all.append(one_m_lb * sig_neg)
            terms = _split_bf16(logf, CUMSUM_TERMS)
            b_all.append(_dot(tri, jnp.concatenate(terms, axis=0)))
        if more:
            next_parts.append(project(s_idx + 1, 0))

        q_ex, decay, v_all, sc_all, inc_all = [], [], [], [], []
        for c in chunks:
            bc = b_all[c]
            bmid = bc[mid:mid + 1]
            blast = bc[chunk - 1:chunk]
            qc = parts[0][rows_of[c]]
            q_mid = qc * jnp.exp(bc - bmid)
            k_mid = k_all[c] * jnp.exp(bmid - bc)
            q_in = q_mid.astype(BF16)
            k_in = k_mid.astype(BF16)
            k_dec = (k_mid * jnp.exp(blast - bmid)).astype(BF16)
            q_ex.append((q_mid * jnp.exp(bmid)).astype(BF16))
            decay.append(jnp.exp(blast))
            vc = parts[2][rows_of[c]].astype(BF16)
            v_all.append(vc)
            sc_all.append([lax.dot_general(q_in[:, hs], k_in[:, hs], _NT, preferred_element_type=F32)
                           for hs in heads])
            inc_all.append([lax.dot_general(vc[:, hs], k_dec[:, hs], _TN, preferred_element_type=F32)
                            for hs in heads])
        if more:
            next_parts.append(project(s_idx + 1, 1))

        st_in = []
        for c in chunks:
            bb = (s_idx * sub + c * chunk) // tt
            st_in.append([])
            for h, hs in enumerate(heads):
                st = st_ref[bb, h]
                st_in[c].append(st.astype(BF16))
                st_ref[bb, h] = st * decay[c][:, hs] + inc_all[c][h]
        if more:
            next_parts.append(project(s_idx + 1, 2))

        for c in chunks:
            r0 = s_idx * sub + c * chunk
            g = parts[3][rows_of[c]]
            silu_g = g / (1.0 + jnp.exp(-g))
            for h, hs in enumerate(heads):
                sc = jnp.where(causal, sc_all[c][h], 0.0).astype(BF16)
                o = _dot(sc, v_all[c][:, hs]) + lax.dot_general(
                    q_ex[c][:, hs], st_in[c][h], _NT, preferred_element_type=F32)
                o = _rms_norm(o, ng) * silu_g[:, hs]
                o_ref[r0:r0 + chunk, hs] = o.astype(BF16)
        if more:
            next_parts.append(project(s_idx + 1, 3))
        parts = next_parts

        ss = slice(s_idx * sub, (s_idx + 1) * sub)
        out = _dot(o_ref[ss, :], w_out_ref[...])
        y = _layer_norm(ALPHA * x[ss] + out, lng_ref[...], lnb_ref[...])
        if n_sub == 1:
            y_ref[...] = y.reshape(bblk, tt, D_MODEL)
        else:
            y_ref[0, ss, :] = y

    @pl.when(t == pl.num_programs(1) - 1)
    def _():
        for bb in range(bblk):
            for h in range(A_HEADS):
                sfin_ref[bb, h] = st_ref[bb, h].T


def _hgrn_layer(x, s0, w_in, lb_gamma, norm_g, w_out, ln_g, ln_b, *, layer, bblk, tt, chunk, sub):
    bsz, t, _ = x.shape
    rows = bblk * tt
    zero_init = s0 is None
    if zero_init:
        s0 = jnp.zeros((bblk, A_HEADS, 8, LANES), F32)
        s0_spec = pl.BlockSpec((bblk, A_HEADS, 8, LANES), lambda b, i: (0, 0, 0, 0))
    else:
        s0_spec = pl.BlockSpec((bblk, A_HEADS, A_DK, A_DV), lambda b, i: (b, 0, 0, 0))
    assert rows % sub == 0 and sub % chunk == 0 and tt % chunk == 0
    r = jnp.arange(chunk)
    tri = (r[:, None] >= r[None, :]).astype(BF16)
    tri3 = jnp.concatenate([tri] * CUMSUM_TERMS, axis=1)
    const = lambda b, i: (0, 0)
    kern = functools.partial(_hgrn_kernel, layer=layer, bblk=bblk, tt=tt, chunk=chunk, sub=sub,
                             zero_init=zero_init)
    return pl.pallas_call(
        kern,
        grid=(bsz // bblk, t // tt),
        in_specs=[
            pl.BlockSpec((bblk, tt, D_MODEL), lambda b, i: (b, i, 0)),
            s0_spec,
            pl.BlockSpec((D_MODEL, 4 * A_WIDTH), const),
            pl.BlockSpec(lb_gamma.shape, const),
            pl.BlockSpec((1, A_DV), const),
            pl.BlockSpec((A_WIDTH, D_MODEL), const),
            pl.BlockSpec((1, D_MODEL), const),
            pl.BlockSpec((1, D_MODEL), const),
            pl.BlockSpec((chunk, CUMSUM_TERMS * chunk), const),
        ],
        out_specs=[
            pl.BlockSpec((bblk, tt, D_MODEL), lambda b, i: (b, i, 0)),
            pl.BlockSpec((bblk, A_HEADS, A_DK, A_DV), lambda b, i: (b, 0, 0, 0)),
        ],
        out_shape=[
            jax.ShapeDtypeStruct((bsz, t, D_MODEL), F32),
            jax.ShapeDtypeStruct((bsz, A_HEADS, A_DK, A_DV), F32),
        ],
        scratch_shapes=[
            pltpu.VMEM((bblk, A_HEADS, A_DV, A_DK), F32),
            pltpu.VMEM((rows, A_WIDTH), BF16),
        ],
        compiler_params=pltpu.CompilerParams(
            dimension_semantics=("arbitrary", "arbitrary"), vmem_limit_bytes=VMEM_LIMIT),
        name=f"hgrn_layer_{'prompt' if zero_init else 'sample'}",
    )(x, s0, w_in, lb_gamma, norm_g, w_out, ln_g, ln_b, tri3)


def _latent_kv(xb, cos2, sin2, w_dkv_ref, kvg_ref, ckv_ref, krope_ref, kcat_ref, bblk, tt):
    kv = _dot(xb, w_dkv_ref[...])
    c = _rms_norm(kv[:, :KV_LORA], kvg_ref[...])
    krd = kv[:, KV_LORA:KV_LORA + LANES] * cos2 + kv[:, KV_LORA + LANES:KV_LORA + 2 * LANES] * sin2
    ckv_ref[...] = c.reshape(bblk, tt, KV_LORA)
    krope_ref[...] = krd[:, :ROPE_DIM].reshape(bblk, tt, ROPE_DIM)
    kcat = jnp.concatenate([c, krd], axis=1)
    kcat_ref[...] = kcat.astype(BF16).reshape(bblk, tt, KCAT)
    return kcat


def _query_latent(xb, w_in_ref, qg_ref, gate_ref, bblk, tt):
    proj = _dot(xb, w_in_ref[...])
    cq = _rms_norm(proj[:, :Q_LORA], qg_ref[...])
    gt = proj[:, Q_LORA:]
    gate_ref[...] = (gt * _sigmoid_pair(gt)[0]).reshape(bblk, tt, B_WIDTH)
    return cq


def _mla_proj_rows_kernel(x_ref, cos_ref, sin_ref, w_dkv_ref, kvg_ref, w_in_ref, qg_ref, w_uq_ref, w_ukt_ref,
                          ckv_ref, krope_ref, kcat_ref, q_ref, gate_ref, *, bblk, tt):
    rows = bblk * tt
    xb = x_ref[...].reshape(rows, D_MODEL).astype(BF16)
    cos2 = jnp.concatenate([cos_ref[...]] * bblk, axis=0)
    sin2 = jnp.concatenate([sin_ref[...]] * bblk, axis=0)
    _latent_kv(xb, cos2, sin2, w_dkv_ref, kvg_ref, ckv_ref, krope_ref, kcat_ref, bblk, tt)
    cq = _query_latent(xb, w_in_ref, qg_ref, gate_ref, bblk, tt)
    qall = _dot(cq.astype(BF16), w_uq_ref[...])
    nope_w = B_HEADS * NOPE_DIM
    pair_w = (B_HEADS // 2) * LANES
    lane = lax.broadcasted_iota(jnp.int32, (rows, LANES), 1)
    for j in range(B_HEADS // 2):
        raw = qall[:, nope_w + j * LANES:nope_w + (j + 1) * LANES]
        rot = qall[:, nope_w + pair_w + j * LANES:nope_w + pair_w + (j + 1) * LANES]
        rp = (raw * cos2 + rot * sin2) * Q_SCALE
        for e in range(2):
            h = 2 * j + e
            ql = _dot(qall[:, h * NOPE_DIM:(h + 1) * NOPE_DIM].astype(BF16), w_ukt_ref[h]) * Q_SCALE
            keep = (lane < ROPE_DIM) if e == 0 else (lane >= ROPE_DIM)
            qc = jnp.concatenate([ql, jnp.where(keep, rp, 0.0)], axis=1).astype(BF16)
            for bb in range(bblk):
                q_ref[bb, h] = qc[bb * tt:(bb + 1) * tt]


def _mla_proj_cols_kernel(x_ref, cos_ref, sin_ref, cost_ref, sint_ref, w_dkv_ref, kvg_ref, w_in_ref, qg_ref,
                          w_uqt_ref, w_uk_ref, ckv_ref, krope_ref, kcat_ref, kt_ref, q_ref, gate_ref, *, tt):
    xb = x_ref[0].astype(BF16)
    kcat = _latent_kv(xb, cos_ref[...], sin_ref[...], w_dkv_ref, kvg_ref, ckv_ref, krope_ref, kcat_ref, 1, tt)
    c_t = kcat[:, :KV_LORA].T
    kt_ref[0, 0] = jnp.concatenate([c_t, jnp.ones((VEXT - KV_LORA, tt), F32)], axis=0).astype(BF16)
    cq = _query_latent(xb, w_in_ref, qg_ref, gate_ref, 1, tt)
    qall_t = _dot(w_uqt_ref[...], cq.T.astype(BF16))
    nope_w = B_HEADS * NOPE_DIM
    half = ROPE_DIM // 2
    cos_t = cost_ref[...]
    sin_t = sint_ref[...]
    zeros = jnp.zeros((ROPE_DIM, tt), F32)
    for j in range(B_HEADS // 2):
        raw = qall_t[nope_w + j * LANES:nope_w + (j + 1) * LANES]
        rot = jnp.concatenate([-raw[half:2 * half], raw[0:half], -raw[3 * half:4 * half], raw[2 * half:3 * half]],
                              axis=0)
        rp = (raw * cos_t + rot * sin_t) * Q_SCALE
        for e in range(2):
            h = 2 * j + e
            ql = _dot(w_uk_ref[h], qall_t[h * NOPE_DIM:(h + 1) * NOPE_DIM].astype(BF16)) * Q_SCALE
            rope_rows = [rp[:ROPE_DIM], zeros] if e == 0 else [zeros, rp[ROPE_DIM:]]
            q_ref[0, 0, :, h * tt:(h + 1) * tt] = jnp.concatenate([ql] + rope_rows, axis=0).astype(BF16)


def _mla_proj(x, rope_tabs, w_dkv_ext, kv_g, w_in, q_g, w_q, w_k, *, bblk, tt, transposed, tag):
    bsz, t, _ = x.shape
    const2 = lambda b, i: (0, 0)
    const3 = lambda b, i: (0, 0, 0)
    tok = lambda w: pl.BlockSpec((bblk, tt, w), lambda b, i: (b, i, 0))
    tab = pl.BlockSpec((tt, LANES), lambda b, i: (i, 0))
    in_specs = [tok(D_MODEL), tab, tab]
    out_specs = [tok(KV_LORA), tok(ROPE_DIM), tok(KCAT)]
    out_shape = [
        jax.ShapeDtypeStruct((bsz, t, KV_LORA), F32),
        jax.ShapeDtypeStruct((bsz, t, ROPE_DIM), F32),
        jax.ShapeDtypeStruct((bsz, t, KCAT), BF16),
    ]
    if transposed:
        assert bblk == 1
        tab_t = pl.BlockSpec((LANES, tt), lambda b, i: (0, i))
        in_specs += [tab_t, tab_t]
        out_specs += [pl.BlockSpec((1, 1, VEXT, tt), lambda b, i: (b, i, 0, 0)),
                      pl.BlockSpec((1, 1, KCAT, B_HEADS * tt), lambda b, i: (b, i, 0, 0))]
        out_shape += [jax.ShapeDtypeStruct((bsz, t // tt, VEXT, tt), BF16),
                      jax.ShapeDtypeStruct((bsz, t // tt, KCAT, B_HEADS * tt), BF16)]
        kern = functools.partial(_mla_proj_cols_kernel, tt=tt)
    else:
        out_specs += [pl.BlockSpec((bblk, B_HEADS, tt, KCAT), lambda b, i: (b, 0, i, 0))]
        out_shape += [jax.ShapeDtypeStruct((bsz, B_HEADS, t, KCAT), BF16)]
        kern = functools.partial(_mla_proj_rows_kernel, bblk=bblk, tt=tt)
    in_specs += [
        pl.BlockSpec(w_dkv_ext.shape, const2),
        pl.BlockSpec((1, KV_LORA), const2),
        pl.BlockSpec(w_in.shape, const2),
        pl.BlockSpec((1, Q_LORA), const2),
        pl.BlockSpec(w_q.shape, const2),
        pl.BlockSpec(w_k.shape, const3),
    ]
    out_specs += [tok(B_WIDTH)]
    out_shape += [jax.ShapeDtypeStruct((bsz, t, B_WIDTH), F32)]
    return pl.pallas_call(
        kern,
        grid=(bsz // bblk, t // tt),
        in_specs=in_specs,
        out_specs=out_specs,
        out_shape=out_shape,
        compiler_params=pltpu.CompilerParams(
            dimension_semantics=("arbitrary", "arbitrary"), vmem_limit_bytes=VMEM_LIMIT),
        name=f"mla_proj_{tag}",
    )(x, *rope_tabs, w_dkv_ext, kv_g, w_in, q_g, w_q, w_k)


def _softmax_step(s, v, m_ref, l_ref, acc_ref):
    m_prev = m_ref[...]
    m_new = jnp.maximum(m_prev, jnp.max(s, axis=1, keepdims=True))
    a = jnp.exp2(m_prev - m_new)
    p = jnp.exp2(s - m_new)
    l_ref[...] = a * l_ref[...] + jnp.sum(p, axis=1, keepdims=True)
    acc_ref[...] = a * acc_ref[...] + _dot(p.astype(BF16), v)
    m_ref[...] = m_new


def _attn_output(acc_ref, l_ref, o_ref, gate, x, w_uv_ref, w_out_ref, lng_ref, lnb_ref, tq):
    for h in range(B_HEADS):
        hs = slice(h * tq, (h + 1) * tq)
        o_lat = acc_ref[hs, :] / l_ref[hs, :]
        oh = _dot(o_lat.astype(BF16), w_uv_ref[h])
        o_ref[:, h * B_DV:(h + 1) * B_DV] = (oh * gate[:, h * B_DV:(h + 1) * B_DV]).astype(BF16)
    out = _dot(o_ref[...], w_out_ref[...])
    return _layer_norm(ALPHA * x + out, lng_ref[...], lnb_ref[...])


def _attn_prompt_kernel(qt_ref, kcat_ref, kt_ref, gate_ref, x_ref, w_uvt_ref, w_out_ref, lng_ref, lnb_ref,
                        y_ref, m_ref, acc_ref, o_ref, *, tq, rc):
    i = pl.program_id(1)
    rows = B_HEADS * tq
    n_rc = rows // rc

    def kv_step(j, visible, nblk=1):
        init = visible is not None
        kb = kcat_ref[0, pl.ds(pl.multiple_of(j * tq, tq), nblk * tq), :]
        ct = jnp.concatenate([kt_ref[0, j + n] for n in range(nblk)], axis=1)

        def scores(r):
            return _dot(kb, qt_ref[0, 0, :, r * rc:(r + 1) * rc])

        def softmax(r, st):
            ls = slice(r * rc, (r + 1) * rc)
            if init:
                st = jnp.where(visible, st, NEG_INF)
                m_new = jnp.max(st, axis=0, keepdims=True)
                a = None
            else:
                m_prev = m_ref[:, ls]
                m_new = jnp.maximum(m_prev, jnp.max(st, axis=0, keepdims=True))
                a = jnp.exp2(m_prev - m_new)
            pt = jnp.exp2(st - m_new)
            m_ref[:, ls] = m_new
            return a, pt.astype(BF16)

        st_cur = scores(0)
        pend = None
        for s in range(n_rc + 1):
            st_next = scores(s + 1) if s + 1 < n_rc else None
            if pend is not None:
                a_prev, pt_prev = pend
                pv = _dot(ct, pt_prev)
            new_pend = softmax(s, st_cur) if s < n_rc else None
            if pend is not None:
                ls = slice((s - 1) * rc, s * rc)
                acc_ref[:, ls] = pv if init else a_prev * acc_ref[:, ls] + pv
            pend = new_pend
            st_cur = st_next

    key_chunk = lax.broadcasted_iota(jnp.int32, (tq, rc), 0) >> CHUNK_SHIFT
    q_chunk = (lax.broadcasted_iota(jnp.int32, (tq, rc), 1) & (tq - 1)) >> CHUNK_SHIFT
    kv_step(i, key_chunk <= q_chunk)

    def body(jj, carry):
        kv_step(2 * jj, None, 2)
        return carry

    lax.fori_loop(0, i // 2, body, 0)

    @pl.when(i % 2 == 1)
    def _():
        kv_step(i - 1, None)

    gate = gate_ref[0]
    for h in range(B_HEADS):
        hs = slice(h * tq, (h + 1) * tq)
        o_lat = acc_ref[0:KV_LORA, hs] / acc_ref[KV_LORA:KV_LORA + 1, hs]
        oh = _dot(w_uvt_ref[h], o_lat.astype(BF16)).T
        o_ref[:, h * B_DV:(h + 1) * B_DV] = (oh * gate[:, h * B_DV:(h + 1) * B_DV]).astype(BF16)
    out = _dot(o_ref[...], w_out_ref[...])
    y_ref[0] = _layer_norm(ALPHA * x_ref[0] + out, lng_ref[...], lnb_ref[...])


def _attn_prompt(qt, kcat, kt, gate, x, w_uvt_h, w_out, ln_g, ln_b, *, rc):
    bsz, nq, _, rows = qt.shape
    tq = rows // B_HEADS
    t = kcat.shape[1]
    assert kt.shape[-1] == tq and tq & (tq - 1) == 0 and tq % CHUNK == 0 and rc % tq == 0
    const2 = lambda b, i: (0, 0)
    kern = functools.partial(_attn_prompt_kernel, tq=tq, rc=rc)
    return pl.pallas_call(
        kern,
        grid=(bsz, nq),
        in_specs=[
            pl.BlockSpec((1, 1, KCAT, rows), lambda b, i: (b, i, 0, 0)),
            pl.BlockSpec((1, t, KCAT), lambda b, i: (b, 0, 0)),
            pl.BlockSpec((1, nq, VEXT, tq), lambda b, i: (b, 0, 0, 0)),
            pl.BlockSpec((1, tq, B_WIDTH), lambda b, i: (b, i, 0)),
            pl.BlockSpec((1, tq, D_MODEL), lambda b, i: (b, i, 0)),
            pl.BlockSpec(w_uvt_h.shape, lambda b, i: (0, 0, 0)),
            pl.BlockSpec(w_out.shape, const2),
            pl.BlockSpec((1, D_MODEL), const2),
            pl.BlockSpec((1, D_MODEL), const2),
        ],
        out_specs=pl.BlockSpec((1, tq, D_MODEL), lambda b, i: (b, i, 0)),
        out_shape=jax.ShapeDtypeStruct((bsz, t, D_MODEL), F32),
        scratch_shapes=[
            pltpu.VMEM((1, rows), F32),
            pltpu.VMEM((VEXT, rows), F32),
            pltpu.VMEM((tq, B_WIDTH), BF16),
        ],
        compiler_params=pltpu.CompilerParams(
            dimension_semantics=("arbitrary", "arbitrary"), vmem_limit_bytes=VMEM_LIMIT),
        name="mla_attend_prompt",
    )(qt, kcat, kt, gate, x, w_uvt_h, w_out, ln_g, ln_b)


def _attn_sample_kernel(q_ref, cc_ref, ckr_ref, kn_ref, gate_ref, x_ref, w_uv_ref, w_out_ref,
                        lng_ref, lnb_ref, y_ref, m_ref, l_ref, acc_ref, o_ref, *, tq, tk):
    rows = B_HEADS * tq
    past = cc_ref.shape[1]
    q = q_ref[0].reshape(rows, KCAT)
    q_lat = q[:, :KV_LORA]
    q3 = q[:, KV_LORA:].astype(F32)
    q_rope = (q3[:, :ROPE_DIM] + q3[:, ROPE_DIM:]).astype(BF16)
    m_ref[...] = jnp.full((rows, 1), NEG_INF, F32)
    l_ref[...] = jnp.zeros((rows, 1), F32)
    acc_ref[...] = jnp.zeros((rows, KV_LORA), F32)

    for j in range(past // tk):
        cb = cc_ref[0, j * tk:(j + 1) * tk, :].astype(BF16)
        krb = ckr_ref[0, j * tk:(j + 1) * tk, :].astype(BF16)
        s = (lax.dot_general(q_lat, cb, _NT, preferred_element_type=F32)
             + lax.dot_general(q_rope, krb, _NT, preferred_element_type=F32))
        _softmax_step(s, cb, m_ref, l_ref, acc_ref)

    kn = kn_ref[0]
    s = lax.dot_general(q, kn, _NT, preferred_element_type=F32)
    _softmax_step(s, kn[:, :KV_LORA], m_ref, l_ref, acc_ref)

    y_ref[0] = _attn_output(acc_ref, l_ref, o_ref, gate_ref[0], x_ref[0],
                            w_uv_ref, w_out_ref, lng_ref, lnb_ref, tq)


def _attn_sample(qcat, cache_c, cache_kr, kcat_new, gate, x, w_uv_h, w_out, ln_g, ln_b, *, tk):
    bsz, _, tq, _ = qcat.shape
    past = cache_c.shape[1]
    rows = B_HEADS * tq
    const2 = lambda b: (0, 0)
    per_b = lambda shape: pl.BlockSpec((1,) + shape, lambda b: (b,) + (0,) * len(shape))
    kern = functools.partial(_attn_sample_kernel, tq=tq, tk=tk)
    return pl.pallas_call(
        kern,
        grid=(bsz,),
        in_specs=[
            per_b((B_HEADS, tq, KCAT)),
            per_b((past, KV_LORA)),
            per_b((past, ROPE_DIM)),
            per_b((tq, KCAT)),
            per_b((tq, B_WIDTH)),
            per_b((tq, D_MODEL)),
            pl.BlockSpec(w_uv_h.shape, lambda b: (0, 0, 0)),
            pl.BlockSpec(w_out.shape, const2),
            pl.BlockSpec((1, D_MODEL), const2),
            pl.BlockSpec((1, D_MODEL), const2),
        ],
        out_specs=per_b((tq, D_MODEL)),
        out_shape=jax.ShapeDtypeStruct((bsz, tq, D_MODEL), F32),
        scratch_shapes=[
            pltpu.VMEM((rows, 1), F32),
            pltpu.VMEM((rows, 1), F32),
            pltpu.VMEM((rows, KV_LORA), F32),
            pltpu.VMEM((tq, B_WIDTH), BF16),
        ],
        compiler_params=pltpu.CompilerParams(
            dimension_semantics=("arbitrary",), vmem_limit_bytes=VMEM_LIMIT),
        name="mla_attend_sample",
    )(qcat, cache_c, cache_kr, kcat_new, gate, x, w_uv_h, w_out, ln_g, ln_b)


def _rot_cols(w):
    half = ROPE_DIM // 2
    return jnp.concatenate([-w[..., half:], w[..., :half]], axis=-1)


def _rope_tables(pos):
    half = ROPE_DIM // 2
    inv = jnp.power(ROPE_BASE, -jnp.arange(half, dtype=F32) / half)
    ang = pos.astype(F32)[:, None] * inv[None, :]
    cos, sin = jnp.cos(ang), jnp.sin(ang)
    return jnp.concatenate([cos] * 4, axis=1), jnp.concatenate([sin] * 4, axis=1)


def kernel(x_prompt, x_sample, state_hgrn, cache_ckv, cache_krope, w_in_a, lb_gamma, a_norm_g, w_out_a,
           w_dkv, kv_norm_g, w_uk, w_uv, w_in_b, q_norm_g, w_uq, w_out_b, ln_g, ln_b):
    assert N_A == 1 and DEPTH == 2
    t_p = x_prompt.shape[1]
    t_s = x_sample.shape[1]
    past = cache_ckv.shape[1]
    row = lambda a: a.reshape(1, -1)

    w_in_a_b = w_in_a[0].astype(BF16)
    w_out_a_b = w_out_a[0].astype(BF16)
    kr_w = w_dkv[:, KV_LORA:]
    w_dkv_ext = jnp.concatenate(
        [w_dkv[:, :KV_LORA], kr_w, kr_w, _rot_cols(kr_w), _rot_cols(kr_w)], axis=1).astype(BF16)
    w_uq3 = w_uq[0].reshape(Q_LORA, B_HEADS, NOPE_DIM + ROPE_DIM)
    q_rope_w = w_uq3[:, :, NOPE_DIM:]
    w_uq_ext = jnp.concatenate([
        w_uq3[:, :, :NOPE_DIM].reshape(Q_LORA, B_HEADS * NOPE_DIM),
        q_rope_w.reshape(Q_LORA, B_HEADS * ROPE_DIM),
        _rot_cols(q_rope_w).reshape(Q_LORA, B_HEADS * ROPE_DIM)], axis=1).astype(BF16)
    w_uqt = jnp.concatenate([
        w_uq3[:, :, :NOPE_DIM].reshape(Q_LORA, B_HEADS * NOPE_DIM),
        q_rope_w.reshape(Q_LORA, B_HEADS * ROPE_DIM)], axis=1).T.astype(BF16)
    w_ukt = jnp.transpose(w_uk, (1, 2, 0)).astype(BF16)
    w_uk_h = jnp.transpose(w_uk, (1, 0, 2)).astype(BF16)
    w_uv_h = jnp.transpose(w_uv, (1, 0, 2)).astype(BF16)
    w_uvt_h = jnp.transpose(w_uv, (1, 2, 0)).astype(BF16)
    w_in_b_b = w_in_b[0].astype(BF16)
    w_out_b_b = w_out_b[0].astype(BF16)

    outs = []
    for x, s0, pos0, prompt in ((x_prompt, None, 0, True), (x_sample, state_hgrn[0], past, False)):
        bsz, t, _ = x.shape
        if prompt:
            bblk, tt, chunk, sub, tm = 1, 1024, CHUNK, 256, 256
        else:
            bblk, tt, chunk, sub, tm = bsz, t, t, bsz * t, t
        x1, s_fin = _hgrn_layer(x, s0, w_in_a_b, lb_gamma, row(a_norm_g[0]), w_out_a_b,
                                row(ln_g[0]), row(ln_b[0]), layer=0, bblk=bblk, tt=tt, chunk=chunk, sub=sub)
        cos2, sin2 = _rope_tables(pos0 + jnp.arange(t, dtype=jnp.int32))
        tabs = (cos2, sin2, cos2.T, sin2.T) if prompt else (cos2, sin2)
        proj_outs = _mla_proj(
            x1, tabs, w_dkv_ext, row(kv_norm_g), w_in_b_b, row(q_norm_g[0]),
            w_uqt if prompt else w_uq_ext, w_uk_h if prompt else w_ukt,
            bblk=bblk, tt=tm, transposed=prompt, tag="prompt" if prompt else "sample")
        if prompt:
            ckv, krope, kcat, kt, qt, gate = proj_outs
            y = _attn_prompt(qt, kcat, kt, gate, x1, w_uvt_h, w_out_b_b, row(ln_g[1]), row(ln_b[1]), rc=512)
        else:
            ckv, krope, kcat, qcat, gate = proj_outs
            y = _attn_sample(qcat, cache_ckv, cache_krope, kcat, gate, x1, w_uv_h, w_out_b_b,
                             row(ln_g[1]), row(ln_b[1]), tk=512)
        outs.append((y, s_fin[None], ckv, krope))
    (y_p, s_p, c_p, kr_p), (y_s, s_s, c_s, kr_s) = outs
    return (y_p, y_s, s_p, c_p, kr_p, s_s, c_s, kr_s)
```

```python
import functools

import jax
import jax.numpy as jnp
from jax import lax
from jax.experimental import pallas as pl
from jax.experimental.pallas import tpu as pltpu

F32 = jnp.float32
BF16 = jnp.bfloat16

D_MODEL = 1024
DEPTH = 2
CHUNK = 64
CHUNK_SHIFT = 6
N_A = DEPTH // 2
A_HEADS = 8
A_DK = 128
A_DV = 128
A_WIDTH = A_HEADS * A_DV
B_HEADS = 16
Q_LORA = 512
KV_LORA = 256
NOPE_DIM = 128
ROPE_DIM = 64
B_DV = 128
B_WIDTH = B_HEADS * B_DV
ROPE_BASE = 10000.0
EPS = 1e-6
NEG_INF = -1e30
ATTN_SCALE = (NOPE_DIM + ROPE_DIM) ** -0.5
LOG2_E = 1.4426950408889634
Q_SCALE = ATTN_SCALE * LOG2_E
ALPHA = (2 * DEPTH) ** 0.25

LANES = 128
KCAT = KV_LORA + 2 * ROPE_DIM
VEXT = KV_LORA + 16
VMEM_LIMIT = 56 * 1024 * 1024

_NT = (((1,), (1,)), ((), ()))
_TN = (((0,), (0,)), ((), ()))


def _dot(a, b):
    return jnp.dot(a, b, preferred_element_type=F32)


def _layer_norm(x, g, b):
    mu = jnp.mean(x, axis=-1, keepdims=True)
    xc = x - mu
    var = jnp.mean(xc * xc, axis=-1, keepdims=True)
    return xc * lax.rsqrt(var + EPS) * g + b


def _rms_norm(x, g):
    return x * lax.rsqrt(jnp.mean(x * x, axis=-1, keepdims=True) + EPS) * g


def _sigmoid_pair(z):
    e = jnp.exp(-jnp.abs(z))
    r = 1.0 / (1.0 + e)
    er = e * r
    pos = z >= 0
    return jnp.where(pos, r, er), jnp.where(pos, er, r)


CUMSUM_TERMS = 3


def _split_bf16(x, n):
    terms = []
    for _ in range(n):
        t = x.astype(BF16)
        terms.append(t)
        x = x - t.astype(F32)
    return terms


def _hgrn_kernel(x_ref, s0_ref, w_in_ref, lbg_ref, ng_ref, w_out_ref, lng_ref, lnb_ref, tri_ref,
                 y_ref, sfin_ref, st_ref, o_ref, *, layer, bblk, tt, chunk, sub, zero_init):
    t = pl.program_id(1)
    rows = bblk * tt
    mid = chunk // 2

    @pl.when(t == 0)
    def _():
        for bb in range(bblk):
            for h in range(A_HEADS):
                if zero_init:
                    st_ref[bb, h] = jnp.zeros((A_DV, A_DK), F32)
                else:
                    st_ref[bb, h] = s0_ref[bb, h].T

    lbg = lbg_ref[...]
    e = jnp.exp(lbg - jnp.max(lbg, axis=0, keepdims=True))
    lb = jnp.sum(e[0:layer + 1], axis=0, keepdims=True) / jnp.sum(e, axis=0, keepdims=True)
    one_m_lb = 1.0 - lb
    ng = ng_ref[...]
    li = lax.broadcasted_iota(jnp.int32, (chunk, chunk), 0)
    si = lax.broadcasted_iota(jnp.int32, (chunk, chunk), 1)
    causal = li >= si

    x = x_ref[...].reshape(rows, D_MODEL)
    n_sub = rows // sub
    n_chunks = sub // chunk
    n_parts = 4
    tri = tri_ref[...]
    heads = [slice(h * A_DK, (h + 1) * A_DK) for h in range(A_HEADS)]

    def project(s_idx, part):
        xs = x[s_idx * sub:(s_idx + 1) * sub].astype(BF16)
        return _dot(xs, w_in_ref[:, part * A_WIDTH:(part + 1) * A_WIDTH])

    parts = [project(0, p) for p in range(n_parts)]
    for s_idx in range(n_sub):
        more = s_idx + 1 < n_sub
        next_parts = []
        chunks = range(n_chunks)
        rows_of = [slice(c * chunk, (c + 1) * chunk) for c in chunks]

        k_all, b_all = [], []
        for c in chunks:
            z = parts[1][rows_of[c]]
            sig_pos = 1.0 / (1.0 + jnp.exp(-z))
            sig_neg = 1.0 / (1.0 + jnp.exp(z))
            logf = jnp.log(lb + one_m_lb * sig_pos)
            k_all.append(one_m_lb * sig_neg)
            terms = _split_bf16(logf, CUMSUM_TERMS)
            b_all.append(_dot(tri, jnp.concatenate(terms, axis=0)))
        if more:
            next_parts.append(project(s_idx + 1, 0))

        q_ex, decay, v_all, sc_all, inc_all = [], [], [], [], []
        for c in chunks:
            bc = b_all[c]
            bmid = bc[mid:mid + 1]
            blast = bc[chunk - 1:chunk]
            qc = parts[0][rows_of[c]]
            q_mid = qc * jnp.exp(bc - bmid)
            k_mid = k_all[c] * jnp.exp(bmid - bc)
            q_in = q_mid.astype(BF16)
            k_in = k_mid.astype(BF16)
            k_dec = (k_mid * jnp.exp(blast - bmid)).astype(BF16)
            q_ex.append((q_mid * jnp.exp(bmid)).astype(BF16))
            decay.append(jnp.exp(blast))
            vc = parts[2][rows_of[c]].astype(BF16)
            v_all.append(vc)
            sc_all.append([lax.dot_general(q_in[:, hs], k_in[:, hs], _NT, preferred_element_type=F32)
                           for hs in heads])
            inc_all.append([lax.dot_general(vc[:, hs], k_dec[:, hs], _TN, preferred_element_type=F32)
                            for hs in heads])
        if more:
            next_parts.append(project(s_idx + 1, 1))

        st_in = []
        for c in chunks:
            bb = (s_idx * sub + c * chunk) // tt
            st_in.append([])
            for h, hs in enumerate(heads):
                st = st_ref[bb, h]
                st_in[c].append(st.T.astype(BF16))
                st_ref[bb, h] = st * decay[c][:, hs] + inc_all[c][h]
        if more:
            next_parts.append(project(s_idx + 1, 2))

        for c in chunks:
            r0 = s_idx * sub + c * chunk
            g = parts[3][rows_of[c]]
            silu_g = g / (1.0 + jnp.exp(-g))
            for h, hs in enumerate(heads):
                sc = jnp.where(causal, sc_all[c][h], 0.0).astype(BF16)
                o = _dot(jnp.concatenate([q_ex[c][:, hs], sc], axis=1),
                         jnp.concatenate([st_in[c][h], v_all[c][:, hs]], axis=0))
                o = _rms_norm(o, ng) * silu_g[:, hs]
                o_ref[r0:r0 + chunk, hs] = o.astype(BF16)
        if more:
            next_parts.append(project(s_idx + 1, 3))
        parts = next_parts

        ss = slice(s_idx * sub, (s_idx + 1) * sub)
        out = _dot(o_ref[ss, :], w_out_ref[...])
        y = _layer_norm(ALPHA * x[ss] + out, lng_ref[...], lnb_ref[...])
        if n_sub == 1:
            y_ref[...] = y.reshape(bblk, tt, D_MODEL)
        else:
            y_ref[0, ss, :] = y

    @pl.when(t == pl.num_programs(1) - 1)
    def _():
        for bb in range(bblk):
            for h in range(A_HEADS):
                sfin_ref[bb, h] = st_ref[bb, h].T


def _hgrn_layer(x, s0, w_in, lb_gamma, norm_g, w_out, ln_g, ln_b, *, layer, bblk, tt, chunk, sub):
    bsz, t, _ = x.shape
    rows = bblk * tt
    zero_init = s0 is None
    if zero_init:
        s0 = jnp.zeros((bblk, A_HEADS, 8, LANES), F32)
        s0_spec = pl.BlockSpec((bblk, A_HEADS, 8, LANES), lambda b, i: (0, 0, 0, 0))
    else:
        s0_spec = pl.BlockSpec((bblk, A_HEADS, A_DK, A_DV), lambda b, i: (b, 0, 0, 0))
    assert rows % sub == 0 and sub % chunk == 0 and tt % chunk == 0
    r = jnp.arange(chunk)
    tri = (r[:, None] >= r[None, :]).astype(BF16)
    tri3 = jnp.concatenate([tri] * CUMSUM_TERMS, axis=1)
    const = lambda b, i: (0, 0)
    kern = functools.partial(_hgrn_kernel, layer=layer, bblk=bblk, tt=tt, chunk=chunk, sub=sub,
                             zero_init=zero_init)
    return pl.pallas_call(
        kern,
        grid=(bsz // bblk, t // tt),
        in_specs=[
            pl.BlockSpec((bblk, tt, D_MODEL), lambda b, i: (b, i, 0)),
            s0_spec,
            pl.BlockSpec((D_MODEL, 4 * A_WIDTH), const),
            pl.BlockSpec(lb_gamma.shape, const),
            pl.BlockSpec((1, A_DV), const),
            pl.BlockSpec((A_WIDTH, D_MODEL), const),
            pl.BlockSpec((1, D_MODEL), const),
            pl.BlockSpec((1, D_MODEL), const),
            pl.BlockSpec((chunk, CUMSUM_TERMS * chunk), const),
        ],
        out_specs=[
            pl.BlockSpec((bblk, tt, D_MODEL), lambda b, i: (b, i, 0)),
            pl.BlockSpec((bblk, A_HEADS, A_DK, A_DV), lambda b, i: (b, 0, 0, 0)),
        ],
        out_shape=[
            jax.ShapeDtypeStruct((bsz, t, D_MODEL), F32),
            jax.ShapeDtypeStruct((bsz, A_HEADS, A_DK, A_DV), F32),
        ],
        scratch_shapes=[
            pltpu.VMEM((bblk, A_HEADS, A_DV, A_DK), F32),
            pltpu.VMEM((rows, A_WIDTH), BF16),
        ],
        compiler_params=pltpu.CompilerParams(
            dimension_semantics=("arbitrary", "arbitrary"), vmem_limit_bytes=VMEM_LIMIT),
        name=f"hgrn_layer_{'prompt' if zero_init else 'sample'}",
    )(x, s0, w_in, lb_gamma, norm_g, w_out, ln_g, ln_b, tri3)


def _latent_kv(xb, cos2, sin2, w_dkv_ref, kvg_ref, ckv_ref, krope_ref, kcat_ref, bblk, tt):
    kv = _dot(xb, w_dkv_ref[...])
    c = _rms_norm(kv[:, :KV_LORA], kvg_ref[...])
    krd = kv[:, KV_LORA:KV_LORA + LANES] * cos2 + kv[:, KV_LORA + LANES:KV_LORA + 2 * LANES] * sin2
    ckv_ref[...] = c.reshape(bblk, tt, KV_LORA)
    krope_ref[...] = krd[:, :ROPE_DIM].reshape(bblk, tt, ROPE_DIM)
    kcat = jnp.concatenate([c, krd], axis=1)
    kcat_ref[...] = kcat.astype(BF16).reshape(bblk, tt, KCAT)
    return kcat


def _query_latent(xb, w_in_ref, qg_ref, gate_ref, bblk, tt):
    proj = _dot(xb, w_in_ref[...])
    cq = _rms_norm(proj[:, :Q_LORA], qg_ref[...])
    gt = proj[:, Q_LORA:]
    gate_ref[...] = (gt * _sigmoid_pair(gt)[0]).reshape(bblk, tt, B_WIDTH)
    return cq


def _mla_proj_rows_kernel(x_ref, cos_ref, sin_ref, w_dkv_ref, kvg_ref, w_in_ref, qg_ref, w_uq_ref, w_uk_ref,
                          ckv_ref, krope_ref, kcat_ref, q_ref, gate_ref, *, bblk, tt):
    rows = bblk * tt
    xb = x_ref[...].reshape(rows, D_MODEL).astype(BF16)
    cos2 = jnp.concatenate([cos_ref[...]] * bblk, axis=0)
    sin2 = jnp.concatenate([sin_ref[...]] * bblk, axis=0)
    _latent_kv(xb, cos2, sin2, w_dkv_ref, kvg_ref, ckv_ref, krope_ref, kcat_ref, bblk, tt)
    cq = _query_latent(xb, w_in_ref, qg_ref, gate_ref, bblk, tt)
    qall = _dot(cq.astype(BF16), w_uq_ref[...])
    nope_w = B_HEADS * NOPE_DIM
    pair_w = (B_HEADS // 2) * LANES
    lane = lax.broadcasted_iota(jnp.int32, (rows, LANES), 1)
    for j in range(B_HEADS // 2):
        raw = qall[:, nope_w + j * LANES:nope_w + (j + 1) * LANES]
        rot = qall[:, nope_w + pair_w + j * LANES:nope_w + pair_w + (j + 1) * LANES]
        rp = (raw * cos2 + rot * sin2) * Q_SCALE
        for e in range(2):
            h = 2 * j + e
            ql = lax.dot_general(qall[:, h * NOPE_DIM:(h + 1) * NOPE_DIM].astype(BF16), w_uk_ref[h], _NT,
                                 preferred_element_type=F32) * Q_SCALE
            keep = (lane < ROPE_DIM) if e == 0 else (lane >= ROPE_DIM)
            qc = jnp.concatenate([ql, jnp.where(keep, rp, 0.0)], axis=1).astype(BF16)
            for bb in range(bblk):
                q_ref[bb, h] = qc[bb * tt:(bb + 1) * tt]


def _mla_proj_cols_kernel(x_ref, cos_ref, sin_ref, cost_ref, sint_ref, w_dkv_ref, kvg_ref, w_in_ref, qg_ref,
                          w_uqt_ref, w_uk_ref, ckv_ref, krope_ref, kcat_ref, kt_ref, q_ref, gate_ref, *, tt):
    xb = x_ref[0].astype(BF16)
    kcat = _latent_kv(xb, cos_ref[...], sin_ref[...], w_dkv_ref, kvg_ref, ckv_ref, krope_ref, kcat_ref, 1, tt)
    c_t = kcat[:, :KV_LORA].T
    kt_ref[0, 0] = jnp.concatenate([c_t, jnp.ones((VEXT - KV_LORA, tt), F32)], axis=0).astype(BF16)
    cq = _query_latent(xb, w_in_ref, qg_ref, gate_ref, 1, tt)
    qall_t = _dot(w_uqt_ref[...], cq.T.astype(BF16))
    nope_w = B_HEADS * NOPE_DIM
    half = ROPE_DIM // 2
    cos_t = cost_ref[...]
    sin_t = sint_ref[...]
    zeros = jnp.zeros((ROPE_DIM, tt), F32)
    for j in range(B_HEADS // 2):
        raw = qall_t[nope_w + j * LANES:nope_w + (j + 1) * LANES]
        rot = jnp.concatenate([-raw[half:2 * half], raw[0:half], -raw[3 * half:4 * half], raw[2 * half:3 * half]],
                              axis=0)
        rp = (raw * cos_t + rot * sin_t) * Q_SCALE
        for e in range(2):
            h = 2 * j + e
            ql = _dot(w_uk_ref[h], qall_t[h * NOPE_DIM:(h + 1) * NOPE_DIM].astype(BF16)) * Q_SCALE
            rope_rows = [rp[:ROPE_DIM], zeros] if e == 0 else [zeros, rp[ROPE_DIM:]]
            q_ref[0, 0, :, h * tt:(h + 1) * tt] = jnp.concatenate([ql] + rope_rows, axis=0).astype(BF16)


def _mla_proj(x, rope_tabs, w_dkv_ext, kv_g, w_in, q_g, w_q, w_k, *, bblk, tt, transposed, tag):
    bsz, t, _ = x.shape
    const2 = lambda b, i: (0, 0)
    const3 = lambda b, i: (0, 0, 0)
    tok = lambda w: pl.BlockSpec((bblk, tt, w), lambda b, i: (b, i, 0))
    tab = pl.BlockSpec((tt, LANES), lambda b, i: (i, 0))
    in_specs = [tok(D_MODEL), tab, tab]
    out_specs = [tok(KV_LORA), tok(ROPE_DIM), tok(KCAT)]
    out_shape = [
        jax.ShapeDtypeStruct((bsz, t, KV_LORA), F32),
        jax.ShapeDtypeStruct((bsz, t, ROPE_DIM), F32),
        jax.ShapeDtypeStruct((bsz, t, KCAT), BF16),
    ]
    if transposed:
        assert bblk == 1
        tab_t = pl.BlockSpec((LANES, tt), lambda b, i: (0, i))
        in_specs += [tab_t, tab_t]
        out_specs += [pl.BlockSpec((1, 1, VEXT, tt), lambda b, i: (b, i, 0, 0)),
                      pl.BlockSpec((1, 1, KCAT, B_HEADS * tt), lambda b, i: (b, i, 0, 0))]
        out_shape += [jax.ShapeDtypeStruct((bsz, t // tt, VEXT, tt), BF16),
                      jax.ShapeDtypeStruct((bsz, t // tt, KCAT, B_HEADS * tt), BF16)]
        kern = functools.partial(_mla_proj_cols_kernel, tt=tt)
    else:
        out_specs += [pl.BlockSpec((bblk, B_HEADS, tt, KCAT), lambda b, i: (b, 0, i, 0))]
        out_shape += [jax.ShapeDtypeStruct((bsz, B_HEADS, t, KCAT), BF16)]
        kern = functools.partial(_mla_proj_rows_kernel, bblk=bblk, tt=tt)
    in_specs += [
        pl.BlockSpec(w_dkv_ext.shape, const2),
        pl.BlockSpec((1, KV_LORA), const2),
        pl.BlockSpec(w_in.shape, const2),
        pl.BlockSpec((1, Q_LORA), const2),
        pl.BlockSpec(w_q.shape, const2),
        pl.BlockSpec(w_k.shape, const3),
    ]
    out_specs += [tok(B_WIDTH)]
    out_shape += [jax.ShapeDtypeStruct((bsz, t, B_WIDTH), F32)]
    return pl.pallas_call(
        kern,
        grid=(bsz // bblk, t // tt),
        in_specs=in_specs,
        out_specs=out_specs,
        out_shape=out_shape,
        compiler_params=pltpu.CompilerParams(
            dimension_semantics=("arbitrary", "arbitrary"), vmem_limit_bytes=VMEM_LIMIT),
        name=f"mla_proj_{tag}",
    )(x, *rope_tabs, w_dkv_ext, kv_g, w_in, q_g, w_q, w_k)


def _softmax_step(s, v, m_ref, l_ref, acc_ref):
    m_prev = m_ref[...]
    m_new = jnp.maximum(m_prev, jnp.max(s, axis=1, keepdims=True))
    a = jnp.exp2(m_prev - m_new)
    p = jnp.exp2(s - m_new)
    l_ref[...] = a * l_ref[...] + jnp.sum(p, axis=1, keepdims=True)
    acc_ref[...] = a * acc_ref[...] + _dot(p.astype(BF16), v)
    m_ref[...] = m_new


def _attn_output(acc_ref, l_ref, o_ref, gate, x, w_uv_ref, w_out_ref, lng_ref, lnb_ref, tq):
    for h in range(B_HEADS):
        hs = slice(h * tq, (h + 1) * tq)
        o_lat = acc_ref[hs, :] / l_ref[hs, :]
        oh = lax.dot_general(o_lat.astype(BF16), w_uv_ref[h], _NT, preferred_element_type=F32)
        o_ref[:, h * B_DV:(h + 1) * B_DV] = (oh * gate[:, h * B_DV:(h + 1) * B_DV]).astype(BF16)
    out = _dot(o_ref[...], w_out_ref[...])
    return _layer_norm(ALPHA * x + out, lng_ref[...], lnb_ref[...])


def _attn_prompt_kernel(qt_ref, kcat_ref, kt_ref, gate_ref, x_ref, w_uvt_ref, w_out_ref, lng_ref, lnb_ref,
                        y_ref, m_ref, acc_ref, o_ref, *, tq, rc):
    i = pl.program_id(1)
    rows = B_HEADS * tq
    n_rc = rows // rc

    def kv_step(j, visible, nblk=1):
        init = visible is not None
        kb = kcat_ref[0, pl.ds(pl.multiple_of(j * tq, tq), nblk * tq), :]
        ct = jnp.concatenate([kt_ref[0, j + n] for n in range(nblk)], axis=1)

        def scores(r):
            return _dot(kb, qt_ref[0, 0, :, r * rc:(r + 1) * rc])

        def softmax(r, st):
            ls = slice(r * rc, (r + 1) * rc)
            if init:
                st = jnp.where(visible, st, NEG_INF)
                m_new = jnp.max(st, axis=0, keepdims=True)
                a = None
            else:
                m_prev = m_ref[:, ls]
                m_new = jnp.maximum(m_prev, jnp.max(st, axis=0, keepdims=True))
                a = jnp.exp2(m_prev - m_new)
            pt = jnp.exp2(st - m_new)
            m_ref[:, ls] = m_new
            return a, pt.astype(BF16)

        st_cur = scores(0)
        pend = None
        for s in range(n_rc + 1):
            st_next = scores(s + 1) if s + 1 < n_rc else None
            if pend is not None:
                a_prev, pt_prev = pend
                pv = _dot(ct, pt_prev)
            new_pend = softmax(s, st_cur) if s < n_rc else None
            if pend is not None:
                ls = slice((s - 1) * rc, s * rc)
                acc_ref[:, ls] = pv if init else a_prev * acc_ref[:, ls] + pv
            pend = new_pend
            st_cur = st_next

    key_chunk = lax.broadcasted_iota(jnp.int32, (tq, rc), 0) >> CHUNK_SHIFT
    q_chunk = (lax.broadcasted_iota(jnp.int32, (tq, rc), 1) & (tq - 1)) >> CHUNK_SHIFT
    kv_step(i, key_chunk <= q_chunk)

    def body(jj, carry):
        kv_step(2 * jj, None, 2)
        return carry

    lax.fori_loop(0, i // 2, body, 0)

    @pl.when(i % 2 == 1)
    def _():
        kv_step(i - 1, None)

    gate = gate_ref[0]
    for h in range(B_HEADS):
        hs = slice(h * tq, (h + 1) * tq)
        o_lat = acc_ref[0:KV_LORA, hs] / acc_ref[KV_LORA:KV_LORA + 1, hs]
        oh = _dot(w_uvt_ref[h], o_lat.astype(BF16)).T
        o_ref[:, h * B_DV:(h + 1) * B_DV] = (oh * gate[:, h * B_DV:(h + 1) * B_DV]).astype(BF16)
    out = _dot(o_ref[...], w_out_ref[...])
    y_ref[0] = _layer_norm(ALPHA * x_ref[0] + out, lng_ref[...], lnb_ref[...])


def _attn_prompt(qt, kcat, kt, gate, x, w_uvt_h, w_out, ln_g, ln_b, *, rc):
    bsz, nq, _, rows = qt.shape
    tq = rows // B_HEADS
    t = kcat.shape[1]
    assert kt.shape[-1] == tq and tq & (tq - 1) == 0 and tq % CHUNK == 0 and rc % tq == 0
    const2 = lambda b, i: (0, 0)
    kern = functools.partial(_attn_prompt_kernel, tq=tq, rc=rc)
    return pl.pallas_call(
        kern,
        grid=(bsz, nq),
        in_specs=[
            pl.BlockSpec((1, 1, KCAT, rows), lambda b, i: (b, i, 0, 0)),
            pl.BlockSpec((1, t, KCAT), lambda b, i: (b, 0, 0)),
            pl.BlockSpec((1, nq, VEXT, tq), lambda b, i: (b, 0, 0, 0)),
            pl.BlockSpec((1, tq, B_WIDTH), lambda b, i: (b, i, 0)),
            pl.BlockSpec((1, tq, D_MODEL), lambda b, i: (b, i, 0)),
            pl.BlockSpec(w_uvt_h.shape, lambda b, i: (0, 0, 0)),
            pl.BlockSpec(w_out.shape, const2),
            pl.BlockSpec((1, D_MODEL), const2),
            pl.BlockSpec((1, D_MODEL), const2),
        ],
        out_specs=pl.BlockSpec((1, tq, D_MODEL), lambda b, i: (b, i, 0)),
        out_shape=jax.ShapeDtypeStruct((bsz, t, D_MODEL), F32),
        scratch_shapes=[
            pltpu.VMEM((1, rows), F32),
            pltpu.VMEM((VEXT, rows), F32),
            pltpu.VMEM((tq, B_WIDTH), BF16),
        ],
        compiler_params=pltpu.CompilerParams(
            dimension_semantics=("arbitrary", "arbitrary"), vmem_limit_bytes=VMEM_LIMIT),
        name="mla_attend_prompt",
    )(qt, kcat, kt, gate, x, w_uvt_h, w_out, ln_g, ln_b)


def _attn_sample_kernel(q_ref, cc_ref, ckr_ref, kn_ref, gate_ref, x_ref, w_uv_ref, w_out_ref,
                        lng_ref, lnb_ref, y_ref, m_ref, l_ref, acc_ref, o_ref, *, tq, tk):
    rows = B_HEADS * tq
    past = cc_ref.shape[1]
    q = q_ref[0].reshape(rows, KCAT)
    q_lat = q[:, :KV_LORA]
    q3 = q[:, KV_LORA:].astype(F32)
    q_rope = (q3[:, :ROPE_DIM] + q3[:, ROPE_DIM:]).astype(BF16)
    m_ref[...] = jnp.full((rows, 1), NEG_INF, F32)
    l_ref[...] = jnp.zeros((rows, 1), F32)
    acc_ref[...] = jnp.zeros((rows, KV_LORA), F32)

    for j in range(past // tk):
        cb = cc_ref[0, j * tk:(j + 1) * tk, :].astype(BF16)
        krb = ckr_ref[0, j * tk:(j + 1) * tk, :].astype(BF16)
        s = (lax.dot_general(q_lat, cb, _NT, preferred_element_type=F32)
             + lax.dot_general(q_rope, krb, _NT, preferred_element_type=F32))
        _softmax_step(s, cb, m_ref, l_ref, acc_ref)

    kn = kn_ref[0]
    s = lax.dot_general(q, kn, _NT, preferred_element_type=F32)
    _softmax_step(s, kn[:, :KV_LORA], m_ref, l_ref, acc_ref)

    y_ref[0] = _attn_output(acc_ref, l_ref, o_ref, gate_ref[0], x_ref[0],
                            w_uv_ref, w_out_ref, lng_ref, lnb_ref, tq)


def _attn_sample(qcat, cache_c, cache_kr, kcat_new, gate, x, w_uv_h, w_out, ln_g, ln_b, *, tk):
    bsz, _, tq, _ = qcat.shape
    past = cache_c.shape[1]
    rows = B_HEADS * tq
    const2 = lambda b: (0, 0)
    per_b = lambda shape: pl.BlockSpec((1,) + shape, lambda b: (b,) + (0,) * len(shape))
    kern = functools.partial(_attn_sample_kernel, tq=tq, tk=tk)
    return pl.pallas_call(
        kern,
        grid=(bsz,),
        in_specs=[
            per_b((B_HEADS, tq, KCAT)),
            per_b((past, KV_LORA)),
            per_b((past, ROPE_DIM)),
            per_b((tq, KCAT)),
            per_b((tq, B_WIDTH)),
            per_b((tq, D_MODEL)),
            pl.BlockSpec(w_uv_h.shape, lambda b: (0, 0, 0)),
            pl.BlockSpec(w_out.shape, const2),
            pl.BlockSpec((1, D_MODEL), const2),
            pl.BlockSpec((1, D_MODEL), const2),
        ],
        out_specs=per_b((tq, D_MODEL)),
        out_shape=jax.ShapeDtypeStruct((bsz, tq, D_MODEL), F32),
        scratch_shapes=[
            pltpu.VMEM((rows, 1), F32),
            pltpu.VMEM((rows, 1), F32),
            pltpu.VMEM((rows, KV_LORA), F32),
            pltpu.VMEM((tq, B_WIDTH), BF16),
        ],
        compiler_params=pltpu.CompilerParams(
            dimension_semantics=("arbitrary",), vmem_limit_bytes=VMEM_LIMIT),
        name="mla_attend_sample",
    )(qcat, cache_c, cache_kr, kcat_new, gate, x, w_uv_h, w_out, ln_g, ln_b)


def _rot_cols(w):
    half = ROPE_DIM // 2
    return jnp.concatenate([-w[..., half:], w[..., :half]], axis=-1)


def _rope_tables(pos):
    half = ROPE_DIM // 2
    inv = jnp.power(ROPE_BASE, -jnp.arange(half, dtype=F32) / half)
    ang = pos.astype(F32)[:, None] * inv[None, :]
    cos, sin = jnp.cos(ang), jnp.sin(ang)
    return jnp.concatenate([cos] * 4, axis=1), jnp.concatenate([sin] * 4, axis=1)


def kernel(x_prompt, x_sample, state_hgrn, cache_ckv, cache_krope, w_in_a, lb_gamma, a_norm_g, w_out_a,
           w_dkv, kv_norm_g, w_uk, w_uv, w_in_b, q_norm_g, w_uq, w_out_b, ln_g, ln_b):
    assert N_A == 1 and DEPTH == 2
    t_p = x_prompt.shape[1]
    t_s = x_sample.shape[1]
    past = cache_ckv.shape[1]
    row = lambda a: a.reshape(1, -1)

    w_in_a_b = w_in_a[0].astype(BF16)
    w_out_a_b = w_out_a[0].astype(BF16)
    kr_w = w_dkv[:, KV_LORA:]
    w_dkv_ext = jnp.concatenate(
        [w_dkv[:, :KV_LORA], kr_w, kr_w, _rot_cols(kr_w), _rot_cols(kr_w)], axis=1).astype(BF16)
    w_uq3 = w_uq[0].reshape(Q_LORA, B_HEADS, NOPE_DIM + ROPE_DIM)
    q_rope_w = w_uq3[:, :, NOPE_DIM:]
    w_uq_ext = jnp.concatenate([
        w_uq3[:, :, :NOPE_DIM].reshape(Q_LORA, B_HEADS * NOPE_DIM),
        q_rope_w.reshape(Q_LORA, B_HEADS * ROPE_DIM),
        _rot_cols(q_rope_w).reshape(Q_LORA, B_HEADS * ROPE_DIM)], axis=1).astype(BF16)
    w_uqt = jnp.concatenate([
        w_uq3[:, :, :NOPE_DIM].reshape(Q_LORA, B_HEADS * NOPE_DIM),
        q_rope_w.reshape(Q_LORA, B_HEADS * ROPE_DIM)], axis=1).T.astype(BF16)
    w_uk_h = jnp.transpose(w_uk, (1, 0, 2)).astype(BF16)
    w_uvt_h = jnp.transpose(w_uv, (1, 2, 0)).astype(BF16)
    w_in_b_b = w_in_b[0].astype(BF16)
    w_out_b_b = w_out_b[0].astype(BF16)

    outs = []
    for x, s0, pos0, prompt in ((x_prompt, None, 0, True), (x_sample, state_hgrn[0], past, False)):
        bsz, t, _ = x.shape
        if prompt:
            bblk, tt, chunk, sub, tm = 1, 1024, CHUNK, 256, 256
        else:
            bblk, tt, chunk, sub, tm = bsz, t, t, bsz * t, t
        x1, s_fin = _hgrn_layer(x, s0, w_in_a_b, lb_gamma, row(a_norm_g[0]), w_out_a_b,
                                row(ln_g[0]), row(ln_b[0]), layer=0, bblk=bblk, tt=tt, chunk=chunk, sub=sub)
        cos2, sin2 = _rope_tables(pos0 + jnp.arange(t, dtype=jnp.int32))
        tabs = (cos2, sin2, cos2.T, sin2.T) if prompt else (cos2, sin2)
        proj_outs = _mla_proj(
            x1, tabs, w_dkv_ext, row(kv_norm_g), w_in_b_b, row(q_norm_g[0]),
            w_uqt if prompt else w_uq_ext, w_uk_h,
            bblk=bblk, tt=tm, transposed=prompt, tag="prompt" if prompt else "sample")
        if prompt:
            ckv, krope, kcat, kt, qt, gate = proj_outs
            y = _attn_prompt(qt, kcat, kt, gate, x1, w_uvt_h, w_out_b_b, row(ln_g[1]), row(ln_b[1]), rc=512)
        else:
            ckv, krope, kcat, qcat, gate = proj_outs
            y = _attn_sample(qcat, cache_ckv, cache_krope, kcat, gate, x1, w_uvt_h, w_out_b_b,
                             row(ln_g[1]), row(ln_b[1]), tk=512)
        outs.append((y, s_fin[None], ckv, krope))
    (y_p, s_p, c_p, kr_p), (y_s, s_s, c_s, kr_s) = outs
    return (y_p, y_s, s_p, c_p, kr_p, s_s, c_s, kr_s)
```

```python
import functools

import jax
import jax.numpy as jnp
from jax import lax
from jax.experimental import pallas as pl
from jax.experimental.pallas import tpu as pltpu

F32 = jnp.float32
BF16 = jnp.bfloat16

D_MODEL = 1024
DEPTH = 2
CHUNK = 64
CHUNK_SHIFT = 6
N_A = DEPTH // 2
A_HEADS = 8
A_DK = 128
A_DV = 128
A_WIDTH = A_HEADS * A_DV
B_HEADS = 16
Q_LORA = 512
KV_LORA = 256
NOPE_DIM = 128
ROPE_DIM = 64
B_DV = 128
B_WIDTH = B_HEADS * B_DV
ROPE_BASE = 10000.0
EPS = 1e-6
NEG_INF = -1e30
ATTN_SCALE = (NOPE_DIM + ROPE_DIM) ** -0.5
LOG2_E = 1.4426950408889634
Q_SCALE = ATTN_SCALE * LOG2_E
ALPHA = (2 * DEPTH) ** 0.25

LANES = 128
KCAT = KV_LORA + 2 * ROPE_DIM
VEXT = KV_LORA + 16
VMEM_LIMIT = 56 * 1024 * 1024

_NT = (((1,), (1,)), ((), ()))
_TN = (((0,), (0,)), ((), ()))


def _dot(a, b):
    return jnp.dot(a, b, preferred_element_type=F32)


def _layer_norm(x, g, b):
    mu = jnp.mean(x, axis=-1, keepdims=True)
    xc = x - mu
    var = jnp.mean(xc * xc, axis=-1, keepdims=True)
    return xc * lax.rsqrt(var + EPS) * g + b


def _rms_norm(x, g):
    return x * lax.rsqrt(jnp.mean(x * x, axis=-1, keepdims=True) + EPS) * g


def _sigmoid_pair(z):
    e = jnp.exp(-jnp.abs(z))
    r = 1.0 / (1.0 + e)
    er = e * r
    pos = z >= 0
    return jnp.where(pos, r, er), jnp.where(pos, er, r)


CUMSUM_TERMS = 3


def _split_bf16(x, n):
    terms = []
    for _ in range(n):
        t = x.astype(BF16)
        terms.append(t)
        x = x - t.astype(F32)
    return terms


def _hgrn_kernel(x_ref, s0_ref, w_in_ref, lbg_ref, ng_ref, w_out_ref, lng_ref, lnb_ref, tri_ref,
                 y_ref, sfin_ref, st_ref, o_ref, *, layer, bblk, tt, chunk, sub, zero_init):
    t = pl.program_id(1)
    rows = bblk * tt
    mid = chunk // 2

    @pl.when(t == 0)
    def _():
        for bb in range(bblk):
            for h in range(A_HEADS):
                if zero_init:
                    st_ref[bb, h] = jnp.zeros((A_DV, A_DK), F32)
                else:
                    st_ref[bb, h] = s0_ref[bb, h].T

    lbg = lbg_ref[...]
    e = jnp.exp(lbg - jnp.max(lbg, axis=0, keepdims=True))
    lb = jnp.sum(e[0:layer + 1], axis=0, keepdims=True) / jnp.sum(e, axis=0, keepdims=True)
    one_m_lb = 1.0 - lb
    ng = ng_ref[...]
    li = lax.broadcasted_iota(jnp.int32, (chunk, chunk), 0)
    si = lax.broadcasted_iota(jnp.int32, (chunk, chunk), 1)
    causal = li >= si

    x = x_ref[...].reshape(rows, D_MODEL)
    n_sub = rows // sub
    n_chunks = sub // chunk
    n_parts = 4
    tri = tri_ref[...]
    heads = [slice(h * A_DK, (h + 1) * A_DK) for h in range(A_HEADS)]

    def project(s_idx, part):
        xs = x[s_idx * sub:(s_idx + 1) * sub].astype(BF16)
        return _dot(xs, w_in_ref[:, part * A_WIDTH:(part + 1) * A_WIDTH])

    parts = [project(0, p) for p in range(n_parts)]
    for s_idx in range(n_sub):
        more = s_idx + 1 < n_sub
        next_parts = []
        chunks = range(n_chunks)
        rows_of = [slice(c * chunk, (c + 1) * chunk) for c in chunks]

        k_all, b_all = [], []
        for c in chunks:
            z = parts[1][rows_of[c]]
            sig_pos = 1.0 / (1.0 + jnp.exp(-z))
            sig_neg = 1.0 / (1.0 + jnp.exp(z))
            logf = jnp.log(lb + one_m_lb * sig_pos)
            k_all.append(one_m_lb * sig_neg)
            terms = _split_bf16(logf, CUMSUM_TERMS)
            b_all.append(_dot(tri, jnp.concatenate(terms, axis=0)))
        if more:
            next_parts.append(project(s_idx + 1, 0))

        q_ex, decay, v_all, sc_all, inc_all = [], [], [], [], []
        for c in chunks:
            bc = b_all[c]
            bmid = bc[mid:mid + 1]
            blast = bc[chunk - 1:chunk]
            qc = parts[0][rows_of[c]]
            q_mid = qc * jnp.exp(bc - bmid)
            k_mid = k_all[c] * jnp.exp(bmid - bc)
            q_in = q_mid.astype(BF16)
            k_in = k_mid.astype(BF16)
            k_dec = (k_mid * jnp.exp(blast - bmid)).astype(BF16)
            q_ex.append((q_mid * jnp.exp(bmid)).astype(BF16))
            decay.append(jnp.exp(blast))
            vc = parts[2][rows_of[c]].astype(BF16)
            v_all.append(vc)
            sc_all.append([lax.dot_general(q_in[:, hs], k_in[:, hs], _NT, preferred_element_type=F32)
                           for hs in heads])
            inc_all.append([lax.dot_general(vc[:, hs], k_dec[:, hs], _TN, preferred_element_type=F32)
                            for hs in heads])
        if more:
            next_parts.append(project(s_idx + 1, 1))

        st_in = []
        for c in chunks:
            bb = (s_idx * sub + c * chunk) // tt
            st_in.append([])
            for h, hs in enumerate(heads):
                st = st_ref[bb, h]
                st_in[c].append(st.T.astype(BF16))
                st_ref[bb, h] = st * decay[c][:, hs] + inc_all[c][h]
        if more:
            next_parts.append(project(s_idx + 1, 2))

        for c in chunks:
            r0 = s_idx * sub + c * chunk
            g = parts[3][rows_of[c]]
            silu_g = g / (1.0 + jnp.exp(-g))
            for h, hs in enumerate(heads):
                sc = jnp.where(causal, sc_all[c][h], 0.0).astype(BF16)
                o = _dot(jnp.concatenate([q_ex[c][:, hs], sc], axis=1),
                         jnp.concatenate([st_in[c][h], v_all[c][:, hs]], axis=0))
                o = _rms_norm(o, ng) * silu_g[:, hs]
                o_ref[r0:r0 + chunk, hs] = o.astype(BF16)
        if more:
            next_parts.append(project(s_idx + 1, 3))
        parts = next_parts

        ss = slice(s_idx * sub, (s_idx + 1) * sub)
        out = _dot(o_ref[ss, :], w_out_ref[...])
        y = _layer_norm(ALPHA * x[ss] + out, lng_ref[...], lnb_ref[...])
        if n_sub == 1:
            y_ref[...] = y.reshape(bblk, tt, D_MODEL)
        else:
            y_ref[0, ss, :] = y

    @pl.when(t == pl.num_programs(1) - 1)
    def _():
        for bb in range(bblk):
            for h in range(A_HEADS):
                sfin_ref[bb, h] = st_ref[bb, h].T


def _hgrn_layer(x, s0, w_in, lb_gamma, norm_g, w_out, ln_g, ln_b, *, layer, bblk, tt, chunk, sub):
    bsz, t, _ = x.shape
    rows = bblk * tt
    zero_init = s0 is None
    if zero_init:
        s0 = jnp.zeros((bblk, A_HEADS, 8, LANES), F32)
        s0_spec = pl.BlockSpec((bblk, A_HEADS, 8, LANES), lambda b, i: (0, 0, 0, 0))
    else:
        s0_spec = pl.BlockSpec((bblk, A_HEADS, A_DK, A_DV), lambda b, i: (b, 0, 0, 0))
    assert rows % sub == 0 and sub % chunk == 0 and tt % chunk == 0
    r = jnp.arange(chunk)
    tri = (r[:, None] >= r[None, :]).astype(BF16)
    tri3 = jnp.concatenate([tri] * CUMSUM_TERMS, axis=1)
    const = lambda b, i: (0, 0)
    kern = functools.partial(_hgrn_kernel, layer=layer, bblk=bblk, tt=tt, chunk=chunk, sub=sub,
                             zero_init=zero_init)
    return pl.pallas_call(
        kern,
        grid=(bsz // bblk, t // tt),
        in_specs=[
            pl.BlockSpec((bblk, tt, D_MODEL), lambda b, i: (b, i, 0)),
            s0_spec,
            pl.BlockSpec((D_MODEL, 4 * A_WIDTH), const),
            pl.BlockSpec(lb_gamma.shape, const),
            pl.BlockSpec((1, A_DV), const),
            pl.BlockSpec((A_WIDTH, D_MODEL), const),
            pl.BlockSpec((1, D_MODEL), const),
            pl.BlockSpec((1, D_MODEL), const),
            pl.BlockSpec((chunk, CUMSUM_TERMS * chunk), const),
        ],
        out_specs=[
            pl.BlockSpec((bblk, tt, D_MODEL), lambda b, i: (b, i, 0)),
            pl.BlockSpec((bblk, A_HEADS, A_DK, A_DV), lambda b, i: (b, 0, 0, 0)),
        ],
        out_shape=[
            jax.ShapeDtypeStruct((bsz, t, D_MODEL), F32),
            jax.ShapeDtypeStruct((bsz, A_HEADS, A_DK, A_DV), F32),
        ],
        scratch_shapes=[
            pltpu.VMEM((bblk, A_HEADS, A_DV, A_DK), F32),
            pltpu.VMEM((rows, A_WIDTH), BF16),
        ],
        compiler_params=pltpu.CompilerParams(
            dimension_semantics=("arbitrary", "arbitrary"), vmem_limit_bytes=VMEM_LIMIT),
        name=f"hgrn_layer_{'prompt' if zero_init else 'sample'}",
    )(x, s0, w_in, lb_gamma, norm_g, w_out, ln_g, ln_b, tri3)


def _latent_kv(xb, cos2, sin2, w_dkv_ref, kvg_ref, ckv_ref, krope_ref, kcat_ref, bblk, tt):
    kv = _dot(xb, w_dkv_ref[...])
    c = _rms_norm(kv[:, :KV_LORA], kvg_ref[...])
    krd = kv[:, KV_LORA:KV_LORA + LANES] * cos2 + kv[:, KV_LORA + LANES:KV_LORA + 2 * LANES] * sin2
    ckv_ref[...] = c.reshape(bblk, tt, KV_LORA)
    krope_ref[...] = krd[:, :ROPE_DIM].reshape(bblk, tt, ROPE_DIM)
    kcat = jnp.concatenate([c, krd], axis=1)
    kcat_ref[...] = kcat.astype(BF16).reshape(bblk, tt, KCAT)
    return kcat


def _query_latent(xb, w_in_ref, qg_ref, gate_ref, bblk, tt):
    proj = _dot(xb, w_in_ref[...])
    cq = _rms_norm(proj[:, :Q_LORA], qg_ref[...])
    gt = proj[:, Q_LORA:]
    gate_ref[...] = (gt * _sigmoid_pair(gt)[0]).reshape(bblk, tt, B_WIDTH)
    return cq


def _mla_proj_rows_kernel(x_ref, cos_ref, sin_ref, w_dkv_ref, kvg_ref, w_in_ref, qg_ref, w_uq_ref, w_uk_ref,
                          ckv_ref, krope_ref, kcat_ref, q_ref, gate_ref, *, bblk, tt):
    rows = bblk * tt
    xb = x_ref[...].reshape(rows, D_MODEL).astype(BF16)
    cos2 = jnp.concatenate([cos_ref[...]] * bblk, axis=0)
    sin2 = jnp.concatenate([sin_ref[...]] * bblk, axis=0)
    _latent_kv(xb, cos2, sin2, w_dkv_ref, kvg_ref, ckv_ref, krope_ref, kcat_ref, bblk, tt)
    cq = _query_latent(xb, w_in_ref, qg_ref, gate_ref, bblk, tt)
    qall = _dot(cq.astype(BF16), w_uq_ref[...])
    nope_w = B_HEADS * NOPE_DIM
    pair_w = (B_HEADS // 2) * LANES
    lane = lax.broadcasted_iota(jnp.int32, (rows, LANES), 1)
    for j in range(B_HEADS // 2):
        raw = qall[:, nope_w + j * LANES:nope_w + (j + 1) * LANES]
        rot = qall[:, nope_w + pair_w + j * LANES:nope_w + pair_w + (j + 1) * LANES]
        rp = (raw * cos2 + rot * sin2) * Q_SCALE
        for e in range(2):
            h = 2 * j + e
            ql = lax.dot_general(qall[:, h * NOPE_DIM:(h + 1) * NOPE_DIM].astype(BF16), w_uk_ref[h], _NT,
                                 preferred_element_type=F32) * Q_SCALE
            keep = (lane < ROPE_DIM) if e == 0 else (lane >= ROPE_DIM)
            qc = jnp.concatenate([ql, jnp.where(keep, rp, 0.0)], axis=1).astype(BF16)
            for bb in range(bblk):
                q_ref[bb, h] = qc[bb * tt:(bb + 1) * tt]


def _mla_proj_cols_kernel(x_ref, cos_ref, sin_ref, cost_ref, sint_ref, w_dkv_ref, kvg_ref, w_in_ref, qg_ref,
                          w_uqt_ref, w_uk_ref, ckv_ref, krope_ref, kcat_ref, kt_ref, q_ref, gate_ref, *, tt):
    xb = x_ref[0].astype(BF16)
    kcat = _latent_kv(xb, cos_ref[...], sin_ref[...], w_dkv_ref, kvg_ref, ckv_ref, krope_ref, kcat_ref, 1, tt)
    c_t = kcat[:, :KV_LORA].T
    kt_ref[0, 0] = jnp.concatenate([c_t, jnp.ones((VEXT - KV_LORA, tt), F32)], axis=0).astype(BF16)
    cq = _query_latent(xb, w_in_ref, qg_ref, gate_ref, 1, tt)
    qall_t = _dot(w_uqt_ref[...], cq.T.astype(BF16))
    nope_w = B_HEADS * NOPE_DIM
    half = ROPE_DIM // 2
    cos_t = cost_ref[...]
    sin_t = sint_ref[...]
    zeros = jnp.zeros((ROPE_DIM, tt), F32)
    for j in range(B_HEADS // 2):
        raw = qall_t[nope_w + j * LANES:nope_w + (j + 1) * LANES]
        rot = jnp.concatenate([-raw[half:2 * half], raw[0:half], -raw[3 * half:4 * half], raw[2 * half:3 * half]],
                              axis=0)
        rp = (raw * cos_t + rot * sin_t) * Q_SCALE
        for e in range(2):
            h = 2 * j + e
            ql = _dot(w_uk_ref[h], qall_t[h * NOPE_DIM:(h + 1) * NOPE_DIM].astype(BF16)) * Q_SCALE
            rope_rows = [rp[:ROPE_DIM], zeros] if e == 0 else [zeros, rp[ROPE_DIM:]]
            q_ref[0, 0, :, h * tt:(h + 1) * tt] = jnp.concatenate([ql] + rope_rows, axis=0).astype(BF16)


def _mla_proj(x, rope_tabs, w_dkv_ext, kv_g, w_in, q_g, w_q, w_k, *, bblk, tt, transposed, tag):
    bsz, t, _ = x.shape
    const2 = lambda b, i: (0, 0)
    const3 = lambda b, i: (0, 0, 0)
    tok = lambda w: pl.BlockSpec((bblk, tt, w), lambda b, i: (b, i, 0))
    tab = pl.BlockSpec((tt, LANES), lambda b, i: (i, 0))
    in_specs = [tok(D_MODEL), tab, tab]
    out_specs = [tok(KV_LORA), tok(ROPE_DIM), tok(KCAT)]
    out_shape = [
        jax.ShapeDtypeStruct((bsz, t, KV_LORA), F32),
        jax.ShapeDtypeStruct((bsz, t, ROPE_DIM), F32),
        jax.ShapeDtypeStruct((bsz, t, KCAT), BF16),
    ]
    if transposed:
        assert bblk == 1
        tab_t = pl.BlockSpec((LANES, tt), lambda b, i: (0, i))
        in_specs += [tab_t, tab_t]
        out_specs += [pl.BlockSpec((1, 1, VEXT, tt), lambda b, i: (b, i, 0, 0)),
                      pl.BlockSpec((1, 1, KCAT, B_HEADS * tt), lambda b, i: (b, i, 0, 0))]
        out_shape += [jax.ShapeDtypeStruct((bsz, t // tt, VEXT, tt), BF16),
                      jax.ShapeDtypeStruct((bsz, t // tt, KCAT, B_HEADS * tt), BF16)]
        kern = functools.partial(_mla_proj_cols_kernel, tt=tt)
    else:
        out_specs += [pl.BlockSpec((bblk, B_HEADS, tt, KCAT), lambda b, i: (b, 0, i, 0))]
        out_shape += [jax.ShapeDtypeStruct((bsz, B_HEADS, t, KCAT), BF16)]
        kern = functools.partial(_mla_proj_rows_kernel, bblk=bblk, tt=tt)
    in_specs += [
        pl.BlockSpec(w_dkv_ext.shape, const2),
        pl.BlockSpec((1, KV_LORA), const2),
        pl.BlockSpec(w_in.shape, const2),
        pl.BlockSpec((1, Q_LORA), const2),
        pl.BlockSpec(w_q.shape, const2),
        pl.BlockSpec(w_k.shape, const3),
    ]
    out_specs += [tok(B_WIDTH)]
    out_shape += [jax.ShapeDtypeStruct((bsz, t, B_WIDTH), F32)]
    return pl.pallas_call(
        kern,
        grid=(bsz // bblk, t // tt),
        in_specs=in_specs,
        out_specs=out_specs,
        out_shape=out_shape,
        compiler_params=pltpu.CompilerParams(
            dimension_semantics=("arbitrary", "arbitrary"), vmem_limit_bytes=VMEM_LIMIT),
        name=f"mla_proj_{tag}",
    )(x, *rope_tabs, w_dkv_ext, kv_g, w_in, q_g, w_q, w_k)


def _softmax_step(s, v, m_ref, l_ref, acc_ref):
    m_prev = m_ref[...]
    m_new = jnp.maximum(m_prev, jnp.max(s, axis=1, keepdims=True))
    a = jnp.exp2(m_prev - m_new)
    p = jnp.exp2(s - m_new)
    l_ref[...] = a * l_ref[...] + jnp.sum(p, axis=1, keepdims=True)
    acc_ref[...] = a * acc_ref[...] + _dot(p.astype(BF16), v)
    m_ref[...] = m_new


def _attn_output(acc_ref, l_ref, o_ref, gate, x, w_uv_ref, w_out_ref, lng_ref, lnb_ref, tq):
    for h in range(B_HEADS):
        hs = slice(h * tq, (h + 1) * tq)
        o_lat = acc_ref[hs, :] / l_ref[hs, :]
        oh = lax.dot_general(o_lat.astype(BF16), w_uv_ref[h], _NT, preferred_element_type=F32)
        o_ref[:, h * B_DV:(h + 1) * B_DV] = (oh * gate[:, h * B_DV:(h + 1) * B_DV]).astype(BF16)
    out = _dot(o_ref[...], w_out_ref[...])
    return _layer_norm(ALPHA * x + out, lng_ref[...], lnb_ref[...])


def _attn_prompt_kernel(qt_ref, kcat_ref, kt_ref, gate_ref, x_ref, w_uvt_ref, w_out_ref, lng_ref, lnb_ref,
                        y_ref, m_ref, acc_ref, o_ref, *, tq, rc):
    i = pl.program_id(1)
    rows = B_HEADS * tq
    n_rc = rows // rc

    def kv_step(j, visible, nblk=1):
        init = visible is not None
        kb = kcat_ref[0, pl.ds(pl.multiple_of(j * tq, tq), nblk * tq), :]
        ct = jnp.concatenate([kt_ref[0, j + n] for n in range(nblk)], axis=1)

        def scores(r):
            return _dot(kb, qt_ref[0, 0, :, r * rc:(r + 1) * rc])

        def softmax(r, st):
            ls = slice(r * rc, (r + 1) * rc)
            if init:
                st = jnp.where(visible, st, NEG_INF)
                m_new = jnp.max(st, axis=0, keepdims=True)
                a = None
            else:
                m_prev = m_ref[:, ls]
                m_new = jnp.maximum(m_prev, jnp.max(st, axis=0, keepdims=True))
                a = jnp.exp2(m_prev - m_new)
            pt = jnp.exp2(st - m_new)
            m_ref[:, ls] = m_new
            return a, pt.astype(BF16)

        st_cur = scores(0)
        pend = None
        for s in range(n_rc + 1):
            st_next = scores(s + 1) if s + 1 < n_rc else None
            if pend is not None:
                a_prev, pt_prev = pend
                pv = _dot(ct, pt_prev)
            new_pend = softmax(s, st_cur) if s < n_rc else None
            if pend is not None:
                ls = slice((s - 1) * rc, s * rc)
                acc_ref[:, ls] = pv if init else a_prev * acc_ref[:, ls] + pv
            pend = new_pend
            st_cur = st_next

    key_chunk = lax.broadcasted_iota(jnp.int32, (tq, rc), 0) >> CHUNK_SHIFT
    q_chunk = (lax.broadcasted_iota(jnp.int32, (tq, rc), 1) & (tq - 1)) >> CHUNK_SHIFT
    kv_step(i, key_chunk <= q_chunk)

    def body(jj, carry):
        kv_step(4 * jj, None, 4)
        return carry

    n_quads = i // 4
    lax.fori_loop(0, n_quads, body, 0)

    @pl.when(i - 4 * n_quads >= 2)
    def _():
        kv_step(4 * n_quads, None, 2)

    @pl.when(i % 2 == 1)
    def _():
        kv_step(i - 1, None)

    gate = gate_ref[0]
    for h in range(B_HEADS):
        hs = slice(h * tq, (h + 1) * tq)
        o_lat = acc_ref[0:KV_LORA, hs] / acc_ref[KV_LORA:KV_LORA + 1, hs]
        oh = _dot(w_uvt_ref[h], o_lat.astype(BF16)).T
        o_ref[:, h * B_DV:(h + 1) * B_DV] = (oh * gate[:, h * B_DV:(h + 1) * B_DV]).astype(BF16)
    out = _dot(o_ref[...], w_out_ref[...])
    y_ref[0] = _layer_norm(ALPHA * x_ref[0] + out, lng_ref[...], lnb_ref[...])


def _attn_prompt(qt, kcat, kt, gate, x, w_uvt_h, w_out, ln_g, ln_b, *, rc):
    bsz, nq, _, rows = qt.shape
    tq = rows // B_HEADS
    t = kcat.shape[1]
    assert kt.shape[-1] == tq and tq & (tq - 1) == 0 and tq % CHUNK == 0 and rc % tq == 0
    const2 = lambda b, i: (0, 0)
    kern = functools.partial(_attn_prompt_kernel, tq=tq, rc=rc)
    return pl.pallas_call(
        kern,
        grid=(bsz, nq),
        in_specs=[
            pl.BlockSpec((1, 1, KCAT, rows), lambda b, i: (b, i, 0, 0)),
            pl.BlockSpec((1, t, KCAT), lambda b, i: (b, 0, 0)),
            pl.BlockSpec((1, nq, VEXT, tq), lambda b, i: (b, 0, 0, 0)),
            pl.BlockSpec((1, tq, B_WIDTH), lambda b, i: (b, i, 0)),
            pl.BlockSpec((1, tq, D_MODEL), lambda b, i: (b, i, 0)),
            pl.BlockSpec(w_uvt_h.shape, lambda b, i: (0, 0, 0)),
            pl.BlockSpec(w_out.shape, const2),
            pl.BlockSpec((1, D_MODEL), const2),
            pl.BlockSpec((1, D_MODEL), const2),
        ],
        out_specs=pl.BlockSpec((1, tq, D_MODEL), lambda b, i: (b, i, 0)),
        out_shape=jax.ShapeDtypeStruct((bsz, t, D_MODEL), F32),
        scratch_shapes=[
            pltpu.VMEM((1, rows), F32),
            pltpu.VMEM((VEXT, rows), F32),
            pltpu.VMEM((tq, B_WIDTH), BF16),
        ],
        compiler_params=pltpu.CompilerParams(
            dimension_semantics=("arbitrary", "arbitrary"), vmem_limit_bytes=VMEM_LIMIT),
        name="mla_attend_prompt",
    )(qt, kcat, kt, gate, x, w_uvt_h, w_out, ln_g, ln_b)


def _attn_sample_kernel(q_ref, cc_ref, ckr_ref, kn_ref, gate_ref, x_ref, w_uv_ref, w_out_ref,
                        lng_ref, lnb_ref, y_ref, m_ref, l_ref, acc_ref, o_ref, *, tq, tk):
    rows = B_HEADS * tq
    past = cc_ref.shape[1]
    q = q_ref[0].reshape(rows, KCAT)
    q_lat = q[:, :KV_LORA]
    q3 = q[:, KV_LORA:].astype(F32)
    q_rope = (q3[:, :ROPE_DIM] + q3[:, ROPE_DIM:]).astype(BF16)
    m_ref[...] = jnp.full((rows, 1), NEG_INF, F32)
    l_ref[...] = jnp.zeros((rows, 1), F32)
    acc_ref[...] = jnp.zeros((rows, KV_LORA), F32)

    for j in range(past // tk):
        cb = cc_ref[0, j * tk:(j + 1) * tk, :].astype(BF16)
        krb = ckr_ref[0, j * tk:(j + 1) * tk, :].astype(BF16)
        s = (lax.dot_general(q_lat, cb, _NT, preferred_element_type=F32)
             + lax.dot_general(q_rope, krb, _NT, preferred_element_type=F32))
        _softmax_step(s, cb, m_ref, l_ref, acc_ref)

    kn = kn_ref[0]
    s = lax.dot_general(q, kn, _NT, preferred_element_type=F32)
    _softmax_step(s, kn[:, :KV_LORA], m_ref, l_ref, acc_ref)

    y_ref[0] = _attn_output(acc_ref, l_ref, o_ref, gate_ref[0], x_ref[0],
                            w_uv_ref, w_out_ref, lng_ref, lnb_ref, tq)


def _attn_sample(qcat, cache_c, cache_kr, kcat_new, gate, x, w_uv_h, w_out, ln_g, ln_b, *, tk):
    bsz, _, tq, _ = qcat.shape
    past = cache_c.shape[1]
    rows = B_HEADS * tq
    const2 = lambda b: (0, 0)
    per_b = lambda shape: pl.BlockSpec((1,) + shape, lambda b: (b,) + (0,) * len(shape))
    kern = functools.partial(_attn_sample_kernel, tq=tq, tk=tk)
    return pl.pallas_call(
        kern,
        grid=(bsz,),
        in_specs=[
            per_b((B_HEADS, tq, KCAT)),
            per_b((past, KV_LORA)),
            per_b((past, ROPE_DIM)),
            per_b((tq, KCAT)),
            per_b((tq, B_WIDTH)),
            per_b((tq, D_MODEL)),
            pl.BlockSpec(w_uv_h.shape, lambda b: (0, 0, 0)),
            pl.BlockSpec(w_out.shape, const2),
            pl.BlockSpec((1, D_MODEL), const2),
            pl.BlockSpec((1, D_MODEL), const2),
        ],
        out_specs=per_b((tq, D_MODEL)),
        out_shape=jax.ShapeDtypeStruct((bsz, tq, D_MODEL), F32),
        scratch_shapes=[
            pltpu.VMEM((rows, 1), F32),
            pltpu.VMEM((rows, 1), F32),
            pltpu.VMEM((rows, KV_LORA), F32),
            pltpu.VMEM((tq, B_WIDTH), BF16),
        ],
        compiler_params=pltpu.CompilerParams(
            dimension_semantics=("arbitrary",), vmem_limit_bytes=VMEM_LIMIT),
        name="mla_attend_sample",
    )(qcat, cache_c, cache_kr, kcat_new, gate, x, w_uv_h, w_out, ln_g, ln_b)


def _rot_cols(w):
    half = ROPE_DIM // 2
    return jnp.concatenate([-w[..., half:], w[..., :half]], axis=-1)


def _rope_tables(pos):
    half = ROPE_DIM // 2
    inv = jnp.power(ROPE_BASE, -jnp.arange(half, dtype=F32) / half)
    ang = pos.astype(F32)[:, None] * inv[None, :]
    cos, sin = jnp.cos(ang), jnp.sin(ang)
    return jnp.concatenate([cos] * 4, axis=1), jnp.concatenate([sin] * 4, axis=1)


def kernel(x_prompt, x_sample, state_hgrn, cache_ckv, cache_krope, w_in_a, lb_gamma, a_norm_g, w_out_a,
           w_dkv, kv_norm_g, w_uk, w_uv, w_in_b, q_norm_g, w_uq, w_out_b, ln_g, ln_b):
    assert N_A == 1 and DEPTH == 2
    t_p = x_prompt.shape[1]
    t_s = x_sample.shape[1]
    past = cache_ckv.shape[1]
    row = lambda a: a.reshape(1, -1)

    w_in_a_b = w_in_a[0].astype(BF16)
    w_out_a_b = w_out_a[0].astype(BF16)
    kr_w = w_dkv[:, KV_LORA:]
    w_dkv_ext = jnp.concatenate(
        [w_dkv[:, :KV_LORA], kr_w, kr_w, _rot_cols(kr_w), _rot_cols(kr_w)], axis=1).astype(BF16)
    w_uq3 = w_uq[0].reshape(Q_LORA, B_HEADS, NOPE_DIM + ROPE_DIM)
    q_rope_w = w_uq3[:, :, NOPE_DIM:]
    w_uq_ext = jnp.concatenate([
        w_uq3[:, :, :NOPE_DIM].reshape(Q_LORA, B_HEADS * NOPE_DIM),
        q_rope_w.reshape(Q_LORA, B_HEADS * ROPE_DIM),
        _rot_cols(q_rope_w).reshape(Q_LORA, B_HEADS * ROPE_DIM)], axis=1).astype(BF16)
    w_uqt = jnp.concatenate([
        w_uq3[:, :, :NOPE_DIM].reshape(Q_LORA, B_HEADS * NOPE_DIM),
        q_rope_w.reshape(Q_LORA, B_HEADS * ROPE_DIM)], axis=1).T.astype(BF16)
    w_uk_h = jnp.transpose(w_uk, (1, 0, 2)).astype(BF16)
    w_uvt_h = jnp.transpose(w_uv, (1, 2, 0)).astype(BF16)
    w_in_b_b = w_in_b[0].astype(BF16)
    w_out_b_b = w_out_b[0].astype(BF16)

    outs = []
    for x, s0, pos0, prompt in ((x_prompt, None, 0, True), (x_sample, state_hgrn[0], past, False)):
        bsz, t, _ = x.shape
        if prompt:
            bblk, tt, chunk, sub, tm = 1, 1024, CHUNK, 256, 256
        else:
            bblk, tt, chunk, sub, tm = bsz, t, t, bsz * t, t
        x1, s_fin = _hgrn_layer(x, s0, w_in_a_b, lb_gamma, row(a_norm_g[0]), w_out_a_b,
                                row(ln_g[0]), row(ln_b[0]), layer=0, bblk=bblk, tt=tt, chunk=chunk, sub=sub)
        cos2, sin2 = _rope_tables(pos0 + jnp.arange(t, dtype=jnp.int32))
        tabs = (cos2, sin2, cos2.T, sin2.T) if prompt else (cos2, sin2)
        proj_outs = _mla_proj(
            x1, tabs, w_dkv_ext, row(kv_norm_g), w_in_b_b, row(q_norm_g[0]),
            w_uqt if prompt else w_uq_ext, w_uk_h,
            bblk=bblk, tt=tm, transposed=prompt, tag="prompt" if prompt else "sample")
        if prompt:
            ckv, krope, kcat, kt, qt, gate = proj_outs
            y = _attn_prompt(qt, kcat, kt, gate, x1, w_uvt_h, w_out_b_b, row(ln_g[1]), row(ln_b[1]), rc=512)
        else:
            ckv, krope, kcat, qcat, gate = proj_outs
            y = _attn_sample(qcat, cache_ckv, cache_krope, kcat, gate, x1, w_uvt_h, w_out_b_b,
                             row(ln_g[1]), row(ln_b[1]), tk=512)
        outs.append((y, s_fin[None], ckv, krope))
    (y_p, s_p, c_p, kr_p), (y_s, s_s, c_s, kr_s) = outs
    return (y_p, y_s, s_p, c_p, kr_p, s_s, c_s, kr_s)
```

```python
import functools

import jax
import jax.numpy as jnp
from jax import lax
from jax.experimental import pallas as pl
from jax.experimental.pallas import tpu as pltpu

F32 = jnp.float32
BF16 = jnp.bfloat16

D_MODEL = 1024
DEPTH = 2
CHUNK = 64
CHUNK_SHIFT = CHUNK.bit_length() - 1
N_A = DEPTH // 2
A_HEADS = 8
A_DK = 128
A_DV = 128
A_WIDTH = A_HEADS * A_DV
B_HEADS = 16
Q_LORA = 512
KV_LORA = 256
NOPE_DIM = 128
ROPE_DIM = 64
B_DV = 128
B_WIDTH = B_HEADS * B_DV
ROPE_BASE = 10000.0
EPS = 1e-6
NEG_INF = -1e30
ATTN_SCALE = (NOPE_DIM + ROPE_DIM) ** -0.5
LOG2_E = 1.4426950408889634
Q_SCALE = ATTN_SCALE * LOG2_E
ALPHA = (2 * DEPTH) ** 0.25

LANES = 128
BF16_SUBLANES = 16
KCAT = KV_LORA + 2 * ROPE_DIM
VEXT = KV_LORA + BF16_SUBLANES
VMEM_LIMIT = 56 * 1024 * 1024

HGRN_STEP_ROWS = 1024
HGRN_SUB_ROWS = 256
TOKEN_TILE = 256
ATTN_ROW_CHUNK = 512
SAMPLE_KEY_CHUNK = 1024

_NT = (((1,), (1,)), ((), ()))
_TN = (((0,), (0,)), ((), ()))


def _dot(a, b):
    return jnp.dot(a, b, preferred_element_type=F32)


def _layer_norm(x, g, b):
    mu = jnp.mean(x, axis=-1, keepdims=True)
    xc = x - mu
    var = jnp.mean(xc * xc, axis=-1, keepdims=True)
    return xc * lax.rsqrt(var + EPS) * g + b


def _rms_norm(x, g):
    return x * lax.rsqrt(jnp.mean(x * x, axis=-1, keepdims=True) + EPS) * g


def _sigmoid_pair(z):
    e = jnp.exp(-jnp.abs(z))
    r = 1.0 / (1.0 + e)
    er = e * r
    pos = z >= 0
    return jnp.where(pos, r, er), jnp.where(pos, er, r)


CUMSUM_TERMS = 3


def _split_bf16(x, n):
    terms = []
    for _ in range(n):
        t = x.astype(BF16)
        terms.append(t)
        x = x - t.astype(F32)
    return terms


def _hgrn_kernel(x_ref, s0_ref, w_in_ref, lbg_ref, ng_ref, w_out_ref, lng_ref, lnb_ref, tri_ref,
                 y_ref, sfin_ref, st_ref, o_ref, *, layer, bblk, tt, chunk, sub, zero_init):
    t = pl.program_id(1)
    rows = bblk * tt
    mid = chunk // 2

    @pl.when(t == 0)
    def _():
        for bb in range(bblk):
            for h in range(A_HEADS):
                if zero_init:
                    st_ref[bb, h] = jnp.zeros((A_DV, A_DK), F32)
                else:
                    st_ref[bb, h] = s0_ref[bb, h].T

    lbg = lbg_ref[...]
    e = jnp.exp(lbg - jnp.max(lbg, axis=0, keepdims=True))
    lb = jnp.sum(e[0:layer + 1], axis=0, keepdims=True) / jnp.sum(e, axis=0, keepdims=True)
    one_m_lb = 1.0 - lb
    ng = ng_ref[...]
    li = lax.broadcasted_iota(jnp.int32, (chunk, chunk), 0)
    si = lax.broadcasted_iota(jnp.int32, (chunk, chunk), 1)
    causal = li >= si

    x = x_ref[...].reshape(rows, D_MODEL)
    n_sub = rows // sub
    n_chunks = sub // chunk
    n_parts = 4
    tri = tri_ref[...]
    heads = [slice(h * A_DK, (h + 1) * A_DK) for h in range(A_HEADS)]

    def project(s_idx, part):
        xs = x[s_idx * sub:(s_idx + 1) * sub].astype(BF16)
        return _dot(xs, w_in_ref[:, part * A_WIDTH:(part + 1) * A_WIDTH])

    parts = [project(0, p) for p in range(n_parts)]
    for s_idx in range(n_sub):
        more = s_idx + 1 < n_sub
        next_parts = []
        chunks = range(n_chunks)
        rows_of = [slice(c * chunk, (c + 1) * chunk) for c in chunks]

        k_all, b_all = [], []
        for c in chunks:
            z = parts[1][rows_of[c]]
            sig_pos = 1.0 / (1.0 + jnp.exp(-z))
            sig_neg = 1.0 / (1.0 + jnp.exp(z))
            logf = jnp.log(lb + one_m_lb * sig_pos)
            k_all.append(one_m_lb * sig_neg)
            terms = _split_bf16(logf, CUMSUM_TERMS)
            b_all.append(_dot(tri, jnp.concatenate(terms, axis=0)))
        if more:
            next_parts.append(project(s_idx + 1, 0))

        q_ex, decay, v_all, sc_all, inc_all = [], [], [], [], []
        for c in chunks:
            bc = b_all[c]
            bmid = bc[mid:mid + 1]
            blast = bc[chunk - 1:chunk]
            qc = parts[0][rows_of[c]]
            q_mid = qc * jnp.exp(bc - bmid)
            k_mid = k_all[c] * jnp.exp(bmid - bc)
            q_in = q_mid.astype(BF16)
            k_in = k_mid.astype(BF16)
            k_dec = (k_mid * jnp.exp(blast - bmid)).astype(BF16)
            q_ex.append((q_mid * jnp.exp(bmid)).astype(BF16))
            decay.append(jnp.exp(blast))
            vc = parts[2][rows_of[c]].astype(BF16)
            v_all.append(vc)
            sc_all.append([lax.dot_general(q_in[:, hs], k_in[:, hs], _NT, preferred_element_type=F32)
                           for hs in heads])
            inc_all.append([lax.dot_general(vc[:, hs], k_dec[:, hs], _TN, preferred_element_type=F32)
                            for hs in heads])
        if more:
            next_parts.append(project(s_idx + 1, 1))

        st_in = []
        for c in chunks:
            bb = (s_idx * sub + c * chunk) // tt
            st_in.append([])
            for h, hs in enumerate(heads):
                st = st_ref[bb, h]
                st_in[c].append(st.T.astype(BF16))
                st_ref[bb, h] = st * decay[c][:, hs] + inc_all[c][h]
        if more:
            next_parts.append(project(s_idx + 1, 2))

        for c in chunks:
            r0 = s_idx * sub + c * chunk
            g = parts[3][rows_of[c]]
            silu_g = g / (1.0 + jnp.exp(-g))
            for h, hs in enumerate(heads):
                sc = jnp.where(causal, sc_all[c][h], 0.0).astype(BF16)
                o = _dot(jnp.concatenate([q_ex[c][:, hs], sc], axis=1),
                         jnp.concatenate([st_in[c][h], v_all[c][:, hs]], axis=0))
                o = _rms_norm(o, ng) * silu_g[:, hs]
                o_ref[r0:r0 + chunk, hs] = o.astype(BF16)
        if more:
            next_parts.append(project(s_idx + 1, 3))
        parts = next_parts

        ss = slice(s_idx * sub, (s_idx + 1) * sub)
        out = _dot(o_ref[ss, :], w_out_ref[...])
        y = _layer_norm(ALPHA * x[ss] + out, lng_ref[...], lnb_ref[...])
        if n_sub == 1:
            y_ref[...] = y.reshape(bblk, tt, D_MODEL)
        else:
            y_ref[0, ss, :] = y

    @pl.when(t == pl.num_programs(1) - 1)
    def _():
        for bb in range(bblk):
            for h in range(A_HEADS):
                sfin_ref[bb, h] = st_ref[bb, h].T


def _hgrn_layer(x, s0, w_in, lb_gamma, norm_g, w_out, ln_g, ln_b, *, layer, bblk, tt, chunk, sub):
    bsz, t, _ = x.shape
    rows = bblk * tt
    zero_init = s0 is None
    if zero_init:
        s0 = jnp.zeros((bblk, A_HEADS, 8, LANES), F32)
        s0_spec = pl.BlockSpec((bblk, A_HEADS, 8, LANES), lambda b, i: (0, 0, 0, 0))
    else:
        s0_spec = pl.BlockSpec((bblk, A_HEADS, A_DK, A_DV), lambda b, i: (b, 0, 0, 0))
    assert rows % sub == 0 and sub % chunk == 0 and tt % chunk == 0
    r = jnp.arange(chunk)
    tri = (r[:, None] >= r[None, :]).astype(BF16)
    tri3 = jnp.concatenate([tri] * CUMSUM_TERMS, axis=1)
    const = lambda b, i: (0, 0)
    kern = functools.partial(_hgrn_kernel, layer=layer, bblk=bblk, tt=tt, chunk=chunk, sub=sub,
                             zero_init=zero_init)
    return pl.pallas_call(
        kern,
        grid=(bsz // bblk, t // tt),
        in_specs=[
            pl.BlockSpec((bblk, tt, D_MODEL), lambda b, i: (b, i, 0)),
            s0_spec,
            pl.BlockSpec((D_MODEL, 4 * A_WIDTH), const),
            pl.BlockSpec(lb_gamma.shape, const),
            pl.BlockSpec((1, A_DV), const),
            pl.BlockSpec((A_WIDTH, D_MODEL), const),
            pl.BlockSpec((1, D_MODEL), const),
            pl.BlockSpec((1, D_MODEL), const),
            pl.BlockSpec((chunk, CUMSUM_TERMS * chunk), const),
        ],
        out_specs=[
            pl.BlockSpec((bblk, tt, D_MODEL), lambda b, i: (b, i, 0)),
            pl.BlockSpec((bblk, A_HEADS, A_DK, A_DV), lambda b, i: (b, 0, 0, 0)),
        ],
        out_shape=[
            jax.ShapeDtypeStruct((bsz, t, D_MODEL), F32),
            jax.ShapeDtypeStruct((bsz, A_HEADS, A_DK, A_DV), F32),
        ],
        scratch_shapes=[
            pltpu.VMEM((bblk, A_HEADS, A_DV, A_DK), F32),
            pltpu.VMEM((rows, A_WIDTH), BF16),
        ],
        compiler_params=pltpu.CompilerParams(
            dimension_semantics=("arbitrary", "arbitrary"), vmem_limit_bytes=VMEM_LIMIT),
        name=f"hgrn_layer_{'prompt' if zero_init else 'sample'}",
    )(x, s0, w_in, lb_gamma, norm_g, w_out, ln_g, ln_b, tri3)


def _latent_kv(xb, cos2, sin2, w_dkv_ref, kvg_ref, ckv_ref, krope_ref, kcat_ref, bblk, tt):
    kv = _dot(xb, w_dkv_ref[...])
    c = _rms_norm(kv[:, :KV_LORA], kvg_ref[...])
    krd = kv[:, KV_LORA:KV_LORA + LANES] * cos2 + kv[:, KV_LORA + LANES:KV_LORA + 2 * LANES] * sin2
    ckv_ref[...] = c.reshape(bblk, tt, KV_LORA)
    krope_ref[...] = krd[:, :ROPE_DIM].reshape(bblk, tt, ROPE_DIM)
    kcat = jnp.concatenate([c, krd], axis=1)
    kcat_ref[...] = kcat.astype(BF16).reshape(bblk, tt, KCAT)
    return kcat


def _query_latent(xb, w_in_ref, qg_ref, gate_ref, bblk, tt):
    proj = _dot(xb, w_in_ref[...])
    cq = _rms_norm(proj[:, :Q_LORA], qg_ref[...])
    gt = proj[:, Q_LORA:]
    gate_ref[...] = (gt * _sigmoid_pair(gt)[0]).reshape(bblk, tt, B_WIDTH)
    return cq


def _mla_proj_rows_kernel(x_ref, cos_ref, sin_ref, w_dkv_ref, kvg_ref, w_in_ref, qg_ref, w_uq_ref, w_uk_ref,
                          ckv_ref, krope_ref, kcat_ref, q_ref, gate_ref, *, bblk, tt):
    rows = bblk * tt
    xb = x_ref[...].reshape(rows, D_MODEL).astype(BF16)
    cos2 = jnp.concatenate([cos_ref[...]] * bblk, axis=0)
    sin2 = jnp.concatenate([sin_ref[...]] * bblk, axis=0)
    _latent_kv(xb, cos2, sin2, w_dkv_ref, kvg_ref, ckv_ref, krope_ref, kcat_ref, bblk, tt)
    cq = _query_latent(xb, w_in_ref, qg_ref, gate_ref, bblk, tt)
    qall = _dot(cq.astype(BF16), w_uq_ref[...])
    nope_w = B_HEADS * NOPE_DIM
    pair_w = (B_HEADS // 2) * LANES
    lane = lax.broadcasted_iota(jnp.int32, (rows, LANES), 1)
    for j in range(B_HEADS // 2):
        raw = qall[:, nope_w + j * LANES:nope_w + (j + 1) * LANES]
        rot = qall[:, nope_w + pair_w + j * LANES:nope_w + pair_w + (j + 1) * LANES]
        rp = (raw * cos2 + rot * sin2) * Q_SCALE
        for e in range(2):
            h = 2 * j + e
            ql = lax.dot_general(qall[:, h * NOPE_DIM:(h + 1) * NOPE_DIM].astype(BF16), w_uk_ref[h], _NT,
                                 preferred_element_type=F32) * Q_SCALE
            keep = (lane < ROPE_DIM) if e == 0 else (lane >= ROPE_DIM)
            qc = jnp.concatenate([ql, jnp.where(keep, rp, 0.0)], axis=1).astype(BF16)
            for bb in range(bblk):
                q_ref[bb, h] = qc[bb * tt:(bb + 1) * tt]


def _mla_proj_cols_kernel(x_ref, cos_ref, sin_ref, cost_ref, sint_ref, w_dkv_ref, kvg_ref, w_in_ref, qg_ref,
                          w_uqt_ref, w_uk_ref, ckv_ref, krope_ref, kcat_ref, kt_ref, q_ref, gate_ref, *, tt):
    xb = x_ref[0].astype(BF16)
    kcat = _latent_kv(xb, cos_ref[...], sin_ref[...], w_dkv_ref, kvg_ref, ckv_ref, krope_ref, kcat_ref, 1, tt)
    c_t = kcat[:, :KV_LORA].T
    kt_ref[0, 0] = jnp.concatenate([c_t, jnp.ones((VEXT - KV_LORA, tt), F32)], axis=0).astype(BF16)
    cq = _query_latent(xb, w_in_ref, qg_ref, gate_ref, 1, tt)
    qall_t = _dot(w_uqt_ref[...], cq.T.astype(BF16))
    nope_w = B_HEADS * NOPE_DIM
    half = ROPE_DIM // 2
    cos_t = cost_ref[...]
    sin_t = sint_ref[...]
    zeros = jnp.zeros((ROPE_DIM, tt), F32)
    for j in range(B_HEADS // 2):
        raw = qall_t[nope_w + j * LANES:nope_w + (j + 1) * LANES]
        rot = jnp.concatenate([-raw[half:2 * half], raw[0:half], -raw[3 * half:4 * half], raw[2 * half:3 * half]],
                              axis=0)
        rp = (raw * cos_t + rot * sin_t) * Q_SCALE
        for e in range(2):
            h = 2 * j + e
            ql = _dot(w_uk_ref[h], qall_t[h * NOPE_DIM:(h + 1) * NOPE_DIM].astype(BF16)) * Q_SCALE
            rope_rows = [rp[:ROPE_DIM], zeros] if e == 0 else [zeros, rp[ROPE_DIM:]]
            q_ref[0, 0, :, h * tt:(h + 1) * tt] = jnp.concatenate([ql] + rope_rows, axis=0).astype(BF16)


def _mla_proj(x, rope_tabs, w_dkv_ext, kv_g, w_in, q_g, w_q, w_k, *, bblk, tt, transposed, tag):
    bsz, t, _ = x.shape
    const2 = lambda b, i: (0, 0)
    const3 = lambda b, i: (0, 0, 0)
    tok = lambda w: pl.BlockSpec((bblk, tt, w), lambda b, i: (b, i, 0))
    tab = pl.BlockSpec((tt, LANES), lambda b, i: (i, 0))
    in_specs = [tok(D_MODEL), tab, tab]
    out_specs = [tok(KV_LORA), tok(ROPE_DIM), tok(KCAT)]
    out_shape = [
        jax.ShapeDtypeStruct((bsz, t, KV_LORA), F32),
        jax.ShapeDtypeStruct((bsz, t, ROPE_DIM), F32),
        jax.ShapeDtypeStruct((bsz, t, KCAT), BF16),
    ]
    if transposed:
        assert bblk == 1
        tab_t = pl.BlockSpec((LANES, tt), lambda b, i: (0, i))
        in_specs += [tab_t, tab_t]
        out_specs += [pl.BlockSpec((1, 1, VEXT, tt), lambda b, i: (b, i, 0, 0)),
                      pl.BlockSpec((1, 1, KCAT, B_HEADS * tt), lambda b, i: (b, i, 0, 0))]
        out_shape += [jax.ShapeDtypeStruct((bsz, t // tt, VEXT, tt), BF16),
                      jax.ShapeDtypeStruct((bsz, t // tt, KCAT, B_HEADS * tt), BF16)]
        kern = functools.partial(_mla_proj_cols_kernel, tt=tt)
    else:
        out_specs += [pl.BlockSpec((bblk, B_HEADS, tt, KCAT), lambda b, i: (b, 0, i, 0))]
        out_shape += [jax.ShapeDtypeStruct((bsz, B_HEADS, t, KCAT), BF16)]
        kern = functools.partial(_mla_proj_rows_kernel, bblk=bblk, tt=tt)
    in_specs += [
        pl.BlockSpec(w_dkv_ext.shape, const2),
        pl.BlockSpec((1, KV_LORA), const2),
        pl.BlockSpec(w_in.shape, const2),
        pl.BlockSpec((1, Q_LORA), const2),
        pl.BlockSpec(w_q.shape, const2),
        pl.BlockSpec(w_k.shape, const3),
    ]
    out_specs += [tok(B_WIDTH)]
    out_shape += [jax.ShapeDtypeStruct((bsz, t, B_WIDTH), F32)]
    return pl.pallas_call(
        kern,
        grid=(bsz // bblk, t // tt),
        in_specs=in_specs,
        out_specs=out_specs,
        out_shape=out_shape,
        compiler_params=pltpu.CompilerParams(
            dimension_semantics=("arbitrary", "arbitrary"), vmem_limit_bytes=VMEM_LIMIT),
        name=f"mla_proj_{tag}",
    )(x, *rope_tabs, w_dkv_ext, kv_g, w_in, q_g, w_q, w_k)


def _softmax_step(s, v, m_ref, l_ref, acc_ref):
    m_prev = m_ref[...]
    m_new = jnp.maximum(m_prev, jnp.max(s, axis=1, keepdims=True))
    a = jnp.exp2(m_prev - m_new)
    p = jnp.exp2(s - m_new)
    l_ref[...] = a * l_ref[...] + jnp.sum(p, axis=1, keepdims=True)
    acc_ref[...] = a * acc_ref[...] + _dot(p.astype(BF16), v)
    m_ref[...] = m_new


def _attn_output(acc_ref, l_ref, o_ref, gate, x, w_uv_ref, w_out_ref, lng_ref, lnb_ref, tq):
    for h in range(B_HEADS):
        hs = slice(h * tq, (h + 1) * tq)
        o_lat = acc_ref[hs, :] / l_ref[hs, :]
        oh = lax.dot_general(o_lat.astype(BF16), w_uv_ref[h], _NT, preferred_element_type=F32)
        o_ref[:, h * B_DV:(h + 1) * B_DV] = (oh * gate[:, h * B_DV:(h + 1) * B_DV]).astype(BF16)
    out = _dot(o_ref[...], w_out_ref[...])
    return _layer_norm(ALPHA * x + out, lng_ref[...], lnb_ref[...])


def _attn_prompt_kernel(qt_ref, kcat_ref, kt_ref, gate_ref, x_ref, w_uvt_ref, w_out_ref, lng_ref, lnb_ref,
                        y_ref, m_ref, acc_ref, o_ref, *, tq, rc):
    i = pl.program_id(1)
    rows = B_HEADS * tq
    n_rc = rows // rc

    def kv_step(j, visible, nblk=1):
        init = visible is not None
        kb = kcat_ref[0, pl.ds(pl.multiple_of(j * tq, tq), nblk * tq), :]
        ct = jnp.concatenate([kt_ref[0, j + n] for n in range(nblk)], axis=1)

        def scores(r):
            return _dot(kb, qt_ref[0, 0, :, r * rc:(r + 1) * rc])

        def softmax(r, st):
            ls = slice(r * rc, (r + 1) * rc)
            if init:
                st = jnp.where(visible, st, NEG_INF)
                m_new = jnp.max(st, axis=0, keepdims=True)
                a = None
            else:
                m_prev = m_ref[:, ls]
                m_new = jnp.maximum(m_prev, jnp.max(st, axis=0, keepdims=True))
                a = jnp.exp2(m_prev - m_new)
            pt = jnp.exp2(st - m_new)
            m_ref[:, ls] = m_new
            return a, pt.astype(BF16)

        st_cur = scores(0)
        pend = None
        for s in range(n_rc + 1):
            st_next = scores(s + 1) if s + 1 < n_rc else None
            if pend is not None:
                a_prev, pt_prev = pend
                pv = _dot(ct, pt_prev)
            new_pend = softmax(s, st_cur) if s < n_rc else None
            if pend is not None:
                ls = slice((s - 1) * rc, s * rc)
                acc_ref[:, ls] = pv if init else a_prev * acc_ref[:, ls] + pv
            pend = new_pend
            st_cur = st_next

    key_chunk = lax.broadcasted_iota(jnp.int32, (tq, rc), 0) >> CHUNK_SHIFT
    q_chunk = (lax.broadcasted_iota(jnp.int32, (tq, rc), 1) & (tq - 1)) >> CHUNK_SHIFT
    kv_step(i, key_chunk <= q_chunk)

    def body(jj, carry):
        kv_step(4 * jj, None, 4)
        return carry

    n_quads = i // 4
    lax.fori_loop(0, n_quads, body, 0)

    @pl.when(i - 4 * n_quads >= 2)
    def _():
        kv_step(4 * n_quads, None, 2)

    @pl.when(i % 2 == 1)
    def _():
        kv_step(i - 1, None)

    gate = gate_ref[0]
    for h in range(B_HEADS):
        hs = slice(h * tq, (h + 1) * tq)
        o_lat = acc_ref[0:KV_LORA, hs] / acc_ref[KV_LORA:KV_LORA + 1, hs]
        oh = _dot(w_uvt_ref[h], o_lat.astype(BF16)).T
        o_ref[:, h * B_DV:(h + 1) * B_DV] = (oh * gate[:, h * B_DV:(h + 1) * B_DV]).astype(BF16)
    out = _dot(o_ref[...], w_out_ref[...])
    y_ref[0] = _layer_norm(ALPHA * x_ref[0] + out, lng_ref[...], lnb_ref[...])


def _attn_prompt(qt, kcat, kt, gate, x, w_uvt_h, w_out, ln_g, ln_b, *, rc):
    bsz, nq, _, rows = qt.shape
    tq = rows // B_HEADS
    t = kcat.shape[1]
    assert kt.shape[-1] == tq and tq & (tq - 1) == 0 and tq % CHUNK == 0 and rc % tq == 0
    const2 = lambda b, i: (0, 0)
    kern = functools.partial(_attn_prompt_kernel, tq=tq, rc=rc)
    return pl.pallas_call(
        kern,
        grid=(bsz, nq),
        in_specs=[
            pl.BlockSpec((1, 1, KCAT, rows), lambda b, i: (b, i, 0, 0)),
            pl.BlockSpec((1, t, KCAT), lambda b, i: (b, 0, 0)),
            pl.BlockSpec((1, nq, VEXT, tq), lambda b, i: (b, 0, 0, 0)),
            pl.BlockSpec((1, tq, B_WIDTH), lambda b, i: (b, i, 0)),
            pl.BlockSpec((1, tq, D_MODEL), lambda b, i: (b, i, 0)),
            pl.BlockSpec(w_uvt_h.shape, lambda b, i: (0, 0, 0)),
            pl.BlockSpec(w_out.shape, const2),
            pl.BlockSpec((1, D_MODEL), const2),
            pl.BlockSpec((1, D_MODEL), const2),
        ],
        out_specs=pl.BlockSpec((1, tq, D_MODEL), lambda b, i: (b, i, 0)),
        out_shape=jax.ShapeDtypeStruct((bsz, t, D_MODEL), F32),
        scratch_shapes=[
            pltpu.VMEM((1, rows), F32),
            pltpu.VMEM((VEXT, rows), F32),
            pltpu.VMEM((tq, B_WIDTH), BF16),
        ],
        compiler_params=pltpu.CompilerParams(
            dimension_semantics=("arbitrary", "arbitrary"), vmem_limit_bytes=VMEM_LIMIT),
        name="mla_attend_prompt",
    )(qt, kcat, kt, gate, x, w_uvt_h, w_out, ln_g, ln_b)


def _attn_sample_kernel(q_ref, cc_ref, ckr_ref, kn_ref, gate_ref, x_ref, w_uv_ref, w_out_ref,
                        lng_ref, lnb_ref, y_ref, m_ref, l_ref, acc_ref, o_ref, *, tq, tk):
    rows = B_HEADS * tq
    past = cc_ref.shape[1]
    q = q_ref[0].reshape(rows, KCAT)
    q_lat = q[:, :KV_LORA]
    q3 = q[:, KV_LORA:].astype(F32)
    q_rope = (q3[:, :ROPE_DIM] + q3[:, ROPE_DIM:]).astype(BF16)
    m_ref[...] = jnp.full((rows, 1), NEG_INF, F32)
    l_ref[...] = jnp.zeros((rows, 1), F32)
    acc_ref[...] = jnp.zeros((rows, KV_LORA), F32)

    for j in range(past // tk):
        cb = cc_ref[0, j * tk:(j + 1) * tk, :].astype(BF16)
        krb = ckr_ref[0, j * tk:(j + 1) * tk, :].astype(BF16)
        s = (lax.dot_general(q_lat, cb, _NT, preferred_element_type=F32)
             + lax.dot_general(q_rope, krb, _NT, preferred_element_type=F32))
        _softmax_step(s, cb, m_ref, l_ref, acc_ref)

    kn = kn_ref[0]
    s = lax.dot_general(q, kn, _NT, preferred_element_type=F32)
    _softmax_step(s, kn[:, :KV_LORA], m_ref, l_ref, acc_ref)

    y_ref[0] = _attn_output(acc_ref, l_ref, o_ref, gate_ref[0], x_ref[0],
                            w_uv_ref, w_out_ref, lng_ref, lnb_ref, tq)


def _attn_sample(qcat, cache_c, cache_kr, kcat_new, gate, x, w_uv_h, w_out, ln_g, ln_b, *, tk):
    bsz, _, tq, _ = qcat.shape
    past = cache_c.shape[1]
    rows = B_HEADS * tq
    const2 = lambda b: (0, 0)
    per_b = lambda shape: pl.BlockSpec((1,) + shape, lambda b: (b,) + (0,) * len(shape))
    kern = functools.partial(_attn_sample_kernel, tq=tq, tk=tk)
    return pl.pallas_call(
        kern,
        grid=(bsz,),
        in_specs=[
            per_b((B_HEADS, tq, KCAT)),
            per_b((past, KV_LORA)),
            per_b((past, ROPE_DIM)),
            per_b((tq, KCAT)),
            per_b((tq, B_WIDTH)),
            per_b((tq, D_MODEL)),
            pl.BlockSpec(w_uv_h.shape, lambda b: (0, 0, 0)),
            pl.BlockSpec(w_out.shape, const2),
            pl.BlockSpec((1, D_MODEL), const2),
            pl.BlockSpec((1, D_MODEL), const2),
        ],
        out_specs=per_b((tq, D_MODEL)),
        out_shape=jax.ShapeDtypeStruct((bsz, tq, D_MODEL), F32),
        scratch_shapes=[
            pltpu.VMEM((rows, 1), F32),
            pltpu.VMEM((rows, 1), F32),
            pltpu.VMEM((rows, KV_LORA), F32),
            pltpu.VMEM((tq, B_WIDTH), BF16),
        ],
        compiler_params=pltpu.CompilerParams(
            dimension_semantics=("arbitrary",), vmem_limit_bytes=VMEM_LIMIT),
        name="mla_attend_sample",
    )(qcat, cache_c, cache_kr, kcat_new, gate, x, w_uv_h, w_out, ln_g, ln_b)


def _rot_cols(w):
    half = ROPE_DIM // 2
    return jnp.concatenate([-w[..., half:], w[..., :half]], axis=-1)


def _rope_tables(pos):
    half = ROPE_DIM // 2
    inv = jnp.power(ROPE_BASE, -jnp.arange(half, dtype=F32) / half)
    ang = pos.astype(F32)[:, None] * inv[None, :]
    cos, sin = jnp.cos(ang), jnp.sin(ang)
    return jnp.concatenate([cos] * 4, axis=1), jnp.concatenate([sin] * 4, axis=1)


def kernel(x_prompt, x_sample, state_hgrn, cache_ckv, cache_krope, w_in_a, lb_gamma, a_norm_g, w_out_a,
           w_dkv, kv_norm_g, w_uk, w_uv, w_in_b, q_norm_g, w_uq, w_out_b, ln_g, ln_b):
    assert N_A == 1 and DEPTH == 2
    past = cache_ckv.shape[1]
    row = lambda a: a.reshape(1, -1)

    w_in_a_b = w_in_a[0].astype(BF16)
    w_out_a_b = w_out_a[0].astype(BF16)
    kr_w = w_dkv[:, KV_LORA:]
    w_dkv_ext = jnp.concatenate(
        [w_dkv[:, :KV_LORA], kr_w, kr_w, _rot_cols(kr_w), _rot_cols(kr_w)], axis=1).astype(BF16)
    w_uq3 = w_uq[0].reshape(Q_LORA, B_HEADS, NOPE_DIM + ROPE_DIM)
    q_rope_w = w_uq3[:, :, NOPE_DIM:]
    w_uq_ext = jnp.concatenate([
        w_uq3[:, :, :NOPE_DIM].reshape(Q_LORA, B_HEADS * NOPE_DIM),
        q_rope_w.reshape(Q_LORA, B_HEADS * ROPE_DIM),
        _rot_cols(q_rope_w).reshape(Q_LORA, B_HEADS * ROPE_DIM)], axis=1).astype(BF16)
    w_uqt = jnp.concatenate([
        w_uq3[:, :, :NOPE_DIM].reshape(Q_LORA, B_HEADS * NOPE_DIM),
        q_rope_w.reshape(Q_LORA, B_HEADS * ROPE_DIM)], axis=1).T.astype(BF16)
    w_uk_h = jnp.transpose(w_uk, (1, 0, 2)).astype(BF16)
    w_uvt_h = jnp.transpose(w_uv, (1, 2, 0)).astype(BF16)
    w_in_b_b = w_in_b[0].astype(BF16)
    w_out_b_b = w_out_b[0].astype(BF16)

    outs = []
    for x, s0, pos0, prompt in ((x_prompt, None, 0, True), (x_sample, state_hgrn[0], past, False)):
        bsz, t, _ = x.shape
        if prompt:
            bblk, tt, chunk, sub, tm = 1, HGRN_STEP_ROWS, CHUNK, HGRN_SUB_ROWS, TOKEN_TILE
        else:
            bblk, tt, chunk, sub, tm = bsz, t, t, bsz * t, t
        x1, s_fin = _hgrn_layer(x, s0, w_in_a_b, lb_gamma, row(a_norm_g[0]), w_out_a_b,
                                row(ln_g[0]), row(ln_b[0]), layer=0, bblk=bblk, tt=tt, chunk=chunk, sub=sub)
        cos2, sin2 = _rope_tables(pos0 + jnp.arange(t, dtype=jnp.int32))
        tabs = (cos2, sin2, cos2.T, sin2.T) if prompt else (cos2, sin2)
        proj_outs = _mla_proj(
            x1, tabs, w_dkv_ext, row(kv_norm_g), w_in_b_b, row(q_norm_g[0]),
            w_uqt if prompt else w_uq_ext, w_uk_h,
            bblk=bblk, tt=tm, transposed=prompt, tag="prompt" if prompt else "sample")
        if prompt:
            ckv, krope, kcat, kt, qt, gate = proj_outs
            y = _attn_prompt(qt, kcat, kt, gate, x1, w_uvt_h, w_out_b_b, row(ln_g[1]), row(ln_b[1]),
                             rc=ATTN_ROW_CHUNK)
        else:
            ckv, krope, kcat, qcat, gate = proj_outs
            y = _attn_sample(qcat, cache_ckv, cache_krope, kcat, gate, x1, w_uvt_h, w_out_b_b,
                             row(ln_g[1]), row(ln_b[1]), tk=SAMPLE_KEY_CHUNK)
        outs.append((y, s_fin[None], ckv, krope))
    (y_p, s_p, c_p, kr_p), (y_s, s_s, c_s, kr_s) = outs
    return (y_p, y_s, s_p, c_p, kr_p, s_s, c_s, kr_s)
```

```python
import functools

import jax
import jax.numpy as jnp
from jax import lax
from jax.experimental import pallas as pl
from jax.experimental.pallas import tpu as pltpu

F32 = jnp.float32
BF16 = jnp.bfloat16

D_MODEL = 1024
DEPTH = 2
CHUNK = 64
CHUNK_SHIFT = CHUNK.bit_length() - 1
N_A = DEPTH // 2
A_HEADS = 8
A_DK = 128
A_DV = 128
A_WIDTH = A_HEADS * A_DV
B_HEADS = 16
Q_LORA = 512
KV_LORA = 256
NOPE_DIM = 128
ROPE_DIM = 64
B_DV = 128
B_WIDTH = B_HEADS * B_DV
ROPE_BASE = 10000.0
EPS = 1e-6
NEG_INF = -1e30
ATTN_SCALE = (NOPE_DIM + ROPE_DIM) ** -0.5
LOG2_E = 1.4426950408889634
Q_SCALE = ATTN_SCALE * LOG2_E
ALPHA = (2 * DEPTH) ** 0.25

LANES = 128
BF16_SUBLANES = 16
KCAT = KV_LORA + 2 * ROPE_DIM
VEXT = KV_LORA + BF16_SUBLANES
VMEM_LIMIT = 56 * 1024 * 1024

HGRN_STEP_ROWS = 1024
HGRN_SUB_ROWS = 256
TOKEN_TILE = 256
ATTN_ROW_CHUNK = 512
SAMPLE_KEY_CHUNK = 1024

_NT = (((1,), (1,)), ((), ()))
_TN = (((0,), (0,)), ((), ()))


def _dot(a, b):
    return jnp.dot(a, b, preferred_element_type=F32)


def _layer_norm(x, g, b):
    mu = jnp.mean(x, axis=-1, keepdims=True)
    xc = x - mu
    var = jnp.mean(xc * xc, axis=-1, keepdims=True)
    return xc * lax.rsqrt(var + EPS) * g + b


def _rms_norm(x, g):
    return x * lax.rsqrt(jnp.mean(x * x, axis=-1, keepdims=True) + EPS) * g


def _sigmoid_pair(z):
    e = jnp.exp(-jnp.abs(z))
    r = 1.0 / (1.0 + e)
    er = e * r
    pos = z >= 0
    return jnp.where(pos, r, er), jnp.where(pos, er, r)


CUMSUM_TERMS = 3


def _split_bf16(x, n):
    terms = []
    for _ in range(n):
        t = x.astype(BF16)
        terms.append(t)
        x = x - t.astype(F32)
    return terms


def _hgrn_kernel(x_ref, s0_ref, w_in_ref, lbg_ref, ng_ref, w_out_ref, lng_ref, lnb_ref, tri_ref,
                 y_ref, sfin_ref, st_ref, o_ref, *, layer, bblk, tt, chunk, sub, zero_init):
    t = pl.program_id(1)
    rows = bblk * tt
    mid = chunk // 2

    @pl.when(t == 0)
    def _():
        for bb in range(bblk):
            for h in range(A_HEADS):
                if zero_init:
                    st_ref[bb, h] = jnp.zeros((A_DV, A_DK), F32)
                else:
                    st_ref[bb, h] = s0_ref[bb, h].T

    lbg = lbg_ref[...]
    e = jnp.exp(lbg - jnp.max(lbg, axis=0, keepdims=True))
    lb = jnp.sum(e[0:layer + 1], axis=0, keepdims=True) / jnp.sum(e, axis=0, keepdims=True)
    one_m_lb = 1.0 - lb
    ng = ng_ref[...]
    grp = LANES // chunk
    qi = lax.broadcasted_iota(jnp.int32, (chunk, LANES), 0)
    kj = lax.broadcasted_iota(jnp.int32, (chunk, LANES), 1)
    own_causal = [(kj >= k * chunk) & (kj - k * chunk <= qi) for k in range(grp)]

    x = x_ref[...].reshape(rows, D_MODEL)
    n_sub = rows // sub
    n_chunks = sub // chunk
    n_parts = 4
    tri = tri_ref[...]
    heads = [slice(h * A_DK, (h + 1) * A_DK) for h in range(A_HEADS)]

    def project(s_idx, part):
        xs = x[s_idx * sub:(s_idx + 1) * sub].astype(BF16)
        return _dot(xs, w_in_ref[:, part * A_WIDTH:(part + 1) * A_WIDTH])

    parts = [project(0, p) for p in range(n_parts)]
    for s_idx in range(n_sub):
        more = s_idx + 1 < n_sub
        next_parts = []
        chunks = range(n_chunks)
        rows_of = [slice(c * chunk, (c + 1) * chunk) for c in chunks]

        k_all, b_all = [], []
        for c in chunks:
            z = parts[1][rows_of[c]]
            sig_pos = 1.0 / (1.0 + jnp.exp(-z))
            sig_neg = 1.0 / (1.0 + jnp.exp(z))
            logf = jnp.log(lb + one_m_lb * sig_pos)
            k_all.append(one_m_lb * sig_neg)
            terms = _split_bf16(logf, CUMSUM_TERMS)
            b_all.append(_dot(tri, jnp.concatenate(terms, axis=0)))
        if more:
            next_parts.append(project(s_idx + 1, 0))

        q_ex, decay, q_in_all, k_in_all, inc_all = [], [], [], [], []
        v_sub = parts[2].astype(BF16)
        for c in chunks:
            bc = b_all[c]
            bmid = bc[mid:mid + 1]
            blast = bc[chunk - 1:chunk]
            qc = parts[0][rows_of[c]]
            q_mid = qc * jnp.exp(bc - bmid)
            k_mid = k_all[c] * jnp.exp(bmid - bc)
            q_in_all.append(q_mid.astype(BF16))
            k_in_all.append(k_mid.astype(BF16))
            k_dec = (k_mid * jnp.exp(blast - bmid)).astype(BF16)
            q_ex.append((q_mid * jnp.exp(bmid)).astype(BF16))
            decay.append(jnp.exp(blast))
            vc = v_sub[rows_of[c]]
            inc_all.append([lax.dot_general(vc[:, hs], k_dec[:, hs], _TN, preferred_element_type=F32)
                            for hs in heads])
        q_in_sub = jnp.concatenate(q_in_all, axis=0)
        k_in_sub = jnp.concatenate(k_in_all, axis=0)
        sc_sub = [lax.dot_general(q_in_sub[:, hs], k_in_sub[:, hs], _NT, preferred_element_type=F32)
                  for hs in heads]
        if more:
            next_parts.append(project(s_idx + 1, 1))

        st_in = []
        for c in chunks:
            bb = (s_idx * sub + c * chunk) // tt
            st_in.append([])
            for h, hs in enumerate(heads):
                st = st_ref[bb, h]
                st_in[c].append(st.T.astype(BF16))
                st_ref[bb, h] = st * decay[c][:, hs] + inc_all[c][h]
        if more:
            next_parts.append(project(s_idx + 1, 2))

        for c in chunks:
            r0 = s_idx * sub + c * chunk
            g = parts[3][rows_of[c]]
            silu_g = g / (1.0 + jnp.exp(-g))
            g0 = (c * chunk) // LANES * LANES
            for h, hs in enumerate(heads):
                sc = jnp.where(own_causal[c % grp], sc_sub[h][rows_of[c], g0:g0 + LANES], 0.0).astype(BF16)
                o = _dot(jnp.concatenate([q_ex[c][:, hs], sc], axis=1),
                         jnp.concatenate([st_in[c][h], v_sub[g0:g0 + LANES, hs]], axis=0))
                o = _rms_norm(o, ng) * silu_g[:, hs]
                o_ref[r0:r0 + chunk, hs] = o.astype(BF16)
        if more:
            next_parts.append(project(s_idx + 1, 3))
        parts = next_parts

        ss = slice(s_idx * sub, (s_idx + 1) * sub)
        out = _dot(o_ref[ss, :], w_out_ref[...])
        y = _layer_norm(ALPHA * x[ss] + out, lng_ref[...], lnb_ref[...])
        if n_sub == 1:
            y_ref[...] = y.reshape(bblk, tt, D_MODEL)
        else:
            y_ref[0, ss, :] = y

    @pl.when(t == pl.num_programs(1) - 1)
    def _():
        for bb in range(bblk):
            for h in range(A_HEADS):
                sfin_ref[bb, h] = st_ref[bb, h].T


def _hgrn_layer(x, s0, w_in, lb_gamma, norm_g, w_out, ln_g, ln_b, *, layer, bblk, tt, chunk, sub):
    bsz, t, _ = x.shape
    rows = bblk * tt
    zero_init = s0 is None
    if zero_init:
        s0 = jnp.zeros((bblk, A_HEADS, 8, LANES), F32)
        s0_spec = pl.BlockSpec((bblk, A_HEADS, 8, LANES), lambda b, i: (0, 0, 0, 0))
    else:
        s0_spec = pl.BlockSpec((bblk, A_HEADS, A_DK, A_DV), lambda b, i: (b, 0, 0, 0))
    assert rows % sub == 0 and sub % LANES == 0 and LANES % chunk == 0 and tt % chunk == 0
    r = jnp.arange(chunk)
    tri = (r[:, None] >= r[None, :]).astype(BF16)
    tri3 = jnp.concatenate([tri] * CUMSUM_TERMS, axis=1)
    const = lambda b, i: (0, 0)
    kern = functools.partial(_hgrn_kernel, layer=layer, bblk=bblk, tt=tt, chunk=chunk, sub=sub,
                             zero_init=zero_init)
    return pl.pallas_call(
        kern,
        grid=(bsz // bblk, t // tt),
        in_specs=[
            pl.BlockSpec((bblk, tt, D_MODEL), lambda b, i: (b, i, 0)),
            s0_spec,
            pl.BlockSpec((D_MODEL, 4 * A_WIDTH), const),
            pl.BlockSpec(lb_gamma.shape, const),
            pl.BlockSpec((1, A_DV), const),
            pl.BlockSpec((A_WIDTH, D_MODEL), const),
            pl.BlockSpec((1, D_MODEL), const),
            pl.BlockSpec((1, D_MODEL), const),
            pl.BlockSpec((chunk, CUMSUM_TERMS * chunk), const),
        ],
        out_specs=[
            pl.BlockSpec((bblk, tt, D_MODEL), lambda b, i: (b, i, 0)),
            pl.BlockSpec((bblk, A_HEADS, A_DK, A_DV), lambda b, i: (b, 0, 0, 0)),
        ],
        out_shape=[
            jax.ShapeDtypeStruct((bsz, t, D_MODEL), F32),
            jax.ShapeDtypeStruct((bsz, A_HEADS, A_DK, A_DV), F32),
        ],
        scratch_shapes=[
            pltpu.VMEM((bblk, A_HEADS, A_DV, A_DK), F32),
            pltpu.VMEM((rows, A_WIDTH), BF16),
        ],
        compiler_params=pltpu.CompilerParams(
            dimension_semantics=("arbitrary", "arbitrary"), vmem_limit_bytes=VMEM_LIMIT),
        name=f"hgrn_layer_{'prompt' if zero_init else 'sample'}",
    )(x, s0, w_in, lb_gamma, norm_g, w_out, ln_g, ln_b, tri3)


def _latent_kv(xb, cos2, sin2, w_dkv_ref, kvg_ref, ckv_ref, krope_ref, kcat_ref, bblk, tt):
    kv = _dot(xb, w_dkv_ref[...])
    c = _rms_norm(kv[:, :KV_LORA], kvg_ref[...])
    krd = kv[:, KV_LORA:KV_LORA + LANES] * cos2 + kv[:, KV_LORA + LANES:KV_LORA + 2 * LANES] * sin2
    ckv_ref[...] = c.reshape(bblk, tt, KV_LORA)
    krope_ref[...] = krd[:, :ROPE_DIM].reshape(bblk, tt, ROPE_DIM)
    kcat = jnp.concatenate([c, krd], axis=1)
    kcat_ref[...] = kcat.astype(BF16).reshape(bblk, tt, KCAT)
    return kcat


def _query_latent(xb, w_in_ref, qg_ref, gate_ref, bblk, tt):
    proj = _dot(xb, w_in_ref[...])
    cq = _rms_norm(proj[:, :Q_LORA], qg_ref[...])
    gt = proj[:, Q_LORA:]
    gate_ref[...] = (gt * _sigmoid_pair(gt)[0]).reshape(bblk, tt, B_WIDTH)
    return cq


def _mla_proj_rows_kernel(x_ref, cos_ref, sin_ref, w_dkv_ref, kvg_ref, w_in_ref, qg_ref, w_uq_ref, w_uk_ref,
                          ckv_ref, krope_ref, kcat_ref, q_ref, gate_ref, *, bblk, tt):
    rows = bblk * tt
    xb = x_ref[...].reshape(rows, D_MODEL).astype(BF16)
    cos2 = jnp.concatenate([cos_ref[...]] * bblk, axis=0)
    sin2 = jnp.concatenate([sin_ref[...]] * bblk, axis=0)
    _latent_kv(xb, cos2, sin2, w_dkv_ref, kvg_ref, ckv_ref, krope_ref, kcat_ref, bblk, tt)
    cq = _query_latent(xb, w_in_ref, qg_ref, gate_ref, bblk, tt)
    qall = _dot(cq.astype(BF16), w_uq_ref[...])
    nope_w = B_HEADS * NOPE_DIM
    pair_w = (B_HEADS // 2) * LANES
    lane = lax.broadcasted_iota(jnp.int32, (rows, LANES), 1)
    for j in range(B_HEADS // 2):
        raw = qall[:, nope_w + j * LANES:nope_w + (j + 1) * LANES]
        rot = qall[:, nope_w + pair_w + j * LANES:nope_w + pair_w + (j + 1) * LANES]
        rp = (raw * cos2 + rot * sin2) * Q_SCALE
        for e in range(2):
            h = 2 * j + e
            ql = lax.dot_general(qall[:, h * NOPE_DIM:(h + 1) * NOPE_DIM].astype(BF16), w_uk_ref[h], _NT,
                                 preferred_element_type=F32) * Q_SCALE
            keep = (lane < ROPE_DIM) if e == 0 else (lane >= ROPE_DIM)
            qc = jnp.concatenate([ql, jnp.where(keep, rp, 0.0)], axis=1).astype(BF16)
            for bb in range(bblk):
                q_ref[bb, h] = qc[bb * tt:(bb + 1) * tt]


def _mla_proj_cols_kernel(x_ref, cos_ref, sin_ref, cost_ref, sint_ref, w_dkv_ref, kvg_ref, w_in_ref, qg_ref,
                          w_uqt_ref, w_uk_ref, ckv_ref, krope_ref, kcat_ref, kt_ref, q_ref, gate_ref, *, tt):
    xb = x_ref[0].astype(BF16)
    kcat = _latent_kv(xb, cos_ref[...], sin_ref[...], w_dkv_ref, kvg_ref, ckv_ref, krope_ref, kcat_ref, 1, tt)
    c_t = kcat[:, :KV_LORA].T
    kt_ref[0, 0] = jnp.concatenate([c_t, jnp.ones((VEXT - KV_LORA, tt), F32)], axis=0).astype(BF16)
    cq = _query_latent(xb, w_in_ref, qg_ref, gate_ref, 1, tt)
    qall_t = _dot(w_uqt_ref[...], cq.T.astype(BF16))
    nope_w = B_HEADS * NOPE_DIM
    half = ROPE_DIM // 2
    cos_t = cost_ref[...]
    sin_t = sint_ref[...]
    zeros = jnp.zeros((ROPE_DIM, tt), F32)
    for j in range(B_HEADS // 2):
        raw = qall_t[nope_w + j * LANES:nope_w + (j + 1) * LANES]
        rot = jnp.concatenate([-raw[half:2 * half], raw[0:half], -raw[3 * half:4 * half], raw[2 * half:3 * half]],
                              axis=0)
        rp = (raw * cos_t + rot * sin_t) * Q_SCALE
        for e in range(2):
            h = 2 * j + e
            ql = _dot(w_uk_ref[h], qall_t[h * NOPE_DIM:(h + 1) * NOPE_DIM].astype(BF16)) * Q_SCALE
            rope_rows = [rp[:ROPE_DIM], zeros] if e == 0 else [zeros, rp[ROPE_DIM:]]
            q_ref[0, 0, :, h * tt:(h + 1) * tt] = jnp.concatenate([ql] + rope_rows, axis=0).astype(BF16)


def _mla_proj(x, rope_tabs, w_dkv_ext, kv_g, w_in, q_g, w_q, w_k, *, bblk, tt, transposed, tag):
    bsz, t, _ = x.shape
    const2 = lambda b, i: (0, 0)
    const3 = lambda b, i: (0, 0, 0)
    tok = lambda w: pl.BlockSpec((bblk, tt, w), lambda b, i: (b, i, 0))
    tab = pl.BlockSpec((tt, LANES), lambda b, i: (i, 0))
    in_specs = [tok(D_MODEL), tab, tab]
    out_specs = [tok(KV_LORA), tok(ROPE_DIM), tok(KCAT)]
    out_shape = [
        jax.ShapeDtypeStruct((bsz, t, KV_LORA), F32),
        jax.ShapeDtypeStruct((bsz, t, ROPE_DIM), F32),
        jax.ShapeDtypeStruct((bsz, t, KCAT), BF16),
    ]
    if transposed:
        assert bblk == 1
        tab_t = pl.BlockSpec((LANES, tt), lambda b, i: (0, i))
        in_specs += [tab_t, tab_t]
        out_specs += [pl.BlockSpec((1, 1, VEXT, tt), lambda b, i: (b, i, 0, 0)),
                      pl.BlockSpec((1, 1, KCAT, B_HEADS * tt), lambda b, i: (b, i, 0, 0))]
        out_shape += [jax.ShapeDtypeStruct((bsz, t // tt, VEXT, tt), BF16),
                      jax.ShapeDtypeStruct((bsz, t // tt, KCAT, B_HEADS * tt), BF16)]
        kern = functools.partial(_mla_proj_cols_kernel, tt=tt)
    else:
        out_specs += [pl.BlockSpec((bblk, B_HEADS, tt, KCAT), lambda b, i: (b, 0, i, 0))]
        out_shape += [jax.ShapeDtypeStruct((bsz, B_HEADS, t, KCAT), BF16)]
        kern = functools.partial(_mla_proj_rows_kernel, bblk=bblk, tt=tt)
    in_specs += [
        pl.BlockSpec(w_dkv_ext.shape, const2),
        pl.BlockSpec((1, KV_LORA), const2),
        pl.BlockSpec(w_in.shape, const2),
        pl.BlockSpec((1, Q_LORA), const2),
        pl.BlockSpec(w_q.shape, const2),
        pl.BlockSpec(w_k.shape, const3),
    ]
    out_specs += [tok(B_WIDTH)]
    out_shape += [jax.ShapeDtypeStruct((bsz, t, B_WIDTH), F32)]
    return pl.pallas_call(
        kern,
        grid=(bsz // bblk, t // tt),
        in_specs=in_specs,
        out_specs=out_specs,
        out_shape=out_shape,
        compiler_params=pltpu.CompilerParams(
            dimension_semantics=("arbitrary", "arbitrary"), vmem_limit_bytes=VMEM_LIMIT),
        name=f"mla_proj_{tag}",
    )(x, *rope_tabs, w_dkv_ext, kv_g, w_in, q_g, w_q, w_k)


def _softmax_step(s, v, m_ref, l_ref, acc_ref):
    m_prev = m_ref[...]
    m_new = jnp.maximum(m_prev, jnp.max(s, axis=1, keepdims=True))
    a = jnp.exp2(m_prev - m_new)
    p = jnp.exp2(s - m_new)
    l_ref[...] = a * l_ref[...] + jnp.sum(p, axis=1, keepdims=True)
    acc_ref[...] = a * acc_ref[...] + _dot(p.astype(BF16), v)
    m_ref[...] = m_new


def _attn_output(acc_ref, l_ref, o_ref, gate, x, w_uv_ref, w_out_ref, lng_ref, lnb_ref, tq):
    for h in range(B_HEADS):
        hs = slice(h * tq, (h + 1) * tq)
        o_lat = acc_ref[hs, :] / l_ref[hs, :]
        oh = lax.dot_general(o_lat.astype(BF16), w_uv_ref[h], _NT, preferred_element_type=F32)
        o_ref[:, h * B_DV:(h + 1) * B_DV] = (oh * gate[:, h * B_DV:(h + 1) * B_DV]).astype(BF16)
    out = _dot(o_ref[...], w_out_ref[...])
    return _layer_norm(ALPHA * x + out, lng_ref[...], lnb_ref[...])


def _attn_prompt_kernel(qt_ref, kcat_ref, kt_ref, gate_ref, x_ref, w_uvt_ref, w_out_ref, lng_ref, lnb_ref,
                        y_ref, m_ref, acc_ref, o_ref, *, tq, rc):
    i = pl.program_id(1)
    rows = B_HEADS * tq
    n_rc = rows // rc

    def kv_step(j, visible, nblk=1):
        init = visible is not None
        kb = kcat_ref[0, pl.ds(pl.multiple_of(j * tq, tq), nblk * tq), :]
        ct = jnp.concatenate([kt_ref[0, j + n] for n in range(nblk)], axis=1)

        def scores(r):
            return _dot(kb, qt_ref[0, 0, :, r * rc:(r + 1) * rc])

        def softmax(r, st):
            ls = slice(r * rc, (r + 1) * rc)
            if init:
                st = jnp.where(visible, st, NEG_INF)
                m_new = jnp.max(st, axis=0, keepdims=True)
                a = None
            else:
                m_prev = m_ref[:, ls]
                m_new = jnp.maximum(m_prev, jnp.max(st, axis=0, keepdims=True))
                a = jnp.exp2(m_prev - m_new)
            pt = jnp.exp2(st - m_new)
            m_ref[:, ls] = m_new
            return a, pt.astype(BF16)

        st_cur = scores(0)
        pend = None
        for s in range(n_rc + 1):
            st_next = scores(s + 1) if s + 1 < n_rc else None
            if pend is not None:
                a_prev, pt_prev = pend
                pv = _dot(ct, pt_prev)
            new_pend = softmax(s, st_cur) if s < n_rc else None
            if pend is not None:
                ls = slice((s - 1) * rc, s * rc)
                acc_ref[:, ls] = pv if init else a_prev * acc_ref[:, ls] + pv
            pend = new_pend
            st_cur = st_next

    key_chunk = lax.broadcasted_iota(jnp.int32, (tq, rc), 0) >> CHUNK_SHIFT
    q_chunk = (lax.broadcasted_iota(jnp.int32, (tq, rc), 1) & (tq - 1)) >> CHUNK_SHIFT
    kv_step(i, key_chunk <= q_chunk)

    def body(jj, carry):
        kv_step(4 * jj, None, 4)
        return carry

    n_quads = i // 4
    lax.fori_loop(0, n_quads, body, 0)

    @pl.when(i - 4 * n_quads >= 2)
    def _():
        kv_step(4 * n_quads, None, 2)

    @pl.when(i % 2 == 1)
    def _():
        kv_step(i - 1, None)

    gate = gate_ref[0]
    for h in range(B_HEADS):
        hs = slice(h * tq, (h + 1) * tq)
        o_lat = acc_ref[0:KV_LORA, hs] / acc_ref[KV_LORA:KV_LORA + 1, hs]
        oh = _dot(w_uvt_ref[h], o_lat.astype(BF16)).T
        o_ref[:, h * B_DV:(h + 1) * B_DV] = (oh * gate[:, h * B_DV:(h + 1) * B_DV]).astype(BF16)
    out = _dot(o_ref[...], w_out_ref[...])
    y_ref[0] = _layer_norm(ALPHA * x_ref[0] + out, lng_ref[...], lnb_ref[...])


def _attn_prompt(qt, kcat, kt, gate, x, w_uvt_h, w_out, ln_g, ln_b, *, rc):
    bsz, nq, _, rows = qt.shape
    tq = rows // B_HEADS
    t = kcat.shape[1]
    assert kt.shape[-1] == tq and tq & (tq - 1) == 0 and tq % CHUNK == 0 and rc % tq == 0
    const2 = lambda b, i: (0, 0)
    kern = functools.partial(_attn_prompt_kernel, tq=tq, rc=rc)
    return pl.pallas_call(
        kern,
        grid=(bsz, nq),
        in_specs=[
            pl.BlockSpec((1, 1, KCAT, rows), lambda b, i: (b, i, 0, 0)),
            pl.BlockSpec((1, t, KCAT), lambda b, i: (b, 0, 0)),
            pl.BlockSpec((1, nq, VEXT, tq), lambda b, i: (b, 0, 0, 0)),
            pl.BlockSpec((1, tq, B_WIDTH), lambda b, i: (b, i, 0)),
            pl.BlockSpec((1, tq, D_MODEL), lambda b, i: (b, i, 0)),
            pl.BlockSpec(w_uvt_h.shape, lambda b, i: (0, 0, 0)),
            pl.BlockSpec(w_out.shape, const2),
            pl.BlockSpec((1, D_MODEL), const2),
            pl.BlockSpec((1, D_MODEL), const2),
        ],
        out_specs=pl.BlockSpec((1, tq, D_MODEL), lambda b, i: (b, i, 0)),
        out_shape=jax.ShapeDtypeStruct((bsz, t, D_MODEL), F32),
        scratch_shapes=[
            pltpu.VMEM((1, rows), F32),
            pltpu.VMEM((VEXT, rows), F32),
            pltpu.VMEM((tq, B_WIDTH), BF16),
        ],
        compiler_params=pltpu.CompilerParams(
            dimension_semantics=("arbitrary", "arbitrary"), vmem_limit_bytes=VMEM_LIMIT),
        name="mla_attend_prompt",
    )(qt, kcat, kt, gate, x, w_uvt_h, w_out, ln_g, ln_b)


def _attn_sample_kernel(q_ref, cc_ref, ckr_ref, kn_ref, gate_ref, x_ref, w_uv_ref, w_out_ref,
                        lng_ref, lnb_ref, y_ref, m_ref, l_ref, acc_ref, o_ref, *, tq, tk):
    rows = B_HEADS * tq
    past = cc_ref.shape[1]
    q = q_ref[0].reshape(rows, KCAT)
    q_lat = q[:, :KV_LORA]
    q3 = q[:, KV_LORA:].astype(F32)
    q_rope = (q3[:, :ROPE_DIM] + q3[:, ROPE_DIM:]).astype(BF16)
    m_ref[...] = jnp.full((rows, 1), NEG_INF, F32)
    l_ref[...] = jnp.zeros((rows, 1), F32)
    acc_ref[...] = jnp.zeros((rows, KV_LORA), F32)

    for j in range(past // tk):
        cb = cc_ref[0, j * tk:(j + 1) * tk, :].astype(BF16)
        krb = ckr_ref[0, j * tk:(j + 1) * tk, :].astype(BF16)
        s = (lax.dot_general(q_lat, cb, _NT, preferred_element_type=F32)
             + lax.dot_general(q_rope, krb, _NT, preferred_element_type=F32))
        _softmax_step(s, cb, m_ref, l_ref, acc_ref)

    kn = kn_ref[0]
    s = lax.dot_general(q, kn, _NT, preferred_element_type=F32)
    _softmax_step(s, kn[:, :KV_LORA], m_ref, l_ref, acc_ref)

    y_ref[0] = _attn_output(acc_ref, l_ref, o_ref, gate_ref[0], x_ref[0],
                            w_uv_ref, w_out_ref, lng_ref, lnb_ref, tq)


def _attn_sample(qcat, cache_c, cache_kr, kcat_new, gate, x, w_uv_h, w_out, ln_g, ln_b, *, tk):
    bsz, _, tq, _ = qcat.shape
    past = cache_c.shape[1]
    rows = B_HEADS * tq
    const2 = lambda b: (0, 0)
    per_b = lambda shape: pl.BlockSpec((1,) + shape, lambda b: (b,) + (0,) * len(shape))
    kern = functools.partial(_attn_sample_kernel, tq=tq, tk=tk)
    return pl.pallas_call(
        kern,
        grid=(bsz,),
        in_specs=[
            per_b((B_HEADS, tq, KCAT)),
            per_b((past, KV_LORA)),
            per_b((past, ROPE_DIM)),
            per_b((tq, KCAT)),
            per_b((tq, B_WIDTH)),
            per_b((tq, D_MODEL)),
            pl.BlockSpec(w_uv_h.shape, lambda b: (0, 0, 0)),
            pl.BlockSpec(w_out.shape, const2),
            pl.BlockSpec((1, D_MODEL), const2),
            pl.BlockSpec((1, D_MODEL), const2),
        ],
        out_specs=per_b((tq, D_MODEL)),
        out_shape=jax.ShapeDtypeStruct((bsz, tq, D_MODEL), F32),
        scratch_shapes=[
            pltpu.VMEM((rows, 1), F32),
            pltpu.VMEM((rows, 1), F32),
            pltpu.VMEM((rows, KV_LORA), F32),
            pltpu.VMEM((tq, B_WIDTH), BF16),
        ],
        compiler_params=pltpu.CompilerParams(
            dimension_semantics=("arbitrary",), vmem_limit_bytes=VMEM_LIMIT),
        name="mla_attend_sample",
    )(qcat, cache_c, cache_kr, kcat_new, gate, x, w_uv_h, w_out, ln_g, ln_b)


def _rot_cols(w):
    half = ROPE_DIM // 2
    return jnp.concatenate([-w[..., half:], w[..., :half]], axis=-1)


def _rope_tables(pos):
    half = ROPE_DIM // 2
    inv = jnp.power(ROPE_BASE, -jnp.arange(half, dtype=F32) / half)
    ang = pos.astype(F32)[:, None] * inv[None, :]
    cos, sin = jnp.cos(ang), jnp.sin(ang)
    return jnp.concatenate([cos] * 4, axis=1), jnp.concatenate([sin] * 4, axis=1)


def kernel(x_prompt, x_sample, state_hgrn, cache_ckv, cache_krope, w_in_a, lb_gamma, a_norm_g, w_out_a,
           w_dkv, kv_norm_g, w_uk, w_uv, w_in_b, q_norm_g, w_uq, w_out_b, ln_g, ln_b):
    assert N_A == 1 and DEPTH == 2
    past = cache_ckv.shape[1]
    row = lambda a: a.reshape(1, -1)

    w_in_a_b = w_in_a[0].astype(BF16)
    w_out_a_b = w_out_a[0].astype(BF16)
    kr_w = w_dkv[:, KV_LORA:]
    w_dkv_ext = jnp.concatenate(
        [w_dkv[:, :KV_LORA], kr_w, kr_w, _rot_cols(kr_w), _rot_cols(kr_w)], axis=1).astype(BF16)
    w_uq3 = w_uq[0].reshape(Q_LORA, B_HEADS, NOPE_DIM + ROPE_DIM)
    q_rope_w = w_uq3[:, :, NOPE_DIM:]
    w_uq_ext = jnp.concatenate([
        w_uq3[:, :, :NOPE_DIM].reshape(Q_LORA, B_HEADS * NOPE_DIM),
        q_rope_w.reshape(Q_LORA, B_HEADS * ROPE_DIM),
        _rot_cols(q_rope_w).reshape(Q_LORA, B_HEADS * ROPE_DIM)], axis=1).astype(BF16)
    w_uqt = jnp.concatenate([
        w_uq3[:, :, :NOPE_DIM].reshape(Q_LORA, B_HEADS * NOPE_DIM),
        q_rope_w.reshape(Q_LORA, B_HEADS * ROPE_DIM)], axis=1).T.astype(BF16)
    w_uk_h = jnp.transpose(w_uk, (1, 0, 2)).astype(BF16)
    w_uvt_h = jnp.transpose(w_uv, (1, 2, 0)).astype(BF16)
    w_in_b_b = w_in_b[0].astype(BF16)
    w_out_b_b = w_out_b[0].astype(BF16)

    outs = []
    for x, s0, pos0, prompt in ((x_prompt, None, 0, True), (x_sample, state_hgrn[0], past, False)):
        bsz, t, _ = x.shape
        if prompt:
            bblk, tt, chunk, sub, tm = 1, HGRN_STEP_ROWS, CHUNK, HGRN_SUB_ROWS, TOKEN_TILE
        else:
            bblk, tt, chunk, sub, tm = bsz, t, t, bsz * t, t
        x1, s_fin = _hgrn_layer(x, s0, w_in_a_b, lb_gamma, row(a_norm_g[0]), w_out_a_b,
                                row(ln_g[0]), row(ln_b[0]), layer=0, bblk=bblk, tt=tt, chunk=chunk, sub=sub)
        cos2, sin2 = _rope_tables(pos0 + jnp.arange(t, dtype=jnp.int32))
        tabs = (cos2, sin2, cos2.T, sin2.T) if prompt else (cos2, sin2)
        proj_outs = _mla_proj(
            x1, tabs, w_dkv_ext, row(kv_norm_g), w_in_b_b, row(q_norm_g[0]),
            w_uqt if prompt else w_uq_ext, w_uk_h,
            bblk=bblk, tt=tm, transposed=prompt, tag="prompt" if prompt else "sample")
        if prompt:
            ckv, krope, kcat, kt, qt, gate = proj_outs
            y = _attn_prompt(qt, kcat, kt, gate, x1, w_uvt_h, w_out_b_b, row(ln_g[1]), row(ln_b[1]),
                             rc=ATTN_ROW_CHUNK)
        else:
            ckv, krope, kcat, qcat, gate = proj_outs
            y = _attn_sample(qcat, cache_ckv, cache_krope, kcat, gate, x1, w_uvt_h, w_out_b_b,
                             row(ln_g[1]), row(ln_b[1]), tk=SAMPLE_KEY_CHUNK)
        outs.append((y, s_fin[None], ckv, krope))
    (y_p, s_p, c_p, kr_p), (y_s, s_s, c_s, kr_s) = outs
    return (y_p, y_s, s_p, c_p, kr_p, s_s, c_s, kr_s)
```

```python
import functools

import jax
import jax.numpy as jnp
from jax import lax
from jax.experimental import pallas as pl
from jax.experimental.pallas import tpu as pltpu

F32 = jnp.float32
BF16 = jnp.bfloat16

D_MODEL = 1024
DEPTH = 2
CHUNK = 64
CHUNK_SHIFT = CHUNK.bit_length() - 1
N_A = DEPTH // 2
A_HEADS = 8
A_DK = 128
A_DV = 128
A_WIDTH = A_HEADS * A_DV
B_HEADS = 16
Q_LORA = 512
KV_LORA = 256
NOPE_DIM = 128
ROPE_DIM = 64
B_DV = 128
B_WIDTH = B_HEADS * B_DV
ROPE_BASE = 10000.0
EPS = 1e-6
NEG_INF = -1e30
ATTN_SCALE = (NOPE_DIM + ROPE_DIM) ** -0.5
LOG2_E = 1.4426950408889634
Q_SCALE = ATTN_SCALE * LOG2_E
ALPHA = (2 * DEPTH) ** 0.25

LANES = 128
BF16_SUBLANES = 16
KCAT = KV_LORA + 2 * ROPE_DIM
VEXT = KV_LORA + BF16_SUBLANES
VMEM_LIMIT = 56 * 1024 * 1024

HGRN_STEP_ROWS = 1024
HGRN_SUB_ROWS = 256
TOKEN_TILE = 256
ATTN_ROW_CHUNK = 512
SAMPLE_KEY_CHUNK = 1024

_NT = (((1,), (1,)), ((), ()))
_TN = (((0,), (0,)), ((), ()))


def _dot(a, b):
    return jnp.dot(a, b, preferred_element_type=F32)


def _layer_norm(x, g, b):
    mu = jnp.mean(x, axis=-1, keepdims=True)
    xc = x - mu
    var = jnp.mean(xc * xc, axis=-1, keepdims=True)
    return xc * lax.rsqrt(var + EPS) * g + b


def _rms_norm(x, g):
    return x * lax.rsqrt(jnp.mean(x * x, axis=-1, keepdims=True) + EPS) * g


def _sigmoid_pair(z):
    e = jnp.exp(-jnp.abs(z))
    r = 1.0 / (1.0 + e)
    er = e * r
    pos = z >= 0
    return jnp.where(pos, r, er), jnp.where(pos, er, r)


CUMSUM_TERMS = 3


def _split_bf16(x, n):
    terms = []
    for _ in range(n):
        t = x.astype(BF16)
        terms.append(t)
        x = x - t.astype(F32)
    return terms


def _hgrn_kernel(x_ref, s0_ref, w_in_ref, lbg_ref, ng_ref, w_out_ref, lng_ref, lnb_ref, tri_ref,
                 y_ref, sfin_ref, st_ref, o_ref, *, layer, bblk, tt, chunk, sub, zero_init):
    t = pl.program_id(1)
    rows = bblk * tt
    mid = chunk // 2

    @pl.when(t == 0)
    def _():
        for bb in range(bblk):
            for h in range(A_HEADS):
                if zero_init:
                    st_ref[bb, h] = jnp.zeros((A_DV, A_DK), F32)
                else:
                    st_ref[bb, h] = s0_ref[bb, h].T

    lbg = lbg_ref[...]
    e = jnp.exp(lbg - jnp.max(lbg, axis=0, keepdims=True))
    lb = jnp.sum(e[0:layer + 1], axis=0, keepdims=True) / jnp.sum(e, axis=0, keepdims=True)
    one_m_lb = 1.0 - lb
    ng = ng_ref[...]
    li = lax.broadcasted_iota(jnp.int32, (chunk, chunk), 0)
    si = lax.broadcasted_iota(jnp.int32, (chunk, chunk), 1)
    causal = li >= si

    x = x_ref[...].reshape(rows, D_MODEL)
    n_sub = rows // sub
    n_chunks = sub // chunk
    n_parts = 4
    tri = tri_ref[...]
    heads = [slice(h * A_DK, (h + 1) * A_DK) for h in range(A_HEADS)]

    def project(s_idx, part):
        xs = x[s_idx * sub:(s_idx + 1) * sub].astype(BF16)
        return _dot(xs, w_in_ref[:, part * A_WIDTH:(part + 1) * A_WIDTH])

    parts = [project(0, p) for p in range(n_parts)]
    for s_idx in range(n_sub):
        more = s_idx + 1 < n_sub
        next_parts = []
        chunks = range(n_chunks)
        rows_of = [slice(c * chunk, (c + 1) * chunk) for c in chunks]

        k_all, b_all = [], []
        for c in chunks:
            z = parts[1][rows_of[c]]
            sig_pos = 1.0 / (1.0 + jnp.exp(-z))
            sig_neg = 1.0 / (1.0 + jnp.exp(z))
            logf = jnp.log(lb + one_m_lb * sig_pos)
            k_all.append(one_m_lb * sig_neg)
            terms = _split_bf16(logf, CUMSUM_TERMS)
            b_all.append(_dot(tri, jnp.concatenate(terms, axis=0)))
        if more:
            next_parts.append(project(s_idx + 1, 0))

        q_ex, decay, v_all, sc_all, inc_all = [], [], [], [], []
        for c in chunks:
            bc = b_all[c]
            bmid = bc[mid:mid + 1]
            blast = bc[chunk - 1:chunk]
            qc = parts[0][rows_of[c]]
            q_mid = qc * jnp.exp(bc - bmid)
            k_mid = k_all[c] * jnp.exp(bmid - bc)
            q_in = q_mid.astype(BF16)
            k_in = k_mid.astype(BF16)
            k_dec = (k_mid * jnp.exp(blast - bmid)).astype(BF16)
            q_ex.append((q_mid * jnp.exp(bmid)).astype(BF16))
            decay.append(jnp.exp(blast))
            vc = parts[2][rows_of[c]].astype(BF16)
            v_all.append(vc)
            sc_all.append([lax.dot_general(q_in[:, hs], k_in[:, hs], _NT, preferred_element_type=F32)
                           for hs in heads])
            inc_all.append([lax.dot_general(vc[:, hs], k_dec[:, hs], _TN, preferred_element_type=F32)
                            for hs in heads])
        if more:
            next_parts.append(project(s_idx + 1, 1))

        st_in = []
        for c in chunks:
            bb = (s_idx * sub + c * chunk) // tt
            st_in.append([])
            for h, hs in enumerate(heads):
                st = st_ref[bb, h]
                st_in[c].append(st.T.astype(BF16))
                st_ref[bb, h] = st * decay[c][:, hs] + inc_all[c][h]
        if more:
            next_parts.append(project(s_idx + 1, 2))

        for c in chunks:
            r0 = s_idx * sub + c * chunk
            g = parts[3][rows_of[c]]
            silu_g = g / (1.0 + jnp.exp(-g))
            for h, hs in enumerate(heads):
                sc = jnp.where(causal, sc_all[c][h], 0.0).astype(BF16)
                o = _dot(jnp.concatenate([q_ex[c][:, hs], sc], axis=1),
                         jnp.concatenate([st_in[c][h], v_all[c][:, hs]], axis=0))
                o = _rms_norm(o, ng) * silu_g[:, hs]
                o_ref[r0:r0 + chunk, hs] = o.astype(BF16)
        if more:
            next_parts.append(project(s_idx + 1, 3))
        parts = next_parts

        ss = slice(s_idx * sub, (s_idx + 1) * sub)
        out = _dot(o_ref[ss, :], w_out_ref[...])
        y = _layer_norm(ALPHA * x[ss] + out, lng_ref[...], lnb_ref[...])
        if n_sub == 1:
            y_ref[...] = y.reshape(bblk, tt, D_MODEL)
        else:
            y_ref[0, ss, :] = y

    @pl.when(t == pl.num_programs(1) - 1)
    def _():
        for bb in range(bblk):
            for h in range(A_HEADS):
                sfin_ref[bb, h] = st_ref[bb, h].T


def _hgrn_layer(x, s0, w_in, lb_gamma, norm_g, w_out, ln_g, ln_b, *, layer, bblk, tt, chunk, sub):
    bsz, t, _ = x.shape
    rows = bblk * tt
    zero_init = s0 is None
    if zero_init:
        s0 = jnp.zeros((bblk, A_HEADS, 8, LANES), F32)
        s0_spec = pl.BlockSpec((bblk, A_HEADS, 8, LANES), lambda b, i: (0, 0, 0, 0))
    else:
        s0_spec = pl.BlockSpec((bblk, A_HEADS, A_DK, A_DV), lambda b, i: (b, 0, 0, 0))
    assert rows % sub == 0 and sub % chunk == 0 and tt % chunk == 0
    r = jnp.arange(chunk)
    tri = (r[:, None] >= r[None, :]).astype(BF16)
    tri3 = jnp.concatenate([tri] * CUMSUM_TERMS, axis=1)
    const = lambda b, i: (0, 0)
    kern = functools.partial(_hgrn_kernel, layer=layer, bblk=bblk, tt=tt, chunk=chunk, sub=sub,
                             zero_init=zero_init)
    return pl.pallas_call(
        kern,
        grid=(bsz // bblk, t // tt),
        in_specs=[
            pl.BlockSpec((bblk, tt, D_MODEL), lambda b, i: (b, i, 0)),
            s0_spec,
            pl.BlockSpec((D_MODEL, 4 * A_WIDTH), const),
            pl.BlockSpec(lb_gamma.shape, const),
            pl.BlockSpec((1, A_DV), const),
            pl.BlockSpec((A_WIDTH, D_MODEL), const),
            pl.BlockSpec((1, D_MODEL), const),
            pl.BlockSpec((1, D_MODEL), const),
            pl.BlockSpec((chunk, CUMSUM_TERMS * chunk), const),
        ],
        out_specs=[
            pl.BlockSpec((bblk, tt, D_MODEL), lambda b, i: (b, i, 0)),
            pl.BlockSpec((bblk, A_HEADS, A_DK, A_DV), lambda b, i: (b, 0, 0, 0)),
        ],
        out_shape=[
            jax.ShapeDtypeStruct((bsz, t, D_MODEL), F32),
            jax.ShapeDtypeStruct((bsz, A_HEADS, A_DK, A_DV), F32),
        ],
        scratch_shapes=[
            pltpu.VMEM((bblk, A_HEADS, A_DV, A_DK), F32),
            pltpu.VMEM((rows, A_WIDTH), BF16),
        ],
        compiler_params=pltpu.CompilerParams(
            dimension_semantics=("arbitrary", "arbitrary"), vmem_limit_bytes=VMEM_LIMIT),
        name=f"hgrn_layer_{'prompt' if zero_init else 'sample'}",
    )(x, s0, w_in, lb_gamma, norm_g, w_out, ln_g, ln_b, tri3)


def _latent_kv(xb, cos2, sin2, w_dkv_ref, kvg_ref, ckv_ref, krope_ref, kcat_ref, bblk, tt):
    kv = _dot(xb, w_dkv_ref[...])
    c = _rms_norm(kv[:, :KV_LORA], kvg_ref[...])
    krd = kv[:, KV_LORA:KV_LORA + LANES] * cos2 + kv[:, KV_LORA + LANES:KV_LORA + 2 * LANES] * sin2
    ckv_ref[...] = c.reshape(bblk, tt, KV_LORA)
    krope_ref[...] = krd[:, :ROPE_DIM].reshape(bblk, tt, ROPE_DIM)
    kcat = jnp.concatenate([c, krd], axis=1)
    kcat_ref[...] = kcat.astype(BF16).reshape(bblk, tt, KCAT)
    return kcat


def _query_latent(xb, w_in_ref, qg_ref, gate_ref, bblk, tt):
    proj = _dot(xb, w_in_ref[...])
    cq = _rms_norm(proj[:, :Q_LORA], qg_ref[...])
    gt = proj[:, Q_LORA:]
    gate_ref[...] = (gt * _sigmoid_pair(gt)[0]).reshape(bblk, tt, B_WIDTH)
    return cq


def _mla_proj_rows_kernel(x_ref, cos_ref, sin_ref, w_dkv_ref, kvg_ref, w_in_ref, qg_ref, w_uq_ref, w_uk_ref,
                          ckv_ref, krope_ref, kcat_ref, q_ref, gate_ref, *, bblk, tt):
    rows = bblk * tt
    xb = x_ref[...].reshape(rows, D_MODEL).astype(BF16)
    cos2 = jnp.concatenate([cos_ref[...]] * bblk, axis=0)
    sin2 = jnp.concatenate([sin_ref[...]] * bblk, axis=0)
    _latent_kv(xb, cos2, sin2, w_dkv_ref, kvg_ref, ckv_ref, krope_ref, kcat_ref, bblk, tt)
    cq = _query_latent(xb, w_in_ref, qg_ref, gate_ref, bblk, tt)
    qall = _dot(cq.astype(BF16), w_uq_ref[...])
    nope_w = B_HEADS * NOPE_DIM
    pair_w = (B_HEADS // 2) * LANES
    lane = lax.broadcasted_iota(jnp.int32, (rows, LANES), 1)
    for j in range(B_HEADS // 2):
        raw = qall[:, nope_w + j * LANES:nope_w + (j + 1) * LANES]
        rot = qall[:, nope_w + pair_w + j * LANES:nope_w + pair_w + (j + 1) * LANES]
        rp = (raw * cos2 + rot * sin2) * Q_SCALE
        for e in range(2):
            h = 2 * j + e
            ql = lax.dot_general(qall[:, h * NOPE_DIM:(h + 1) * NOPE_DIM].astype(BF16), w_uk_ref[h], _NT,
                                 preferred_element_type=F32) * Q_SCALE
            keep = (lane < ROPE_DIM) if e == 0 else (lane >= ROPE_DIM)
            qc = jnp.concatenate([ql, jnp.where(keep, rp, 0.0)], axis=1).astype(BF16)
            for bb in range(bblk):
                q_ref[bb, h] = qc[bb * tt:(bb + 1) * tt]


def _mla_proj_cols_kernel(x_ref, cos_ref, sin_ref, cost_ref, sint_ref, w_dkv_ref, kvg_ref, w_in_ref, qg_ref,
                          w_uqt_ref, w_uk_ref, ckv_ref, krope_ref, kcat_ref, kt_ref, q_ref, gate_ref, *, tt):
    xb = x_ref[0].astype(BF16)
    kcat = _latent_kv(xb, cos_ref[...], sin_ref[...], w_dkv_ref, kvg_ref, ckv_ref, krope_ref, kcat_ref, 1, tt)
    c_t = kcat[:, :KV_LORA].T
    kt_ref[0, 0] = jnp.concatenate([c_t, jnp.ones((VEXT - KV_LORA, tt), F32)], axis=0).astype(BF16)
    cq = _query_latent(xb, w_in_ref, qg_ref, gate_ref, 1, tt)
    qall_t = _dot(w_uqt_ref[...], cq.T.astype(BF16))
    nope_w = B_HEADS * NOPE_DIM
    half = ROPE_DIM // 2
    cos_t = cost_ref[...]
    sin_t = sint_ref[...]
    zeros = jnp.zeros((ROPE_DIM, tt), F32)
    for j in range(B_HEADS // 2):
        raw = qall_t[nope_w + j * LANES:nope_w + (j + 1) * LANES]
        rot = jnp.concatenate([-raw[half:2 * half], raw[0:half], -raw[3 * half:4 * half], raw[2 * half:3 * half]],
                              axis=0)
        rp = (raw * cos_t + rot * sin_t) * Q_SCALE
        for e in range(2):
            h = 2 * j + e
            ql = _dot(w_uk_ref[h], qall_t[h * NOPE_DIM:(h + 1) * NOPE_DIM].astype(BF16)) * Q_SCALE
            rope_rows = [rp[:ROPE_DIM], zeros] if e == 0 else [zeros, rp[ROPE_DIM:]]
            q_ref[0, 0, :, h * tt:(h + 1) * tt] = jnp.concatenate([ql] + rope_rows, axis=0).astype(BF16)


def _mla_proj(x, rope_tabs, w_dkv_ext, kv_g, w_in, q_g, w_q, w_k, *, bblk, tt, transposed, tag):
    bsz, t, _ = x.shape
    const2 = lambda b, i: (0, 0)
    const3 = lambda b, i: (0, 0, 0)
    tok = lambda w: pl.BlockSpec((bblk, tt, w), lambda b, i: (b, i, 0))
    tab = pl.BlockSpec((tt, LANES), lambda b, i: (i, 0))
    in_specs = [tok(D_MODEL), tab, tab]
    out_specs = [tok(KV_LORA), tok(ROPE_DIM), tok(KCAT)]
    out_shape = [
        jax.ShapeDtypeStruct((bsz, t, KV_LORA), F32),
        jax.ShapeDtypeStruct((bsz, t, ROPE_DIM), F32),
        jax.ShapeDtypeStruct((bsz, t, KCAT), BF16),
    ]
    if transposed:
        assert bblk == 1
        tab_t = pl.BlockSpec((LANES, tt), lambda b, i: (0, i))
        in_specs += [tab_t, tab_t]
        out_specs += [pl.BlockSpec((1, 1, VEXT, tt), lambda b, i: (b, i, 0, 0)),
                      pl.BlockSpec((1, 1, KCAT, B_HEADS * tt), lambda b, i: (b, i, 0, 0))]
        out_shape += [jax.ShapeDtypeStruct((bsz, t // tt, VEXT, tt), BF16),
                      jax.ShapeDtypeStruct((bsz, t // tt, KCAT, B_HEADS * tt), BF16)]
        kern = functools.partial(_mla_proj_cols_kernel, tt=tt)
    else:
        out_specs += [pl.BlockSpec((bblk, B_HEADS, tt, KCAT), lambda b, i: (b, 0, i, 0))]
        out_shape += [jax.ShapeDtypeStruct((bsz, B_HEADS, t, KCAT), BF16)]
        kern = functools.partial(_mla_proj_rows_kernel, bblk=bblk, tt=tt)
    in_specs += [
        pl.BlockSpec(w_dkv_ext.shape, const2),
        pl.BlockSpec((1, KV_LORA), const2),
        pl.BlockSpec(w_in.shape, const2),
        pl.BlockSpec((1, Q_LORA), const2),
        pl.BlockSpec(w_q.shape, const2),
        pl.BlockSpec(w_k.shape, const3),
    ]
    out_specs += [tok(B_WIDTH)]
    out_shape += [jax.ShapeDtypeStruct((bsz, t, B_WIDTH), F32)]
    return pl.pallas_call(
        kern,
        grid=(bsz // bblk, t // tt),
        in_specs=in_specs,
        out_specs=out_specs,
        out_shape=out_shape,
        compiler_params=pltpu.CompilerParams(
            dimension_semantics=("arbitrary", "arbitrary"), vmem_limit_bytes=VMEM_LIMIT),
        name=f"mla_proj_{tag}",
    )(x, *rope_tabs, w_dkv_ext, kv_g, w_in, q_g, w_q, w_k)


def _softmax_step(s, v, m_ref, l_ref, acc_ref):
    m_prev = m_ref[...]
    m_new = jnp.maximum(m_prev, jnp.max(s, axis=1, keepdims=True))
    a = jnp.exp2(m_prev - m_new)
    p = jnp.exp2(s - m_new)
    l_ref[...] = a * l_ref[...] + jnp.sum(p, axis=1, keepdims=True)
    acc_ref[...] = a * acc_ref[...] + _dot(p.astype(BF16), v)
    m_ref[...] = m_new


def _attn_output(acc_ref, l_ref, o_ref, gate, x, w_uv_ref, w_out_ref, lng_ref, lnb_ref, tq):
    for h in range(B_HEADS):
        hs = slice(h * tq, (h + 1) * tq)
        o_lat = acc_ref[hs, :] / l_ref[hs, :]
        oh = lax.dot_general(o_lat.astype(BF16), w_uv_ref[h], _NT, preferred_element_type=F32)
        o_ref[:, h * B_DV:(h + 1) * B_DV] = (oh * gate[:, h * B_DV:(h + 1) * B_DV]).astype(BF16)
    out = _dot(o_ref[...], w_out_ref[...])
    return _layer_norm(ALPHA * x + out, lng_ref[...], lnb_ref[...])


def _attn_prompt_kernel(qt_ref, kcat_ref, kt_ref, gate_ref, x_ref, w_uvt_ref, w_out_ref, lng_ref, lnb_ref,
                        y_ref, m_ref, acc_ref, o_ref, *, tq, rc):
    i = pl.program_id(1)
    rows = B_HEADS * tq
    n_rc = rows // rc

    def kv_step(j, visible, nblk=1):
        init = visible is not None
        kb = kcat_ref[0, pl.ds(pl.multiple_of(j * tq, tq), nblk * tq), :]
        ct = jnp.concatenate([kt_ref[0, j + n] for n in range(nblk)], axis=1)

        def scores(r):
            return _dot(kb, qt_ref[0, 0, :, r * rc:(r + 1) * rc])

        def softmax(r, st):
            ls = slice(r * rc, (r + 1) * rc)
            if init:
                st = jnp.where(visible, st, NEG_INF)
                m_new = jnp.max(st, axis=0, keepdims=True)
                a = None
            else:
                m_prev = m_ref[:, ls]
                m_new = jnp.maximum(m_prev, jnp.max(st, axis=0, keepdims=True))
                a = jnp.exp2(m_prev - m_new)
            pt = jnp.exp2(st - m_new)
            m_ref[:, ls] = m_new
            return a, pt.astype(BF16)

        st_cur = scores(0)
        pend = None
        for s in range(n_rc + 1):
            st_next = scores(s + 1) if s + 1 < n_rc else None
            if pend is not None:
                a_prev, pt_prev = pend
                pv = _dot(ct, pt_prev)
            new_pend = softmax(s, st_cur) if s < n_rc else None
            if pend is not None:
                ls = slice((s - 1) * rc, s * rc)
                acc_ref[:, ls] = pv if init else a_prev * acc_ref[:, ls] + pv
            pend = new_pend
            st_cur = st_next

    def visible_mask(nblk):
        key = lax.broadcasted_iota(jnp.int32, (nblk * tq, rc), 0) - (nblk - 1) * tq
        q_chunk = (lax.broadcasted_iota(jnp.int32, (nblk * tq, rc), 1) & (tq - 1)) >> CHUNK_SHIFT
        return (key >> CHUNK_SHIFT) <= q_chunk

    odd = i % 2

    @pl.when(odd == 0)
    def _():
        kv_step(i, visible_mask(1))

    @pl.when(odd == 1)
    def _():
        kv_step(i - 1, visible_mask(2), 2)

    def body(jj, carry):
        kv_step(4 * jj, None, 4)
        return carry

    n_prev = i - odd
    n_quads = n_prev // 4
    lax.fori_loop(0, n_quads, body, 0)

    @pl.when(n_prev - 4 * n_quads == 2)
    def _():
        kv_step(4 * n_quads, None, 2)

    gate = gate_ref[0]
    for h in range(B_HEADS):
        hs = slice(h * tq, (h + 1) * tq)
        o_lat = acc_ref[0:KV_LORA, hs] / acc_ref[KV_LORA:KV_LORA + 1, hs]
        oh = _dot(w_uvt_ref[h], o_lat.astype(BF16)).T
        o_ref[:, h * B_DV:(h + 1) * B_DV] = (oh * gate[:, h * B_DV:(h + 1) * B_DV]).astype(BF16)
    out = _dot(o_ref[...], w_out_ref[...])
    y_ref[0] = _layer_norm(ALPHA * x_ref[0] + out, lng_ref[...], lnb_ref[...])


def _attn_prompt(qt, kcat, kt, gate, x, w_uvt_h, w_out, ln_g, ln_b, *, rc):
    bsz, nq, _, rows = qt.shape
    tq = rows // B_HEADS
    t = kcat.shape[1]
    assert kt.shape[-1] == tq and tq & (tq - 1) == 0 and tq % CHUNK == 0 and rc % tq == 0
    const2 = lambda b, i: (0, 0)
    kern = functools.partial(_attn_prompt_kernel, tq=tq, rc=rc)
    return pl.pallas_call(
        kern,
        grid=(bsz, nq),
        in_specs=[
            pl.BlockSpec((1, 1, KCAT, rows), lambda b, i: (b, i, 0, 0)),
            pl.BlockSpec((1, t, KCAT), lambda b, i: (b, 0, 0)),
            pl.BlockSpec((1, nq, VEXT, tq), lambda b, i: (b, 0, 0, 0)),
            pl.BlockSpec((1, tq, B_WIDTH), lambda b, i: (b, i, 0)),
            pl.BlockSpec((1, tq, D_MODEL), lambda b, i: (b, i, 0)),
            pl.BlockSpec(w_uvt_h.shape, lambda b, i: (0, 0, 0)),
            pl.BlockSpec(w_out.shape, const2),
            pl.BlockSpec((1, D_MODEL), const2),
            pl.BlockSpec((1, D_MODEL), const2),
        ],
        out_specs=pl.BlockSpec((1, tq, D_MODEL), lambda b, i: (b, i, 0)),
        out_shape=jax.ShapeDtypeStruct((bsz, t, D_MODEL), F32),
        scratch_shapes=[
            pltpu.VMEM((1, rows), F32),
            pltpu.VMEM((VEXT, rows), F32),
            pltpu.VMEM((tq, B_WIDTH), BF16),
        ],
        compiler_params=pltpu.CompilerParams(
            dimension_semantics=("arbitrary", "arbitrary"), vmem_limit_bytes=VMEM_LIMIT),
        name="mla_attend_prompt",
    )(qt, kcat, kt, gate, x, w_uvt_h, w_out, ln_g, ln_b)


def _attn_sample_kernel(q_ref, cc_ref, ckr_ref, kn_ref, gate_ref, x_ref, w_uv_ref, w_out_ref,
                        lng_ref, lnb_ref, y_ref, m_ref, l_ref, acc_ref, o_ref, *, tq, tk):
    rows = B_HEADS * tq
    past = cc_ref.shape[1]
    q = q_ref[0].reshape(rows, KCAT)
    q_lat = q[:, :KV_LORA]
    q3 = q[:, KV_LORA:].astype(F32)
    q_rope = (q3[:, :ROPE_DIM] + q3[:, ROPE_DIM:]).astype(BF16)
    m_ref[...] = jnp.full((rows, 1), NEG_INF, F32)
    l_ref[...] = jnp.zeros((rows, 1), F32)
    acc_ref[...] = jnp.zeros((rows, KV_LORA), F32)

    for j in range(past // tk):
        cb = cc_ref[0, j * tk:(j + 1) * tk, :].astype(BF16)
        krb = ckr_ref[0, j * tk:(j + 1) * tk, :].astype(BF16)
        s = (lax.dot_general(q_lat, cb, _NT, preferred_element_type=F32)
             + lax.dot_general(q_rope, krb, _NT, preferred_element_type=F32))
        _softmax_step(s, cb, m_ref, l_ref, acc_ref)

    kn = kn_ref[0]
    s = lax.dot_general(q, kn, _NT, preferred_element_type=F32)
    _softmax_step(s, kn[:, :KV_LORA], m_ref, l_ref, acc_ref)

    y_ref[0] = _attn_output(acc_ref, l_ref, o_ref, gate_ref[0], x_ref[0],
                            w_uv_ref, w_out_ref, lng_ref, lnb_ref, tq)


def _attn_sample(qcat, cache_c, cache_kr, kcat_new, gate, x, w_uv_h, w_out, ln_g, ln_b, *, tk):
    bsz, _, tq, _ = qcat.shape
    past = cache_c.shape[1]
    rows = B_HEADS * tq
    const2 = lambda b: (0, 0)
    per_b = lambda shape: pl.BlockSpec((1,) + shape, lambda b: (b,) + (0,) * len(shape))
    kern = functools.partial(_attn_sample_kernel, tq=tq, tk=tk)
    return pl.pallas_call(
        kern,
        grid=(bsz,),
        in_specs=[
            per_b((B_HEADS, tq, KCAT)),
            per_b((past, KV_LORA)),
            per_b((past, ROPE_DIM)),
            per_b((tq, KCAT)),
            per_b((tq, B_WIDTH)),
            per_b((tq, D_MODEL)),
            pl.BlockSpec(w_uv_h.shape, lambda b: (0, 0, 0)),
            pl.BlockSpec(w_out.shape, const2),
            pl.BlockSpec((1, D_MODEL), const2),
            pl.BlockSpec((1, D_MODEL), const2),
        ],
        out_specs=per_b((tq, D_MODEL)),
        out_shape=jax.ShapeDtypeStruct((bsz, tq, D_MODEL), F32),
        scratch_shapes=[
            pltpu.VMEM((rows, 1), F32),
            pltpu.VMEM((rows, 1), F32),
            pltpu.VMEM((rows, KV_LORA), F32),
            pltpu.VMEM((tq, B_WIDTH), BF16),
        ],
        compiler_params=pltpu.CompilerParams(
            dimension_semantics=("arbitrary",), vmem_limit_bytes=VMEM_LIMIT),
        name="mla_attend_sample",
    )(qcat, cache_c, cache_kr, kcat_new, gate, x, w_uv_h, w_out, ln_g, ln_b)


def _rot_cols(w):
    half = ROPE_DIM // 2
    return jnp.concatenate([-w[..., half:], w[..., :half]], axis=-1)


def _rope_tables(pos):
    half = ROPE_DIM // 2
    inv = jnp.power(ROPE_BASE, -jnp.arange(half, dtype=F32) / half)
    ang = pos.astype(F32)[:, None] * inv[None, :]
    cos, sin = jnp.cos(ang), jnp.sin(ang)
    return jnp.concatenate([cos] * 4, axis=1), jnp.concatenate([sin] * 4, axis=1)


def kernel(x_prompt, x_sample, state_hgrn, cache_ckv, cache_krope, w_in_a, lb_gamma, a_norm_g, w_out_a,
           w_dkv, kv_norm_g, w_uk, w_uv, w_in_b, q_norm_g, w_uq, w_out_b, ln_g, ln_b):
    assert N_A == 1 and DEPTH == 2
    past = cache_ckv.shape[1]
    row = lambda a: a.reshape(1, -1)

    w_in_a_b = w_in_a[0].astype(BF16)
    w_out_a_b = w_out_a[0].astype(BF16)
    kr_w = w_dkv[:, KV_LORA:]
    w_dkv_ext = jnp.concatenate(
        [w_dkv[:, :KV_LORA], kr_w, kr_w, _rot_cols(kr_w), _rot_cols(kr_w)], axis=1).astype(BF16)
    w_uq3 = w_uq[0].reshape(Q_LORA, B_HEADS, NOPE_DIM + ROPE_DIM)
    q_rope_w = w_uq3[:, :, NOPE_DIM:]
    w_uq_ext = jnp.concatenate([
        w_uq3[:, :, :NOPE_DIM].reshape(Q_LORA, B_HEADS * NOPE_DIM),
        q_rope_w.reshape(Q_LORA, B_HEADS * ROPE_DIM),
        _rot_cols(q_rope_w).reshape(Q_LORA, B_HEADS * ROPE_DIM)], axis=1).astype(BF16)
    w_uqt = jnp.concatenate([
        w_uq3[:, :, :NOPE_DIM].reshape(Q_LORA, B_HEADS * NOPE_DIM),
        q_rope_w.reshape(Q_LORA, B_HEADS * ROPE_DIM)], axis=1).T.astype(BF16)
    w_uk_h = jnp.transpose(w_uk, (1, 0, 2)).astype(BF16)
    w_uvt_h = jnp.transpose(w_uv, (1, 2, 0)).astype(BF16)
    w_in_b_b = w_in_b[0].astype(BF16)
    w_out_b_b = w_out_b[0].astype(BF16)

    outs = []
    for x, s0, pos0, prompt in ((x_prompt, None, 0, True), (x_sample, state_hgrn[0], past, False)):
        bsz, t, _ = x.shape
        if prompt:
            bblk, tt, chunk, sub, tm = 1, HGRN_STEP_ROWS, CHUNK, HGRN_SUB_ROWS, TOKEN_TILE
        else:
            bblk, tt, chunk, sub, tm = bsz, t, t, bsz * t, t
        x1, s_fin = _hgrn_layer(x, s0, w_in_a_b, lb_gamma, row(a_norm_g[0]), w_out_a_b,
                                row(ln_g[0]), row(ln_b[0]), layer=0, bblk=bblk, tt=tt, chunk=chunk, sub=sub)
        cos2, sin2 = _rope_tables(pos0 + jnp.arange(t, dtype=jnp.int32))
        tabs = (cos2, sin2, cos2.T, sin2.T) if prompt else (cos2, sin2)
        proj_outs = _mla_proj(
            x1, tabs, w_dkv_ext, row(kv_norm_g), w_in_b_b, row(q_norm_g[0]),
            w_uqt if prompt else w_uq_ext, w_uk_h,
            bblk=bblk, tt=tm, transposed=prompt, tag="prompt" if prompt else "sample")
        if prompt:
            ckv, krope, kcat, kt, qt, gate = proj_outs
            y = _attn_prompt(qt, kcat, kt, gate, x1, w_uvt_h, w_out_b_b, row(ln_g[1]), row(ln_b[1]),
                             rc=ATTN_ROW_CHUNK)
        else:
            ckv, krope, kcat, qcat, gate = proj_outs
            y = _attn_sample(qcat, cache_ckv, cache_krope, kcat, gate, x1, w_uvt_h, w_out_b_b,
                             row(ln_g[1]), row(ln_b[1]), tk=SAMPLE_KEY_CHUNK)
        outs.append((y, s_fin[None], ckv, krope))
    (y_p, s_p, c_p, kr_p), (y_s, s_s, c_s, kr_s) = outs
    return (y_p, y_s, s_p, c_p, kr_p, s_s, c_s, kr_s)
```

```python
import functools

import jax
import jax.numpy as jnp
from jax import lax
from jax.experimental import pallas as pl
from jax.experimental.pallas import tpu as pltpu

F32 = jnp.float32
BF16 = jnp.bfloat16

D_MODEL = 1024
DEPTH = 2
CHUNK = 64
CHUNK_SHIFT = CHUNK.bit_length() - 1
N_A = DEPTH // 2
A_HEADS = 8
A_DK = 128
A_DV = 128
A_WIDTH = A_HEADS * A_DV
B_HEADS = 16
Q_LORA = 512
KV_LORA = 256
NOPE_DIM = 128
ROPE_DIM = 64
B_DV = 128
B_WIDTH = B_HEADS * B_DV
ROPE_BASE = 10000.0
EPS = 1e-6
NEG_INF = -1e30
ATTN_SCALE = (NOPE_DIM + ROPE_DIM) ** -0.5
LOG2_E = 1.4426950408889634
Q_SCALE = ATTN_SCALE * LOG2_E
ALPHA = (2 * DEPTH) ** 0.25

LANES = 128
BF16_SUBLANES = 16
KCAT = KV_LORA + 2 * ROPE_DIM
VEXT = KV_LORA + BF16_SUBLANES
VMEM_LIMIT = 56 * 1024 * 1024

HGRN_STEP_ROWS = 1024
HGRN_SUB_ROWS = 256
TOKEN_TILE = 256
ATTN_ROW_CHUNK = 512
OUTPUT_QUERY_BLOCKS = 2
SAMPLE_KEY_CHUNK = 1024

_NT = (((1,), (1,)), ((), ()))
_TN = (((0,), (0,)), ((), ()))


def _dot(a, b):
    return jnp.dot(a, b, preferred_element_type=F32)


def _layer_norm(x, g, b):
    mu = jnp.mean(x, axis=-1, keepdims=True)
    xc = x - mu
    var = jnp.mean(xc * xc, axis=-1, keepdims=True)
    return xc * lax.rsqrt(var + EPS) * g + b


def _rms_norm(x, g):
    return x * lax.rsqrt(jnp.mean(x * x, axis=-1, keepdims=True) + EPS) * g


def _sigmoid_pair(z):
    e = jnp.exp(-jnp.abs(z))
    r = 1.0 / (1.0 + e)
    er = e * r
    pos = z >= 0
    return jnp.where(pos, r, er), jnp.where(pos, er, r)


CUMSUM_TERMS = 3


def _split_bf16(x, n):
    terms = []
    for _ in range(n):
        t = x.astype(BF16)
        terms.append(t)
        x = x - t.astype(F32)
    return terms


def _hgrn_kernel(x_ref, s0_ref, w_in_ref, lbg_ref, ng_ref, w_out_ref, lng_ref, lnb_ref, tri_ref,
                 y_ref, sfin_ref, st_ref, o_ref, *, layer, bblk, tt, chunk, sub, zero_init):
    t = pl.program_id(1)
    rows = bblk * tt
    mid = chunk // 2

    @pl.when(t == 0)
    def _():
        for bb in range(bblk):
            for h in range(A_HEADS):
                if zero_init:
                    st_ref[bb, h] = jnp.zeros((A_DV, A_DK), F32)
                else:
                    st_ref[bb, h] = s0_ref[bb, h].T

    lbg = lbg_ref[...]
    e = jnp.exp(lbg - jnp.max(lbg, axis=0, keepdims=True))
    lb = jnp.sum(e[0:layer + 1], axis=0, keepdims=True) / jnp.sum(e, axis=0, keepdims=True)
    one_m_lb = 1.0 - lb
    ng = ng_ref[...]
    li = lax.broadcasted_iota(jnp.int32, (chunk, chunk), 0)
    si = lax.broadcasted_iota(jnp.int32, (chunk, chunk), 1)
    causal = li >= si

    x = x_ref[...].reshape(rows, D_MODEL)
    n_sub = rows // sub
    n_chunks = sub // chunk
    n_parts = 4
    tri = tri_ref[...]
    heads = [slice(h * A_DK, (h + 1) * A_DK) for h in range(A_HEADS)]

    def project(s_idx, part):
        xs = x[s_idx * sub:(s_idx + 1) * sub].astype(BF16)
        return _dot(xs, w_in_ref[:, part * A_WIDTH:(part + 1) * A_WIDTH])

    parts = [project(0, p) for p in range(n_parts)]
    for s_idx in range(n_sub):
        more = s_idx + 1 < n_sub
        next_parts = []
        chunks = range(n_chunks)
        rows_of = [slice(c * chunk, (c + 1) * chunk) for c in chunks]

        k_all, b_all = [], []
        for c in chunks:
            z = parts[1][rows_of[c]]
            sig_pos = 1.0 / (1.0 + jnp.exp(-z))
            sig_neg = 1.0 / (1.0 + jnp.exp(z))
            logf = jnp.log(lb + one_m_lb * sig_pos)
            k_all.append(one_m_lb * sig_neg)
            terms = _split_bf16(logf, CUMSUM_TERMS)
            b_all.append(_dot(tri, jnp.concatenate(terms, axis=0)))
        if more:
            next_parts.append(project(s_idx + 1, 0))

        q_ex, decay, v_all, sc_all, inc_all = [], [], [], [], []
        for c in chunks:
            bc = b_all[c]
            bmid = bc[mid:mid + 1]
            blast = bc[chunk - 1:chunk]
            qc = parts[0][rows_of[c]]
            q_mid = qc * jnp.exp(bc - bmid)
            k_mid = k_all[c] * jnp.exp(bmid - bc)
            q_in = q_mid.astype(BF16)
            k_in = k_mid.astype(BF16)
            k_dec = (k_mid * jnp.exp(blast - bmid)).astype(BF16)
            q_ex.append((q_mid * jnp.exp(bmid)).astype(BF16))
            decay.append(jnp.exp(blast))
            vc = parts[2][rows_of[c]].astype(BF16)
            v_all.append(vc)
            sc_all.append([lax.dot_general(q_in[:, hs], k_in[:, hs], _NT, preferred_element_type=F32)
                           for hs in heads])
            inc_all.append([lax.dot_general(vc[:, hs], k_dec[:, hs], _TN, preferred_element_type=F32)
                            for hs in heads])
        if more:
            next_parts.append(project(s_idx + 1, 1))

        st_in = []
        for c in chunks:
            bb = (s_idx * sub + c * chunk) // tt
            st_in.append([])
            for h, hs in enumerate(heads):
                st = st_ref[bb, h]
                st_in[c].append(st.T.astype(BF16))
                st_ref[bb, h] = st * decay[c][:, hs] + inc_all[c][h]
        if more:
            next_parts.append(project(s_idx + 1, 2))

        for c in chunks:
            r0 = s_idx * sub + c * chunk
            g = parts[3][rows_of[c]]
            silu_g = g / (1.0 + jnp.exp(-g))
            for h, hs in enumerate(heads):
                sc = jnp.where(causal, sc_all[c][h], 0.0).astype(BF16)
                o = _dot(jnp.concatenate([q_ex[c][:, hs], sc], axis=1),
                         jnp.concatenate([st_in[c][h], v_all[c][:, hs]], axis=0))
                o = _rms_norm(o, ng) * silu_g[:, hs]
                o_ref[r0:r0 + chunk, hs] = o.astype(BF16)
        if more:
            next_parts.append(project(s_idx + 1, 3))
        parts = next_parts

        ss = slice(s_idx * sub, (s_idx + 1) * sub)
        out = _dot(o_ref[ss, :], w_out_ref[...])
        y = _layer_norm(ALPHA * x[ss] + out, lng_ref[...], lnb_ref[...])
        if n_sub == 1:
            y_ref[...] = y.reshape(bblk, tt, D_MODEL)
        else:
            y_ref[0, ss, :] = y

    @pl.when(t == pl.num_programs(1) - 1)
    def _():
        for bb in range(bblk):
            for h in range(A_HEADS):
                sfin_ref[bb, h] = st_ref[bb, h].T


def _hgrn_layer(x, s0, w_in, lb_gamma, norm_g, w_out, ln_g, ln_b, *, layer, bblk, tt, chunk, sub):
    bsz, t, _ = x.shape
    rows = bblk * tt
    zero_init = s0 is None
    if zero_init:
        s0 = jnp.zeros((bblk, A_HEADS, 8, LANES), F32)
        s0_spec = pl.BlockSpec((bblk, A_HEADS, 8, LANES), lambda b, i: (0, 0, 0, 0))
    else:
        s0_spec = pl.BlockSpec((bblk, A_HEADS, A_DK, A_DV), lambda b, i: (b, 0, 0, 0))
    assert rows % sub == 0 and sub % chunk == 0 and tt % chunk == 0
    r = jnp.arange(chunk)
    tri = (r[:, None] >= r[None, :]).astype(BF16)
    tri3 = jnp.concatenate([tri] * CUMSUM_TERMS, axis=1)
    const = lambda b, i: (0, 0)
    kern = functools.partial(_hgrn_kernel, layer=layer, bblk=bblk, tt=tt, chunk=chunk, sub=sub,
                             zero_init=zero_init)
    return pl.pallas_call(
        kern,
        grid=(bsz // bblk, t // tt),
        in_specs=[
            pl.BlockSpec((bblk, tt, D_MODEL), lambda b, i: (b, i, 0)),
            s0_spec,
            pl.BlockSpec((D_MODEL, 4 * A_WIDTH), const),
            pl.BlockSpec(lb_gamma.shape, const),
            pl.BlockSpec((1, A_DV), const),
            pl.BlockSpec((A_WIDTH, D_MODEL), const),
            pl.BlockSpec((1, D_MODEL), const),
            pl.BlockSpec((1, D_MODEL), const),
            pl.BlockSpec((chunk, CUMSUM_TERMS * chunk), const),
        ],
        out_specs=[
            pl.BlockSpec((bblk, tt, D_MODEL), lambda b, i: (b, i, 0)),
            pl.BlockSpec((bblk, A_HEADS, A_DK, A_DV), lambda b, i: (b, 0, 0, 0)),
        ],
        out_shape=[
            jax.ShapeDtypeStruct((bsz, t, D_MODEL), F32),
            jax.ShapeDtypeStruct((bsz, A_HEADS, A_DK, A_DV), F32),
        ],
        scratch_shapes=[
            pltpu.VMEM((bblk, A_HEADS, A_DV, A_DK), F32),
            pltpu.VMEM((rows, A_WIDTH), BF16),
        ],
        compiler_params=pltpu.CompilerParams(
            dimension_semantics=("arbitrary", "arbitrary"), vmem_limit_bytes=VMEM_LIMIT),
        name=f"hgrn_layer_{'prompt' if zero_init else 'sample'}",
    )(x, s0, w_in, lb_gamma, norm_g, w_out, ln_g, ln_b, tri3)


def _latent_kv(xb, cos2, sin2, w_dkv_ref, kvg_ref, ckv_ref, krope_ref, kcat_ref, bblk, tt):
    kv = _dot(xb, w_dkv_ref[...])
    c = _rms_norm(kv[:, :KV_LORA], kvg_ref[...])
    krd = kv[:, KV_LORA:KV_LORA + LANES] * cos2 + kv[:, KV_LORA + LANES:KV_LORA + 2 * LANES] * sin2
    ckv_ref[...] = c.reshape(bblk, tt, KV_LORA)
    krope_ref[...] = krd[:, :ROPE_DIM].reshape(bblk, tt, ROPE_DIM)
    kcat = jnp.concatenate([c, krd], axis=1)
    kcat_ref[...] = kcat.astype(BF16).reshape(bblk, tt, KCAT)
    return kcat


def _query_latent(xb, w_in_ref, qg_ref, gate_ref, bblk, tt):
    proj = _dot(xb, w_in_ref[...])
    cq = _rms_norm(proj[:, :Q_LORA], qg_ref[...])
    gt = proj[:, Q_LORA:]
    gate_ref[...] = (gt * _sigmoid_pair(gt)[0]).reshape(bblk, tt, B_WIDTH)
    return cq


def _mla_proj_rows_kernel(x_ref, cos_ref, sin_ref, w_dkv_ref, kvg_ref, w_in_ref, qg_ref, w_uq_ref, w_uk_ref,
                          ckv_ref, krope_ref, kcat_ref, q_ref, gate_ref, *, bblk, tt):
    rows = bblk * tt
    xb = x_ref[...].reshape(rows, D_MODEL).astype(BF16)
    cos2 = jnp.concatenate([cos_ref[...]] * bblk, axis=0)
    sin2 = jnp.concatenate([sin_ref[...]] * bblk, axis=0)
    _latent_kv(xb, cos2, sin2, w_dkv_ref, kvg_ref, ckv_ref, krope_ref, kcat_ref, bblk, tt)
    cq = _query_latent(xb, w_in_ref, qg_ref, gate_ref, bblk, tt)
    qall = _dot(cq.astype(BF16), w_uq_ref[...])
    nope_w = B_HEADS * NOPE_DIM
    pair_w = (B_HEADS // 2) * LANES
    lane = lax.broadcasted_iota(jnp.int32, (rows, LANES), 1)
    for j in range(B_HEADS // 2):
        raw = qall[:, nope_w + j * LANES:nope_w + (j + 1) * LANES]
        rot = qall[:, nope_w + pair_w + j * LANES:nope_w + pair_w + (j + 1) * LANES]
        rp = (raw * cos2 + rot * sin2) * Q_SCALE
        for e in range(2):
            h = 2 * j + e
            ql = lax.dot_general(qall[:, h * NOPE_DIM:(h + 1) * NOPE_DIM].astype(BF16), w_uk_ref[h], _NT,
                                 preferred_element_type=F32) * Q_SCALE
            keep = (lane < ROPE_DIM) if e == 0 else (lane >= ROPE_DIM)
            qc = jnp.concatenate([ql, jnp.where(keep, rp, 0.0)], axis=1).astype(BF16)
            for bb in range(bblk):
                q_ref[bb, h] = qc[bb * tt:(bb + 1) * tt]


def _mla_proj_cols_kernel(x_ref, cos_ref, sin_ref, cost_ref, sint_ref, w_dkv_ref, kvg_ref, w_in_ref, qg_ref,
                          w_uqt_ref, w_uk_ref, ckv_ref, krope_ref, kcat_ref, kt_ref, q_ref, gate_ref, *, tt):
    xb = x_ref[0].astype(BF16)
    kcat = _latent_kv(xb, cos_ref[...], sin_ref[...], w_dkv_ref, kvg_ref, ckv_ref, krope_ref, kcat_ref, 1, tt)
    c_t = kcat[:, :KV_LORA].T
    kt_ref[0, 0] = jnp.concatenate([c_t, jnp.ones((VEXT - KV_LORA, tt), F32)], axis=0).astype(BF16)
    cq = _query_latent(xb, w_in_ref, qg_ref, gate_ref, 1, tt)
    qall_t = _dot(w_uqt_ref[...], cq.T.astype(BF16))
    nope_w = B_HEADS * NOPE_DIM
    half = ROPE_DIM // 2
    cos_t = cost_ref[...]
    sin_t = sint_ref[...]
    zeros = jnp.zeros((ROPE_DIM, tt), F32)
    for j in range(B_HEADS // 2):
        raw = qall_t[nope_w + j * LANES:nope_w + (j + 1) * LANES]
        rot = jnp.concatenate([-raw[half:2 * half], raw[0:half], -raw[3 * half:4 * half], raw[2 * half:3 * half]],
                              axis=0)
        rp = (raw * cos_t + rot * sin_t) * Q_SCALE
        for e in range(2):
            h = 2 * j + e
            ql = _dot(w_uk_ref[h], qall_t[h * NOPE_DIM:(h + 1) * NOPE_DIM].astype(BF16)) * Q_SCALE
            rope_rows = [rp[:ROPE_DIM], zeros] if e == 0 else [zeros, rp[ROPE_DIM:]]
            q_ref[0, 0, :, h * tt:(h + 1) * tt] = jnp.concatenate([ql] + rope_rows, axis=0).astype(BF16)


def _mla_proj(x, rope_tabs, w_dkv_ext, kv_g, w_in, q_g, w_q, w_k, *, bblk, tt, transposed, tag):
    bsz, t, _ = x.shape
    const2 = lambda b, i: (0, 0)
    const3 = lambda b, i: (0, 0, 0)
    tok = lambda w: pl.BlockSpec((bblk, tt, w), lambda b, i: (b, i, 0))
    tab = pl.BlockSpec((tt, LANES), lambda b, i: (i, 0))
    in_specs = [tok(D_MODEL), tab, tab]
    out_specs = [tok(KV_LORA), tok(ROPE_DIM), tok(KCAT)]
    out_shape = [
        jax.ShapeDtypeStruct((bsz, t, KV_LORA), F32),
        jax.ShapeDtypeStruct((bsz, t, ROPE_DIM), F32),
        jax.ShapeDtypeStruct((bsz, t, KCAT), BF16),
    ]
    if transposed:
        assert bblk == 1
        tab_t = pl.BlockSpec((LANES, tt), lambda b, i: (0, i))
        in_specs += [tab_t, tab_t]
        out_specs += [pl.BlockSpec((1, 1, VEXT, tt), lambda b, i: (b, i, 0, 0)),
                      pl.BlockSpec((1, 1, KCAT, B_HEADS * tt), lambda b, i: (b, i, 0, 0))]
        out_shape += [jax.ShapeDtypeStruct((bsz, t // tt, VEXT, tt), BF16),
                      jax.ShapeDtypeStruct((bsz, t // tt, KCAT, B_HEADS * tt), BF16)]
        kern = functools.partial(_mla_proj_cols_kernel, tt=tt)
    else:
        out_specs += [pl.BlockSpec((bblk, B_HEADS, tt, KCAT), lambda b, i: (b, 0, i, 0))]
        out_shape += [jax.ShapeDtypeStruct((bsz, B_HEADS, t, KCAT), BF16)]
        kern = functools.partial(_mla_proj_rows_kernel, bblk=bblk, tt=tt)
    in_specs += [
        pl.BlockSpec(w_dkv_ext.shape, const2),
        pl.BlockSpec((1, KV_LORA), const2),
        pl.BlockSpec(w_in.shape, const2),
        pl.BlockSpec((1, Q_LORA), const2),
        pl.BlockSpec(w_q.shape, const2),
        pl.BlockSpec(w_k.shape, const3),
    ]
    out_specs += [tok(B_WIDTH)]
    out_shape += [jax.ShapeDtypeStruct((bsz, t, B_WIDTH), F32)]
    return pl.pallas_call(
        kern,
        grid=(bsz // bblk, t // tt),
        in_specs=in_specs,
        out_specs=out_specs,
        out_shape=out_shape,
        compiler_params=pltpu.CompilerParams(
            dimension_semantics=("arbitrary", "arbitrary"), vmem_limit_bytes=VMEM_LIMIT),
        name=f"mla_proj_{tag}",
    )(x, *rope_tabs, w_dkv_ext, kv_g, w_in, q_g, w_q, w_k)


def _softmax_step(s, v, m_ref, l_ref, acc_ref):
    m_prev = m_ref[...]
    m_new = jnp.maximum(m_prev, jnp.max(s, axis=1, keepdims=True))
    a = jnp.exp2(m_prev - m_new)
    p = jnp.exp2(s - m_new)
    l_ref[...] = a * l_ref[...] + jnp.sum(p, axis=1, keepdims=True)
    acc_ref[...] = a * acc_ref[...] + _dot(p.astype(BF16), v)
    m_ref[...] = m_new


def _attn_output(acc_ref, l_ref, o_ref, gate, x, w_uv_ref, w_out_ref, lng_ref, lnb_ref, tq):
    for h in range(B_HEADS):
        hs = slice(h * tq, (h + 1) * tq)
        o_lat = acc_ref[hs, :] / l_ref[hs, :]
        oh = lax.dot_general(o_lat.astype(BF16), w_uv_ref[h], _NT, preferred_element_type=F32)
        o_ref[:, h * B_DV:(h + 1) * B_DV] = (oh * gate[:, h * B_DV:(h + 1) * B_DV]).astype(BF16)
    out = _dot(o_ref[...], w_out_ref[...])
    return _layer_norm(ALPHA * x + out, lng_ref[...], lnb_ref[...])


def _attn_prompt_kernel(qt_ref, kcat_ref, kt_ref, ot_ref, m_ref, acc_ref, *, tq, rc):
    i = pl.program_id(1)
    rows = B_HEADS * tq
    n_rc = rows // rc

    def kv_step(j, visible, nblk=1):
        init = visible is not None
        kb = kcat_ref[0, pl.ds(pl.multiple_of(j * tq, tq), nblk * tq), :]
        ct = jnp.concatenate([kt_ref[0, j + n] for n in range(nblk)], axis=1)

        def scores(r):
            return _dot(kb, qt_ref[0, 0, :, r * rc:(r + 1) * rc])

        def softmax(r, st):
            ls = slice(r * rc, (r + 1) * rc)
            if init:
                st = jnp.where(visible, st, NEG_INF)
                m_new = jnp.max(st, axis=0, keepdims=True)
                a = None
            else:
                m_prev = m_ref[:, ls]
                m_new = jnp.maximum(m_prev, jnp.max(st, axis=0, keepdims=True))
                a = jnp.exp2(m_prev - m_new)
            pt = jnp.exp2(st - m_new)
            m_ref[:, ls] = m_new
            return a, pt.astype(BF16)

        st_cur = scores(0)
        pend = None
        for s in range(n_rc + 1):
            st_next = scores(s + 1) if s + 1 < n_rc else None
            if pend is not None:
                a_prev, pt_prev = pend
                pv = _dot(ct, pt_prev)
            new_pend = softmax(s, st_cur) if s < n_rc else None
            if pend is not None:
                ls = slice((s - 1) * rc, s * rc)
                acc_ref[:, ls] = pv if init else a_prev * acc_ref[:, ls] + pv
            pend = new_pend
            st_cur = st_next

    key_chunk = lax.broadcasted_iota(jnp.int32, (tq, rc), 0) >> CHUNK_SHIFT
    q_chunk = (lax.broadcasted_iota(jnp.int32, (tq, rc), 1) & (tq - 1)) >> CHUNK_SHIFT
    kv_step(i, key_chunk <= q_chunk)

    def body(jj, carry):
        kv_step(4 * jj, None, 4)
        return carry

    n_quads = i // 4
    lax.fori_loop(0, n_quads, body, 0)

    @pl.when(i - 4 * n_quads >= 2)
    def _():
        kv_step(4 * n_quads, None, 2)

    @pl.when(i % 2 == 1)
    def _():
        kv_step(i - 1, None)

    for r in range(n_rc):
        ls = slice(r * rc, (r + 1) * rc)
        ot_ref[0, 0, :, ls] = (acc_ref[0:KV_LORA, ls] / acc_ref[KV_LORA:KV_LORA + 1, ls]).astype(BF16)


def _attn_prompt(qt, kcat, kt, *, rc):
    bsz, nq, _, rows = qt.shape
    tq = rows // B_HEADS
    t = kcat.shape[1]
    assert kt.shape[-1] == tq and tq & (tq - 1) == 0 and tq % CHUNK == 0 and rc % tq == 0
    kern = functools.partial(_attn_prompt_kernel, tq=tq, rc=rc)
    return pl.pallas_call(
        kern,
        grid=(bsz, nq),
        in_specs=[
            pl.BlockSpec((1, 1, KCAT, rows), lambda b, i: (b, i, 0, 0)),
            pl.BlockSpec((1, t, KCAT), lambda b, i: (b, 0, 0)),
            pl.BlockSpec((1, nq, VEXT, tq), lambda b, i: (b, 0, 0, 0)),
        ],
        out_specs=pl.BlockSpec((1, 1, KV_LORA, rows), lambda b, i: (b, i, 0, 0)),
        out_shape=jax.ShapeDtypeStruct((bsz, nq, KV_LORA, rows), BF16),
        scratch_shapes=[
            pltpu.VMEM((1, rows), F32),
            pltpu.VMEM((VEXT, rows), F32),
        ],
        compiler_params=pltpu.CompilerParams(
            dimension_semantics=("arbitrary", "arbitrary"), vmem_limit_bytes=VMEM_LIMIT),
        name="mla_attend_prompt",
    )(qt, kcat, kt)


def _mla_output_kernel(ot_ref, gate_ref, x_ref, w_uvt_ref, w_out_ref, lng_ref, lnb_ref, y_ref, o_ref, *, tq, nqb):
    for qb in range(nqb):
        rs = slice(qb * tq, (qb + 1) * tq)
        for h in range(B_HEADS):
            oh = _dot(w_uvt_ref[h], ot_ref[0, qb, :, h * tq:(h + 1) * tq]).T
            o_ref[rs, h * B_DV:(h + 1) * B_DV] = (oh * gate_ref[0, rs, h * B_DV:(h + 1) * B_DV]).astype(BF16)
        out = _dot(o_ref[rs, :], w_out_ref[...])
        y_ref[0, rs, :] = _layer_norm(ALPHA * x_ref[0, rs, :] + out, lng_ref[...], lnb_ref[...])


def _mla_output(ot, gate, x, w_uvt_h, w_out, ln_g, ln_b, *, nqb):
    bsz, nq, _, rows = ot.shape
    tq = rows // B_HEADS
    t = x.shape[1]
    tt = nqb * tq
    const2 = lambda b, i: (0, 0)
    kern = functools.partial(_mla_output_kernel, tq=tq, nqb=nqb)
    return pl.pallas_call(
        kern,
        grid=(bsz, nq // nqb),
        in_specs=[
            pl.BlockSpec((1, nqb, KV_LORA, rows), lambda b, i: (b, i, 0, 0)),
            pl.BlockSpec((1, tt, B_WIDTH), lambda b, i: (b, i, 0)),
            pl.BlockSpec((1, tt, D_MODEL), lambda b, i: (b, i, 0)),
            pl.BlockSpec(w_uvt_h.shape, lambda b, i: (0, 0, 0)),
            pl.BlockSpec(w_out.shape, const2),
            pl.BlockSpec((1, D_MODEL), const2),
            pl.BlockSpec((1, D_MODEL), const2),
        ],
        out_specs=pl.BlockSpec((1, tt, D_MODEL), lambda b, i: (b, i, 0)),
        out_shape=jax.ShapeDtypeStruct((bsz, t, D_MODEL), F32),
        scratch_shapes=[pltpu.VMEM((tt, B_WIDTH), BF16)],
        compiler_params=pltpu.CompilerParams(
            dimension_semantics=("arbitrary", "arbitrary"), vmem_limit_bytes=VMEM_LIMIT),
        name="mla_output_prompt",
    )(ot, gate, x, w_uvt_h, w_out, ln_g, ln_b)


def _attn_sample_kernel(q_ref, cc_ref, ckr_ref, kn_ref, gate_ref, x_ref, w_uv_ref, w_out_ref,
                        lng_ref, lnb_ref, y_ref, m_ref, l_ref, acc_ref, o_ref, *, tq, tk):
    rows = B_HEADS * tq
    past = cc_ref.shape[1]
    q = q_ref[0].reshape(rows, KCAT)
    q_lat = q[:, :KV_LORA]
    q3 = q[:, KV_LORA:].astype(F32)
    q_rope = (q3[:, :ROPE_DIM] + q3[:, ROPE_DIM:]).astype(BF16)
    m_ref[...] = jnp.full((rows, 1), NEG_INF, F32)
    l_ref[...] = jnp.zeros((rows, 1), F32)
    acc_ref[...] = jnp.zeros((rows, KV_LORA), F32)

    for j in range(past // tk):
        cb = cc_ref[0, j * tk:(j + 1) * tk, :].astype(BF16)
        krb = ckr_ref[0, j * tk:(j + 1) * tk, :].astype(BF16)
        s = (lax.dot_general(q_lat, cb, _NT, preferred_element_type=F32)
             + lax.dot_general(q_rope, krb, _NT, preferred_element_type=F32))
        _softmax_step(s, cb, m_ref, l_ref, acc_ref)

    kn = kn_ref[0]
    s = lax.dot_general(q, kn, _NT, preferred_element_type=F32)
    _softmax_step(s, kn[:, :KV_LORA], m_ref, l_ref, acc_ref)

    y_ref[0] = _attn_output(acc_ref, l_ref, o_ref, gate_ref[0], x_ref[0],
                            w_uv_ref, w_out_ref, lng_ref, lnb_ref, tq)


def _attn_sample(qcat, cache_c, cache_kr, kcat_new, gate, x, w_uv_h, w_out, ln_g, ln_b, *, tk):
    bsz, _, tq, _ = qcat.shape
    past = cache_c.shape[1]
    rows = B_HEADS * tq
    const2 = lambda b: (0, 0)
    per_b = lambda shape: pl.BlockSpec((1,) + shape, lambda b: (b,) + (0,) * len(shape))
    kern = functools.partial(_attn_sample_kernel, tq=tq, tk=tk)
    return pl.pallas_call(
        kern,
        grid=(bsz,),
        in_specs=[
            per_b((B_HEADS, tq, KCAT)),
            per_b((past, KV_LORA)),
            per_b((past, ROPE_DIM)),
            per_b((tq, KCAT)),
            per_b((tq, B_WIDTH)),
            per_b((tq, D_MODEL)),
            pl.BlockSpec(w_uv_h.shape, lambda b: (0, 0, 0)),
            pl.BlockSpec(w_out.shape, const2),
            pl.BlockSpec((1, D_MODEL), const2),
            pl.BlockSpec((1, D_MODEL), const2),
        ],
        out_specs=per_b((tq, D_MODEL)),
        out_shape=jax.ShapeDtypeStruct((bsz, tq, D_MODEL), F32),
        scratch_shapes=[
            pltpu.VMEM((rows, 1), F32),
            pltpu.VMEM((rows, 1), F32),
            pltpu.VMEM((rows, KV_LORA), F32),
            pltpu.VMEM((tq, B_WIDTH), BF16),
        ],
        compiler_params=pltpu.CompilerParams(
            dimension_semantics=("arbitrary",), vmem_limit_bytes=VMEM_LIMIT),
        name="mla_attend_sample",
    )(qcat, cache_c, cache_kr, kcat_new, gate, x, w_uv_h, w_out, ln_g, ln_b)


def _rot_cols(w):
    half = ROPE_DIM // 2
    return jnp.concatenate([-w[..., half:], w[..., :half]], axis=-1)


def _rope_tables(pos):
    half = ROPE_DIM // 2
    inv = jnp.power(ROPE_BASE, -jnp.arange(half, dtype=F32) / half)
    ang = pos.astype(F32)[:, None] * inv[None, :]
    cos, sin = jnp.cos(ang), jnp.sin(ang)
    return jnp.concatenate([cos] * 4, axis=1), jnp.concatenate([sin] * 4, axis=1)


def kernel(x_prompt, x_sample, state_hgrn, cache_ckv, cache_krope, w_in_a, lb_gamma, a_norm_g, w_out_a,
           w_dkv, kv_norm_g, w_uk, w_uv, w_in_b, q_norm_g, w_uq, w_out_b, ln_g, ln_b):
    assert N_A == 1 and DEPTH == 2
    past = cache_ckv.shape[1]
    row = lambda a: a.reshape(1, -1)

    w_in_a_b = w_in_a[0].astype(BF16)
    w_out_a_b = w_out_a[0].astype(BF16)
    kr_w = w_dkv[:, KV_LORA:]
    w_dkv_ext = jnp.concatenate(
        [w_dkv[:, :KV_LORA], kr_w, kr_w, _rot_cols(kr_w), _rot_cols(kr_w)], axis=1).astype(BF16)
    w_uq3 = w_uq[0].reshape(Q_LORA, B_HEADS, NOPE_DIM + ROPE_DIM)
    q_rope_w = w_uq3[:, :, NOPE_DIM:]
    w_uq_ext = jnp.concatenate([
        w_uq3[:, :, :NOPE_DIM].reshape(Q_LORA, B_HEADS * NOPE_DIM),
        q_rope_w.reshape(Q_LORA, B_HEADS * ROPE_DIM),
        _rot_cols(q_rope_w).reshape(Q_LORA, B_HEADS * ROPE_DIM)], axis=1).astype(BF16)
    w_uqt = jnp.concatenate([
        w_uq3[:, :, :NOPE_DIM].reshape(Q_LORA, B_HEADS * NOPE_DIM),
        q_rope_w.reshape(Q_LORA, B_HEADS * ROPE_DIM)], axis=1).T.astype(BF16)
    w_uk_h = jnp.transpose(w_uk, (1, 0, 2)).astype(BF16)
    w_uvt_h = jnp.transpose(w_uv, (1, 2, 0)).astype(BF16)
    w_in_b_b = w_in_b[0].astype(BF16)
    w_out_b_b = w_out_b[0].astype(BF16)

    outs = []
    for x, s0, pos0, prompt in ((x_prompt, None, 0, True), (x_sample, state_hgrn[0], past, False)):
        bsz, t, _ = x.shape
        if prompt:
            bblk, tt, chunk, sub, tm = 1, HGRN_STEP_ROWS, CHUNK, HGRN_SUB_ROWS, TOKEN_TILE
        else:
            bblk, tt, chunk, sub, tm = bsz, t, t, bsz * t, t
        x1, s_fin = _hgrn_layer(x, s0, w_in_a_b, lb_gamma, row(a_norm_g[0]), w_out_a_b,
                                row(ln_g[0]), row(ln_b[0]), layer=0, bblk=bblk, tt=tt, chunk=chunk, sub=sub)
        cos2, sin2 = _rope_tables(pos0 + jnp.arange(t, dtype=jnp.int32))
        tabs = (cos2, sin2, cos2.T, sin2.T) if prompt else (cos2, sin2)
        proj_outs = _mla_proj(
            x1, tabs, w_dkv_ext, row(kv_norm_g), w_in_b_b, row(q_norm_g[0]),
            w_uqt if prompt else w_uq_ext, w_uk_h,
            bblk=bblk, tt=tm, transposed=prompt, tag="prompt" if prompt else "sample")
        if prompt:
            ckv, krope, kcat, kt, qt, gate = proj_outs
            ot = _attn_prompt(qt, kcat, kt, rc=ATTN_ROW_CHUNK)
            y = _mla_output(ot, gate, x1, w_uvt_h, w_out_b_b, row(ln_g[1]), row(ln_b[1]),
                            nqb=OUTPUT_QUERY_BLOCKS)
        else:
            ckv, krope, kcat, qcat, gate = proj_outs
            y = _attn_sample(qcat, cache_ckv, cache_krope, kcat, gate, x1, w_uvt_h, w_out_b_b,
                             row(ln_g[1]), row(ln_b[1]), tk=SAMPLE_KEY_CHUNK)
        outs.append((y, s_fin[None], ckv, krope))
    (y_p, s_p, c_p, kr_p), (y_s, s_s, c_s, kr_s) = outs
    return (y_p, y_s, s_p, c_p, kr_p, s_s, c_s, kr_s)
```

```python
import functools

import jax
import jax.numpy as jnp
from jax import lax
from jax.experimental import pallas as pl
from jax.experimental.pallas import tpu as pltpu

F32 = jnp.float32
BF16 = jnp.bfloat16

D_MODEL = 1024
DEPTH = 2
CHUNK = 64
CHUNK_SHIFT = CHUNK.bit_length() - 1
N_A = DEPTH // 2
A_HEADS = 8
A_DK = 128
A_DV = 128
A_WIDTH = A_HEADS * A_DV
B_HEADS = 16
Q_LORA = 512
KV_LORA = 256
NOPE_DIM = 128
ROPE_DIM = 64
B_DV = 128
B_WIDTH = B_HEADS * B_DV
ROPE_BASE = 10000.0
EPS = 1e-6
NEG_INF = -1e30
ATTN_SCALE = (NOPE_DIM + ROPE_DIM) ** -0.5
LOG2_E = 1.4426950408889634
Q_SCALE = ATTN_SCALE * LOG2_E
ALPHA = (2 * DEPTH) ** 0.25

LANES = 128
BF16_SUBLANES = 16
KCAT = KV_LORA + 2 * ROPE_DIM
VEXT = KV_LORA + BF16_SUBLANES
VMEM_LIMIT = 56 * 1024 * 1024

HGRN_STEP_ROWS = 1024
HGRN_SUB_ROWS = 256
TOKEN_TILE = 256
ATTN_ROW_CHUNK = 512
SAMPLE_KEY_CHUNK = 1024

_NT = (((1,), (1,)), ((), ()))
_TN = (((0,), (0,)), ((), ()))


def _dot(a, b):
    return jnp.dot(a, b, preferred_element_type=F32)


def _layer_norm(x, g, b):
    mu = jnp.mean(x, axis=-1, keepdims=True)
    xc = x - mu
    var = jnp.mean(xc * xc, axis=-1, keepdims=True)
    return xc * lax.rsqrt(var + EPS) * g + b


def _rms_norm(x, g):
    return x * lax.rsqrt(jnp.mean(x * x, axis=-1, keepdims=True) + EPS) * g


def _sigmoid_pair(z):
    e = jnp.exp(-jnp.abs(z))
    r = 1.0 / (1.0 + e)
    er = e * r
    pos = z >= 0
    return jnp.where(pos, r, er), jnp.where(pos, er, r)


CUMSUM_TERMS = 3


def _split_bf16(x, n):
    terms = []
    for _ in range(n):
        t = x.astype(BF16)
        terms.append(t)
        x = x - t.astype(F32)
    return terms


def _hgrn_kernel(x_ref, s0_ref, w_in_ref, lbg_ref, ng_ref, w_out_ref, lng_ref, lnb_ref, tri_ref,
                 y_ref, sfin_ref, st_ref, o_ref, *, layer, bblk, tt, chunk, sub, zero_init):
    t = pl.program_id(1)
    rows = bblk * tt
    mid = chunk // 2

    @pl.when(t == 0)
    def _():
        for bb in range(bblk):
            for h in range(A_HEADS):
                if zero_init:
                    st_ref[bb, h] = jnp.zeros((A_DV, A_DK), F32)
                else:
                    st_ref[bb, h] = s0_ref[bb, h].T

    lbg = lbg_ref[...]
    e = jnp.exp(lbg - jnp.max(lbg, axis=0, keepdims=True))
    lb = jnp.sum(e[0:layer + 1], axis=0, keepdims=True) / jnp.sum(e, axis=0, keepdims=True)
    one_m_lb = 1.0 - lb
    ng = ng_ref[...]
    li = lax.broadcasted_iota(jnp.int32, (chunk, chunk), 0)
    si = lax.broadcasted_iota(jnp.int32, (chunk, chunk), 1)
    causal = li >= si

    def x_rows(s_idx):
        if bblk == 1:
            return x_ref[0, s_idx * sub:(s_idx + 1) * sub, :]
        return x_ref[...].reshape(rows, D_MODEL)[s_idx * sub:(s_idx + 1) * sub]

    n_sub = rows // sub
    n_chunks = sub // chunk
    n_parts = 4
    tri = tri_ref[...]
    heads = [slice(h * A_DK, (h + 1) * A_DK) for h in range(A_HEADS)]

    def project(s_idx, part):
        xs = x_rows(s_idx).astype(BF16)
        return _dot(xs, w_in_ref[:, part * A_WIDTH:(part + 1) * A_WIDTH])

    parts = [project(0, p) for p in range(n_parts)]
    for s_idx in range(n_sub):
        more = s_idx + 1 < n_sub
        next_parts = []
        chunks = range(n_chunks)
        rows_of = [slice(c * chunk, (c + 1) * chunk) for c in chunks]

        k_all, b_all = [], []
        for c in chunks:
            z = parts[1][rows_of[c]]
            sig_pos = 1.0 / (1.0 + jnp.exp(-z))
            sig_neg = 1.0 / (1.0 + jnp.exp(z))
            logf = jnp.log(lb + one_m_lb * sig_pos)
            k_all.append(one_m_lb * sig_neg)
            terms = _split_bf16(logf, CUMSUM_TERMS)
            b_all.append(_dot(tri, jnp.concatenate(terms, axis=0)))
        if more:
            next_parts.append(project(s_idx + 1, 0))

        q_ex, decay, v_all, sc_all, inc_all = [], [], [], [], []
        for c in chunks:
            bc = b_all[c]
            bmid = bc[mid:mid + 1]
            blast = bc[chunk - 1:chunk]
            qc = parts[0][rows_of[c]]
            q_mid = qc * jnp.exp(bc - bmid)
            k_mid = k_all[c] * jnp.exp(bmid - bc)
            q_in = q_mid.astype(BF16)
            k_in = k_mid.astype(BF16)
            k_dec = (k_mid * jnp.exp(blast - bmid)).astype(BF16)
            q_ex.append((q_mid * jnp.exp(bmid)).astype(BF16))
            decay.append(jnp.exp(blast))
            vc = parts[2][rows_of[c]].astype(BF16)
            v_all.append(vc)
            sc_all.append([lax.dot_general(q_in[:, hs], k_in[:, hs], _NT, preferred_element_type=F32)
                           for hs in heads])
            inc_all.append([lax.dot_general(vc[:, hs], k_dec[:, hs], _TN, preferred_element_type=F32)
                            for hs in heads])
        if more:
            next_parts.append(project(s_idx + 1, 1))

        st_in = []
        for c in chunks:
            bb = (s_idx * sub + c * chunk) // tt
            st_in.append([])
            for h, hs in enumerate(heads):
                st = st_ref[bb, h]
                st_in[c].append(st.T.astype(BF16))
                st_ref[bb, h] = st * decay[c][:, hs] + inc_all[c][h]
        if more:
            next_parts.append(project(s_idx + 1, 2))

        for c in chunks:
            r0 = s_idx * sub + c * chunk
            g = parts[3][rows_of[c]]
            silu_g = g / (1.0 + jnp.exp(-g))
            for h, hs in enumerate(heads):
                sc = jnp.where(causal, sc_all[c][h], 0.0).astype(BF16)
                o = _dot(jnp.concatenate([q_ex[c][:, hs], sc], axis=1),
                         jnp.concatenate([st_in[c][h], v_all[c][:, hs]], axis=0))
                o = _rms_norm(o, ng) * silu_g[:, hs]
                o_ref[r0:r0 + chunk, hs] = o.astype(BF16)
        if more:
            next_parts.append(project(s_idx + 1, 3))
        parts = next_parts

        ss = slice(s_idx * sub, (s_idx + 1) * sub)
        out = _dot(o_ref[ss, :], w_out_ref[...])
        y = _layer_norm(ALPHA * x_rows(s_idx) + out, lng_ref[...], lnb_ref[...])
        if n_sub == 1:
            y_ref[...] = y.reshape(bblk, tt, D_MODEL)
        else:
            y_ref[0, ss, :] = y

    @pl.when(t == pl.num_programs(1) - 1)
    def _():
        for bb in range(bblk):
            for h in range(A_HEADS):
                sfin_ref[bb, h] = st_ref[bb, h].T


def _hgrn_layer(x, s0, w_in, lb_gamma, norm_g, w_out, ln_g, ln_b, *, layer, bblk, tt, chunk, sub):
    bsz, t, _ = x.shape
    rows = bblk * tt
    zero_init = s0 is None
    if zero_init:
        s0 = jnp.zeros((bblk, A_HEADS, 8, LANES), F32)
        s0_spec = pl.BlockSpec((bblk, A_HEADS, 8, LANES), lambda b, i: (0, 0, 0, 0))
    else:
        s0_spec = pl.BlockSpec((bblk, A_HEADS, A_DK, A_DV), lambda b, i: (b, 0, 0, 0))
    assert rows % sub == 0 and sub % chunk == 0 and tt % chunk == 0
    r = jnp.arange(chunk)
    tri = (r[:, None] >= r[None, :]).astype(BF16)
    tri3 = jnp.concatenate([tri] * CUMSUM_TERMS, axis=1)
    const = lambda b, i: (0, 0)
    kern = functools.partial(_hgrn_kernel, layer=layer, bblk=bblk, tt=tt, chunk=chunk, sub=sub,
                             zero_init=zero_init)
    return pl.pallas_call(
        kern,
        grid=(bsz // bblk, t // tt),
        in_specs=[
            pl.BlockSpec((bblk, tt, D_MODEL), lambda b, i: (b, i, 0)),
            s0_spec,
            pl.BlockSpec((D_MODEL, 4 * A_WIDTH), const),
            pl.BlockSpec(lb_gamma.shape, const),
            pl.BlockSpec((1, A_DV), const),
            pl.BlockSpec((A_WIDTH, D_MODEL), const),
            pl.BlockSpec((1, D_MODEL), const),
            pl.BlockSpec((1, D_MODEL), const),
            pl.BlockSpec((chunk, CUMSUM_TERMS * chunk), const),
        ],
        out_specs=[
            pl.BlockSpec((bblk, tt, D_MODEL), lambda b, i: (b, i, 0)),
            pl.BlockSpec((bblk, A_HEADS, A_DK, A_DV), lambda b, i: (b, 0, 0, 0)),
        ],
        out_shape=[
            jax.ShapeDtypeStruct((bsz, t, D_MODEL), F32),
            jax.ShapeDtypeStruct((bsz, A_HEADS, A_DK, A_DV), F32),
        ],
        scratch_shapes=[
            pltpu.VMEM((bblk, A_HEADS, A_DV, A_DK), F32),
            pltpu.VMEM((rows, A_WIDTH), BF16),
        ],
        compiler_params=pltpu.CompilerParams(
            dimension_semantics=("arbitrary", "arbitrary"), vmem_limit_bytes=VMEM_LIMIT),
        name=f"hgrn_layer_{'prompt' if zero_init else 'sample'}",
    )(x, s0, w_in, lb_gamma, norm_g, w_out, ln_g, ln_b, tri3)


def _latent_kv(xb, cos2, sin2, w_dkv_ref, kvg_ref, ckv_ref, krope_ref, kcat_ref, bblk, tt):
    kv = _dot(xb, w_dkv_ref[...])
    c = _rms_norm(kv[:, :KV_LORA], kvg_ref[...])
    krd = kv[:, KV_LORA:KV_LORA + LANES] * cos2 + kv[:, KV_LORA + LANES:KV_LORA + 2 * LANES] * sin2
    ckv_ref[...] = c.reshape(bblk, tt, KV_LORA)
    krope_ref[...] = krd[:, :ROPE_DIM].reshape(bblk, tt, ROPE_DIM)
    kcat = jnp.concatenate([c, krd], axis=1)
    kcat_ref[...] = kcat.astype(BF16).reshape(bblk, tt, KCAT)
    return kcat


def _query_latent(xb, w_in_ref, qg_ref, gate_ref, bblk, tt):
    proj = _dot(xb, w_in_ref[...])
    cq = _rms_norm(proj[:, :Q_LORA], qg_ref[...])
    gt = proj[:, Q_LORA:]
    gate_ref[...] = (gt * _sigmoid_pair(gt)[0]).reshape(bblk, tt, B_WIDTH)
    return cq


def _mla_proj_rows_kernel(x_ref, cos_ref, sin_ref, w_dkv_ref, kvg_ref, w_in_ref, qg_ref, w_uq_ref, w_uk_ref,
                          ckv_ref, krope_ref, kcat_ref, q_ref, gate_ref, *, bblk, tt):
    rows = bblk * tt
    xb = x_ref[...].reshape(rows, D_MODEL).astype(BF16)
    cos2 = jnp.concatenate([cos_ref[...]] * bblk, axis=0)
    sin2 = jnp.concatenate([sin_ref[...]] * bblk, axis=0)
    _latent_kv(xb, cos2, sin2, w_dkv_ref, kvg_ref, ckv_ref, krope_ref, kcat_ref, bblk, tt)
    cq = _query_latent(xb, w_in_ref, qg_ref, gate_ref, bblk, tt)
    qall = _dot(cq.astype(BF16), w_uq_ref[...])
    nope_w = B_HEADS * NOPE_DIM
    pair_w = (B_HEADS // 2) * LANES
    lane = lax.broadcasted_iota(jnp.int32, (rows, LANES), 1)
    for j in range(B_HEADS // 2):
        raw = qall[:, nope_w + j * LANES:nope_w + (j + 1) * LANES]
        rot = qall[:, nope_w + pair_w + j * LANES:nope_w + pair_w + (j + 1) * LANES]
        rp = (raw * cos2 + rot * sin2) * Q_SCALE
        for e in range(2):
            h = 2 * j + e
            ql = lax.dot_general(qall[:, h * NOPE_DIM:(h + 1) * NOPE_DIM].astype(BF16), w_uk_ref[h], _NT,
                                 preferred_element_type=F32) * Q_SCALE
            keep = (lane < ROPE_DIM) if e == 0 else (lane >= ROPE_DIM)
            qc = jnp.concatenate([ql, jnp.where(keep, rp, 0.0)], axis=1).astype(BF16)
            for bb in range(bblk):
                q_ref[bb, h] = qc[bb * tt:(bb + 1) * tt]


def _mla_proj_cols_kernel(x_ref, cos_ref, sin_ref, cost_ref, sint_ref, w_dkv_ref, kvg_ref, w_in_ref, qg_ref,
                          w_uqt_ref, w_uk_ref, ckv_ref, krope_ref, kcat_ref, kt_ref, q_ref, gate_ref, *, tt):
    xb = x_ref[0].astype(BF16)
    kcat = _latent_kv(xb, cos_ref[...], sin_ref[...], w_dkv_ref, kvg_ref, ckv_ref, krope_ref, kcat_ref, 1, tt)
    c_t = kcat[:, :KV_LORA].T
    kt_ref[0, 0] = jnp.concatenate([c_t, jnp.ones((VEXT - KV_LORA, tt), F32)], axis=0).astype(BF16)
    cq = _query_latent(xb, w_in_ref, qg_ref, gate_ref, 1, tt)
    qall_t = _dot(w_uqt_ref[...], cq.T.astype(BF16))
    nope_w = B_HEADS * NOPE_DIM
    half = ROPE_DIM // 2
    cos_t = cost_ref[...]
    sin_t = sint_ref[...]
    zeros = jnp.zeros((ROPE_DIM, tt), F32)
    for j in range(B_HEADS // 2):
        raw = qall_t[nope_w + j * LANES:nope_w + (j + 1) * LANES]
        rot = jnp.concatenate([-raw[half:2 * half], raw[0:half], -raw[3 * half:4 * half], raw[2 * half:3 * half]],
                              axis=0)
        rp = (raw * cos_t + rot * sin_t) * Q_SCALE
        for e in range(2):
            h = 2 * j + e
            ql = _dot(w_uk_ref[h], qall_t[h * NOPE_DIM:(h + 1) * NOPE_DIM].astype(BF16)) * Q_SCALE
            rope_rows = [rp[:ROPE_DIM], zeros] if e == 0 else [zeros, rp[ROPE_DIM:]]
            q_ref[0, 0, :, h * tt:(h + 1) * tt] = jnp.concatenate([ql] + rope_rows, axis=0).astype(BF16)


def _mla_proj(x, rope_tabs, w_dkv_ext, kv_g, w_in, q_g, w_q, w_k, *, bblk, tt, transposed, tag):
    bsz, t, _ = x.shape
    const2 = lambda b, i: (0, 0)
    const3 = lambda b, i: (0, 0, 0)
    tok = lambda w: pl.BlockSpec((bblk, tt, w), lambda b, i: (b, i, 0))
    tab = pl.BlockSpec((tt, LANES), lambda b, i: (i, 0))
    in_specs = [tok(D_MODEL), tab, tab]
    out_specs = [tok(KV_LORA), tok(ROPE_DIM), tok(KCAT)]
    out_shape = [
        jax.ShapeDtypeStruct((bsz, t, KV_LORA), F32),
        jax.ShapeDtypeStruct((bsz, t, ROPE_DIM), F32),
        jax.ShapeDtypeStruct((bsz, t, KCAT), BF16),
    ]
    if transposed:
        assert bblk == 1
        tab_t = pl.BlockSpec((LANES, tt), lambda b, i: (0, i))
        in_specs += [tab_t, tab_t]
        out_specs += [pl.BlockSpec((1, 1, VEXT, tt), lambda b, i: (b, i, 0, 0)),
                      pl.BlockSpec((1, 1, KCAT, B_HEADS * tt), lambda b, i: (b, i, 0, 0))]
        out_shape += [jax.ShapeDtypeStruct((bsz, t // tt, VEXT, tt), BF16),
                      jax.ShapeDtypeStruct((bsz, t // tt, KCAT, B_HEADS * tt), BF16)]
        kern = functools.partial(_mla_proj_cols_kernel, tt=tt)
    else:
        out_specs += [pl.BlockSpec((bblk, B_HEADS, tt, KCAT), lambda b, i: (b, 0, i, 0))]
        out_shape += [jax.ShapeDtypeStruct((bsz, B_HEADS, t, KCAT), BF16)]
        kern = functools.partial(_mla_proj_rows_kernel, bblk=bblk, tt=tt)
    in_specs += [
        pl.BlockSpec(w_dkv_ext.shape, const2),
        pl.BlockSpec((1, KV_LORA), const2),
        pl.BlockSpec(w_in.shape, const2),
        pl.BlockSpec((1, Q_LORA), const2),
        pl.BlockSpec(w_q.shape, const2),
        pl.BlockSpec(w_k.shape, const3),
    ]
    out_specs += [tok(B_WIDTH)]
    out_shape += [jax.ShapeDtypeStruct((bsz, t, B_WIDTH), F32)]
    return pl.pallas_call(
        kern,
        grid=(bsz // bblk, t // tt),
        in_specs=in_specs,
        out_specs=out_specs,
        out_shape=out_shape,
        compiler_params=pltpu.CompilerParams(
            dimension_semantics=("arbitrary", "arbitrary"), vmem_limit_bytes=VMEM_LIMIT),
        name=f"mla_proj_{tag}",
    )(x, *rope_tabs, w_dkv_ext, kv_g, w_in, q_g, w_q, w_k)


def _softmax_step(s, v, m_ref, l_ref, acc_ref):
    m_prev = m_ref[...]
    m_new = jnp.maximum(m_prev, jnp.max(s, axis=1, keepdims=True))
    a = jnp.exp2(m_prev - m_new)
    p = jnp.exp2(s - m_new)
    l_ref[...] = a * l_ref[...] + jnp.sum(p, axis=1, keepdims=True)
    acc_ref[...] = a * acc_ref[...] + _dot(p.astype(BF16), v)
    m_ref[...] = m_new


def _attn_output(acc_ref, l_ref, o_ref, gate, x, w_uv_ref, w_out_ref, lng_ref, lnb_ref, tq):
    for h in range(B_HEADS):
        hs = slice(h * tq, (h + 1) * tq)
        o_lat = acc_ref[hs, :] / l_ref[hs, :]
        oh = lax.dot_general(o_lat.astype(BF16), w_uv_ref[h], _NT, preferred_element_type=F32)
        o_ref[:, h * B_DV:(h + 1) * B_DV] = (oh * gate[:, h * B_DV:(h + 1) * B_DV]).astype(BF16)
    out = _dot(o_ref[...], w_out_ref[...])
    return _layer_norm(ALPHA * x + out, lng_ref[...], lnb_ref[...])


def _attn_prompt_kernel(qt_ref, kcat_ref, kt_ref, gate_ref, x_ref, w_uvt_ref, w_out_ref, lng_ref, lnb_ref,
                        y_ref, m_ref, acc_ref, o_ref, *, tq, rc):
    i = pl.program_id(1)
    rows = B_HEADS * tq
    n_rc = rows // rc

    def kv_step(j, visible, nblk=1):
        init = visible is not None
        kb = kcat_ref[0, pl.ds(pl.multiple_of(j * tq, tq), nblk * tq), :]
        ct = jnp.concatenate([kt_ref[0, j + n] for n in range(nblk)], axis=1)

        def scores(r):
            return _dot(kb, qt_ref[0, 0, :, r * rc:(r + 1) * rc])

        def softmax(r, st):
            ls = slice(r * rc, (r + 1) * rc)
            if init:
                st = jnp.where(visible, st, NEG_INF)
                m_new = jnp.max(st, axis=0, keepdims=True)
                a = None
            else:
                m_prev = m_ref[:, ls]
                m_new = jnp.maximum(m_prev, jnp.max(st, axis=0, keepdims=True))
                a = jnp.exp2(m_prev - m_new)
            pt = jnp.exp2(st - m_new)
            m_ref[:, ls] = m_new
            return a, pt.astype(BF16)

        st_cur = scores(0)
        pend = None
        for s in range(n_rc + 1):
            st_next = scores(s + 1) if s + 1 < n_rc else None
            if pend is not None:
                a_prev, pt_prev = pend
                pv = _dot(ct, pt_prev)
            new_pend = softmax(s, st_cur) if s < n_rc else None
            if pend is not None:
                ls = slice((s - 1) * rc, s * rc)
                acc_ref[:, ls] = pv if init else a_prev * acc_ref[:, ls] + pv
            pend = new_pend
            st_cur = st_next

    key_chunk = lax.broadcasted_iota(jnp.int32, (tq, rc), 0) >> CHUNK_SHIFT
    q_chunk = (lax.broadcasted_iota(jnp.int32, (tq, rc), 1) & (tq - 1)) >> CHUNK_SHIFT
    kv_step(i, key_chunk <= q_chunk)

    def body(jj, carry):
        kv_step(4 * jj, None, 4)
        return carry

    n_quads = i // 4
    lax.fori_loop(0, n_quads, body, 0)

    @pl.when(i - 4 * n_quads >= 2)
    def _():
        kv_step(4 * n_quads, None, 2)

    @pl.when(i % 2 == 1)
    def _():
        kv_step(i - 1, None)

    for h in range(B_HEADS):
        hs = slice(h * tq, (h + 1) * tq)
        vs = slice(h * B_DV, (h + 1) * B_DV)
        o_lat = acc_ref[0:KV_LORA, hs] / acc_ref[KV_LORA:KV_LORA + 1, hs]
        oh = _dot(w_uvt_ref[h], o_lat.astype(BF16)).T
        o_ref[:, vs] = (oh * gate_ref[0, :, vs]).astype(BF16)
    out = _dot(o_ref[...], w_out_ref[...])
    y_ref[0] = _layer_norm(ALPHA * x_ref[0] + out, lng_ref[...], lnb_ref[...])


def _attn_prompt(qt, kcat, kt, gate, x, w_uvt_h, w_out, ln_g, ln_b, *, rc):
    bsz, nq, _, rows = qt.shape
    tq = rows // B_HEADS
    t = kcat.shape[1]
    assert kt.shape[-1] == tq and tq & (tq - 1) == 0 and tq % CHUNK == 0 and rc % tq == 0
    const2 = lambda b, i: (0, 0)
    kern = functools.partial(_attn_prompt_kernel, tq=tq, rc=rc)
    return pl.pallas_call(
        kern,
        grid=(bsz, nq),
        in_specs=[
            pl.BlockSpec((1, 1, KCAT, rows), lambda b, i: (b, i, 0, 0)),
            pl.BlockSpec((1, t, KCAT), lambda b, i: (b, 0, 0)),
            pl.BlockSpec((1, nq, VEXT, tq), lambda b, i: (b, 0, 0, 0)),
            pl.BlockSpec((1, tq, B_WIDTH), lambda b, i: (b, i, 0)),
            pl.BlockSpec((1, tq, D_MODEL), lambda b, i: (b, i, 0)),
            pl.BlockSpec(w_uvt_h.shape, lambda b, i: (0, 0, 0)),
            pl.BlockSpec(w_out.shape, const2),
            pl.BlockSpec((1, D_MODEL), const2),
            pl.BlockSpec((1, D_MODEL), const2),
        ],
        out_specs=pl.BlockSpec((1, tq, D_MODEL), lambda b, i: (b, i, 0)),
        out_shape=jax.ShapeDtypeStruct((bsz, t, D_MODEL), F32),
        scratch_shapes=[
            pltpu.VMEM((1, rows), F32),
            pltpu.VMEM((VEXT, rows), F32),
            pltpu.VMEM((tq, B_WIDTH), BF16),
        ],
        compiler_params=pltpu.CompilerParams(
            dimension_semantics=("arbitrary", "arbitrary"), vmem_limit_bytes=VMEM_LIMIT),
        name="mla_attend_prompt",
    )(qt, kcat, kt, gate, x, w_uvt_h, w_out, ln_g, ln_b)


def _attn_sample_kernel(q_ref, cc_ref, ckr_ref, kn_ref, gate_ref, x_ref, w_uv_ref, w_out_ref,
                        lng_ref, lnb_ref, y_ref, m_ref, l_ref, acc_ref, o_ref, *, tq, tk):
    rows = B_HEADS * tq
    past = cc_ref.shape[1]
    q = q_ref[0].reshape(rows, KCAT)
    q_lat = q[:, :KV_LORA]
    q3 = q[:, KV_LORA:].astype(F32)
    q_rope = (q3[:, :ROPE_DIM] + q3[:, ROPE_DIM:]).astype(BF16)
    m_ref[...] = jnp.full((rows, 1), NEG_INF, F32)
    l_ref[...] = jnp.zeros((rows, 1), F32)
    acc_ref[...] = jnp.zeros((rows, KV_LORA), F32)

    for j in range(past // tk):
        cb = cc_ref[0, j * tk:(j + 1) * tk, :].astype(BF16)
        krb = ckr_ref[0, j * tk:(j + 1) * tk, :].astype(BF16)
        s = (lax.dot_general(q_lat, cb, _NT, preferred_element_type=F32)
             + lax.dot_general(q_rope, krb, _NT, preferred_element_type=F32))
        _softmax_step(s, cb, m_ref, l_ref, acc_ref)

    kn = kn_ref[0]
    s = lax.dot_general(q, kn, _NT, preferred_element_type=F32)
    _softmax_step(s, kn[:, :KV_LORA], m_ref, l_ref, acc_ref)

    y_ref[0] = _attn_output(acc_ref, l_ref, o_ref, gate_ref[0], x_ref[0],
                            w_uv_ref, w_out_ref, lng_ref, lnb_ref, tq)


def _attn_sample(qcat, cache_c, cache_kr, kcat_new, gate, x, w_uv_h, w_out, ln_g, ln_b, *, tk):
    bsz, _, tq, _ = qcat.shape
    past = cache_c.shape[1]
    rows = B_HEADS * tq
    const2 = lambda b: (0, 0)
    per_b = lambda shape: pl.BlockSpec((1,) + shape, lambda b: (b,) + (0,) * len(shape))
    kern = functools.partial(_attn_sample_kernel, tq=tq, tk=tk)
    return pl.pallas_call(
        kern,
        grid=(bsz,),
        in_specs=[
            per_b((B_HEADS, tq, KCAT)),
            per_b((past, KV_LORA)),
            per_b((past, ROPE_DIM)),
            per_b((tq, KCAT)),
            per_b((tq, B_WIDTH)),
            per_b((tq, D_MODEL)),
            pl.BlockSpec(w_uv_h.shape, lambda b: (0, 0, 0)),
            pl.BlockSpec(w_out.shape, const2),
            pl.BlockSpec((1, D_MODEL), const2),
            pl.BlockSpec((1, D_MODEL), const2),
        ],
        out_specs=per_b((tq, D_MODEL)),
        out_shape=jax.ShapeDtypeStruct((bsz, tq, D_MODEL), F32),
        scratch_shapes=[
            pltpu.VMEM((rows, 1), F32),
            pltpu.VMEM((rows, 1), F32),
            pltpu.VMEM((rows, KV_LORA), F32),
            pltpu.VMEM((tq, B_WIDTH), BF16),
        ],
        compiler_params=pltpu.CompilerParams(
            dimension_semantics=("arbitrary",), vmem_limit_bytes=VMEM_LIMIT),
        name="mla_attend_sample",
    )(qcat, cache_c, cache_kr, kcat_new, gate, x, w_uv_h, w_out, ln_g, ln_b)


def _rot_cols(w):
    half = ROPE_DIM // 2
    return jnp.concatenate([-w[..., half:], w[..., :half]], axis=-1)


def _rope_tables(pos):
    half = ROPE_DIM // 2
    inv = jnp.power(ROPE_BASE, -jnp.arange(half, dtype=F32) / half)
    ang = pos.astype(F32)[:, None] * inv[None, :]
    cos, sin = jnp.cos(ang), jnp.sin(ang)
    return jnp.concatenate([cos] * 4, axis=1), jnp.concatenate([sin] * 4, axis=1)


def kernel(x_prompt, x_sample, state_hgrn, cache_ckv, cache_krope, w_in_a, lb_gamma, a_norm_g, w_out_a,
           w_dkv, kv_norm_g, w_uk, w_uv, w_in_b, q_norm_g, w_uq, w_out_b, ln_g, ln_b):
    assert N_A == 1 and DEPTH == 2
    past = cache_ckv.shape[1]
    row = lambda a: a.reshape(1, -1)

    w_in_a_b = w_in_a[0].astype(BF16)
    w_out_a_b = w_out_a[0].astype(BF16)
    kr_w = w_dkv[:, KV_LORA:]
    w_dkv_ext = jnp.concatenate(
        [w_dkv[:, :KV_LORA], kr_w, kr_w, _rot_cols(kr_w), _rot_cols(kr_w)], axis=1).astype(BF16)
    w_uq3 = w_uq[0].reshape(Q_LORA, B_HEADS, NOPE_DIM + ROPE_DIM)
    q_rope_w = w_uq3[:, :, NOPE_DIM:]
    w_uq_ext = jnp.concatenate([
        w_uq3[:, :, :NOPE_DIM].reshape(Q_LORA, B_HEADS * NOPE_DIM),
        q_rope_w.reshape(Q_LORA, B_HEADS * ROPE_DIM),
        _rot_cols(q_rope_w).reshape(Q_LORA, B_HEADS * ROPE_DIM)], axis=1).astype(BF16)
    w_uqt = jnp.concatenate([
        w_uq3[:, :, :NOPE_DIM].reshape(Q_LORA, B_HEADS * NOPE_DIM),
        q_rope_w.reshape(Q_LORA, B_HEADS * ROPE_DIM)], axis=1).T.astype(BF16)
    w_uk_h = jnp.transpose(w_uk, (1, 0, 2)).astype(BF16)
    w_uvt_h = jnp.transpose(w_uv, (1, 2, 0)).astype(BF16)
    w_in_b_b = w_in_b[0].astype(BF16)
    w_out_b_b = w_out_b[0].astype(BF16)

    outs = []
    for x, s0, pos0, prompt in ((x_prompt, None, 0, True), (x_sample, state_hgrn[0], past, False)):
        bsz, t, _ = x.shape
        if prompt:
            bblk, tt, chunk, sub, tm = 1, HGRN_STEP_ROWS, CHUNK, HGRN_SUB_ROWS, TOKEN_TILE
        else:
            bblk, tt, chunk, sub, tm = bsz, t, t, bsz * t, t
        x1, s_fin = _hgrn_layer(x, s0, w_in_a_b, lb_gamma, row(a_norm_g[0]), w_out_a_b,
                                row(ln_g[0]), row(ln_b[0]), layer=0, bblk=bblk, tt=tt, chunk=chunk, sub=sub)
        cos2, sin2 = _rope_tables(pos0 + jnp.arange(t, dtype=jnp.int32))
        tabs = (cos2, sin2, cos2.T, sin2.T) if prompt else (cos2, sin2)
        proj_outs = _mla_proj(
            x1, tabs, w_dkv_ext, row(kv_norm_g), w_in_b_b, row(q_norm_g[0]),
            w_uqt if prompt else w_uq_ext, w_uk_h,
            bblk=bblk, tt=tm, transposed=prompt, tag="prompt" if prompt else "sample")
        if prompt:
            ckv, krope, kcat, kt, qt, gate = proj_outs
            y = _attn_prompt(qt, kcat, kt, gate, x1, w_uvt_h, w_out_b_b, row(ln_g[1]), row(ln_b[1]),
                             rc=ATTN_ROW_CHUNK)
        else:
            ckv, krope, kcat, qcat, gate = proj_outs
            y = _attn_sample(qcat, cache_ckv, cache_krope, kcat, gate, x1, w_uvt_h, w_out_b_b,
                             row(ln_g[1]), row(ln_b[1]), tk=SAMPLE_KEY_CHUNK)
        outs.append((y, s_fin[None], ckv, krope))
    (y_p, s_p, c_p, kr_p), (y_s, s_s, c_s, kr_s) = outs
    return (y_p, y_s, s_p, c_p, kr_p, s_s, c_s, kr_s)
```

```python
import functools

import jax
import jax.numpy as jnp
from jax import lax
from jax.experimental import pallas as pl
from jax.experimental.pallas import tpu as pltpu

F32 = jnp.float32
BF16 = jnp.bfloat16

D_MODEL = 1024
DEPTH = 2
CHUNK = 64
CHUNK_SHIFT = CHUNK.bit_length() - 1
N_A = DEPTH // 2
A_HEADS = 8
A_DK = 128
A_DV = 128
A_WIDTH = A_HEADS * A_DV
B_HEADS = 16
Q_LORA = 512
KV_LORA = 256
NOPE_DIM = 128
ROPE_DIM = 64
B_DV = 128
B_WIDTH = B_HEADS * B_DV
ROPE_BASE = 10000.0
EPS = 1e-6
NEG_INF = -1e30
ATTN_SCALE = (NOPE_DIM + ROPE_DIM) ** -0.5
LOG2_E = 1.4426950408889634
Q_SCALE = ATTN_SCALE * LOG2_E
ALPHA = (2 * DEPTH) ** 0.25

LANES = 128
BF16_SUBLANES = 16
KCAT = KV_LORA + 2 * ROPE_DIM
VEXT = KV_LORA + BF16_SUBLANES
VMEM_LIMIT = 56 * 1024 * 1024

HGRN_STEP_ROWS = 1024
HGRN_SUB_ROWS = 256
TOKEN_TILE = 256
ATTN_ROW_CHUNK = 512
SAMPLE_KEY_CHUNK = 1024

_NT = (((1,), (1,)), ((), ()))
_TN = (((0,), (0,)), ((), ()))


def _resident(shape, index_map):
    return pl.BlockSpec(shape, index_map, pipeline_mode=pl.Buffered(1))


def _dot(a, b):
    return jnp.dot(a, b, preferred_element_type=F32)


def _layer_norm(x, g, b):
    mu = jnp.mean(x, axis=-1, keepdims=True)
    xc = x - mu
    var = jnp.mean(xc * xc, axis=-1, keepdims=True)
    return xc * lax.rsqrt(var + EPS) * g + b


def _rms_norm(x, g):
    return x * lax.rsqrt(jnp.mean(x * x, axis=-1, keepdims=True) + EPS) * g


def _sigmoid_pair(z):
    e = jnp.exp(-jnp.abs(z))
    r = 1.0 / (1.0 + e)
    er = e * r
    pos = z >= 0
    return jnp.where(pos, r, er), jnp.where(pos, er, r)


CUMSUM_TERMS = 3


def _split_bf16(x, n):
    terms = []
    for _ in range(n):
        t = x.astype(BF16)
        terms.append(t)
        x = x - t.astype(F32)
    return terms


def _hgrn_kernel(x_ref, s0_ref, w_in_ref, lbg_ref, ng_ref, w_out_ref, lng_ref, lnb_ref, tri_ref,
                 y_ref, sfin_ref, st_ref, o_ref, *, layer, bblk, tt, chunk, sub, zero_init):
    t = pl.program_id(1)
    rows = bblk * tt
    mid = chunk // 2

    @pl.when(t == 0)
    def _():
        for bb in range(bblk):
            for h in range(A_HEADS):
                if zero_init:
                    st_ref[bb, h] = jnp.zeros((A_DV, A_DK), F32)
                else:
                    st_ref[bb, h] = s0_ref[bb, h].T

    lbg = lbg_ref[...]
    e = jnp.exp(lbg - jnp.max(lbg, axis=0, keepdims=True))
    lb = jnp.sum(e[0:layer + 1], axis=0, keepdims=True) / jnp.sum(e, axis=0, keepdims=True)
    one_m_lb = 1.0 - lb
    ng = ng_ref[...]
    li = lax.broadcasted_iota(jnp.int32, (chunk, chunk), 0)
    si = lax.broadcasted_iota(jnp.int32, (chunk, chunk), 1)
    causal = li >= si

    x = x_ref[...].reshape(rows, D_MODEL)
    n_sub = rows // sub
    n_chunks = sub // chunk
    n_parts = 4
    tri = tri_ref[...]
    heads = [slice(h * A_DK, (h + 1) * A_DK) for h in range(A_HEADS)]

    def project(s_idx, part):
        xs = x[s_idx * sub:(s_idx + 1) * sub].astype(BF16)
        return _dot(xs, w_in_ref[:, part * A_WIDTH:(part + 1) * A_WIDTH])

    parts = [project(0, p) for p in range(n_parts)]
    for s_idx in range(n_sub):
        more = s_idx + 1 < n_sub
        next_parts = []
        chunks = range(n_chunks)
        rows_of = [slice(c * chunk, (c + 1) * chunk) for c in chunks]

        k_all, b_all = [], []
        for c in chunks:
            z = parts[1][rows_of[c]]
            sig_pos = 1.0 / (1.0 + jnp.exp(-z))
            sig_neg = 1.0 / (1.0 + jnp.exp(z))
            logf = jnp.log(lb + one_m_lb * sig_pos)
            k_all.append(one_m_lb * sig_neg)
            terms = _split_bf16(logf, CUMSUM_TERMS)
            b_all.append(_dot(tri, jnp.concatenate(terms, axis=0)))
        if more:
            next_parts.append(project(s_idx + 1, 0))

        q_ex, decay, v_all, sc_all, inc_all = [], [], [], [], []
        for c in chunks:
            bc = b_all[c]
            bmid = bc[mid:mid + 1]
            blast = bc[chunk - 1:chunk]
            qc = parts[0][rows_of[c]]
            q_mid = qc * jnp.exp(bc - bmid)
            k_mid = k_all[c] * jnp.exp(bmid - bc)
            q_in = q_mid.astype(BF16)
            k_in = k_mid.astype(BF16)
            k_dec = (k_mid * jnp.exp(blast - bmid)).astype(BF16)
            q_ex.append((q_mid * jnp.exp(bmid)).astype(BF16))
            decay.append(jnp.exp(blast))
            vc = parts[2][rows_of[c]].astype(BF16)
            v_all.append(vc)
            sc_all.append([lax.dot_general(q_in[:, hs], k_in[:, hs], _NT, preferred_element_type=F32)
                           for hs in heads])
            inc_all.append([lax.dot_general(vc[:, hs], k_dec[:, hs], _TN, preferred_element_type=F32)
                            for hs in heads])
        if more:
            next_parts.append(project(s_idx + 1, 1))

        st_in = []
        for c in chunks:
            bb = (s_idx * sub + c * chunk) // tt
            st_in.append([])
            for h, hs in enumerate(heads):
                st = st_ref[bb, h]
                st_in[c].append(st.T.astype(BF16))
                st_ref[bb, h] = st * decay[c][:, hs] + inc_all[c][h]
        if more:
            next_parts.append(project(s_idx + 1, 2))

        for c in chunks:
            r0 = s_idx * sub + c * chunk
            g = parts[3][rows_of[c]]
            silu_g = g / (1.0 + jnp.exp(-g))
            for h, hs in enumerate(heads):
                sc = jnp.where(causal, sc_all[c][h], 0.0).astype(BF16)
                o = _dot(jnp.concatenate([q_ex[c][:, hs], sc], axis=1),
                         jnp.concatenate([st_in[c][h], v_all[c][:, hs]], axis=0))
                o = _rms_norm(o, ng) * silu_g[:, hs]
                o_ref[r0:r0 + chunk, hs] = o.astype(BF16)
        if more:
            next_parts.append(project(s_idx + 1, 3))
        parts = next_parts

        ss = slice(s_idx * sub, (s_idx + 1) * sub)
        out = _dot(o_ref[ss, :], w_out_ref[...])
        y = _layer_norm(ALPHA * x[ss] + out, lng_ref[...], lnb_ref[...])
        if n_sub == 1:
            y_ref[...] = y.reshape(bblk, tt, D_MODEL)
        else:
            y_ref[0, ss, :] = y

    @pl.when(t == pl.num_programs(1) - 1)
    def _():
        for bb in range(bblk):
            for h in range(A_HEADS):
                sfin_ref[bb, h] = st_ref[bb, h].T


def _hgrn_layer(x, s0, w_in, lb_gamma, norm_g, w_out, ln_g, ln_b, *, layer, bblk, tt, chunk, sub):
    bsz, t, _ = x.shape
    rows = bblk * tt
    zero_init = s0 is None
    if zero_init:
        s0 = jnp.zeros((bblk, A_HEADS, 8, LANES), F32)
        s0_spec = pl.BlockSpec((bblk, A_HEADS, 8, LANES), lambda b, i: (0, 0, 0, 0))
    else:
        s0_spec = pl.BlockSpec((bblk, A_HEADS, A_DK, A_DV), lambda b, i: (b, 0, 0, 0))
    assert rows % sub == 0 and sub % chunk == 0 and tt % chunk == 0
    r = jnp.arange(chunk)
    tri = (r[:, None] >= r[None, :]).astype(BF16)
    tri3 = jnp.concatenate([tri] * CUMSUM_TERMS, axis=1)
    const = lambda b, i: (0, 0)
    kern = functools.partial(_hgrn_kernel, layer=layer, bblk=bblk, tt=tt, chunk=chunk, sub=sub,
                             zero_init=zero_init)
    return pl.pallas_call(
        kern,
        grid=(bsz // bblk, t // tt),
        in_specs=[
            pl.BlockSpec((bblk, tt, D_MODEL), lambda b, i: (b, i, 0)),
            s0_spec,
            _resident((D_MODEL, 4 * A_WIDTH), const),
            pl.BlockSpec(lb_gamma.shape, const),
            pl.BlockSpec((1, A_DV), const),
            _resident((A_WIDTH, D_MODEL), const),
            pl.BlockSpec((1, D_MODEL), const),
            pl.BlockSpec((1, D_MODEL), const),
            pl.BlockSpec((chunk, CUMSUM_TERMS * chunk), const),
        ],
        out_specs=[
            pl.BlockSpec((bblk, tt, D_MODEL), lambda b, i: (b, i, 0)),
            pl.BlockSpec((bblk, A_HEADS, A_DK, A_DV), lambda b, i: (b, 0, 0, 0)),
        ],
        out_shape=[
            jax.ShapeDtypeStruct((bsz, t, D_MODEL), F32),
            jax.ShapeDtypeStruct((bsz, A_HEADS, A_DK, A_DV), F32),
        ],
        scratch_shapes=[
            pltpu.VMEM((bblk, A_HEADS, A_DV, A_DK), F32),
            pltpu.VMEM((rows, A_WIDTH), BF16),
        ],
        compiler_params=pltpu.CompilerParams(
            dimension_semantics=("arbitrary", "arbitrary"), vmem_limit_bytes=VMEM_LIMIT),
        name=f"hgrn_layer_{'prompt' if zero_init else 'sample'}",
    )(x, s0, w_in, lb_gamma, norm_g, w_out, ln_g, ln_b, tri3)


def _latent_kv(xb, cos2, sin2, w_dkv_ref, kvg_ref, ckv_ref, krope_ref, kcat_ref, bblk, tt):
    kv = _dot(xb, w_dkv_ref[...])
    c = _rms_norm(kv[:, :KV_LORA], kvg_ref[...])
    krd = kv[:, KV_LORA:KV_LORA + LANES] * cos2 + kv[:, KV_LORA + LANES:KV_LORA + 2 * LANES] * sin2
    ckv_ref[...] = c.reshape(bblk, tt, KV_LORA)
    krope_ref[...] = krd[:, :ROPE_DIM].reshape(bblk, tt, ROPE_DIM)
    kcat = jnp.concatenate([c, krd], axis=1)
    kcat_ref[...] = kcat.astype(BF16).reshape(bblk, tt, KCAT)
    return kcat


def _query_latent(xb, w_in_ref, qg_ref, gate_ref, bblk, tt):
    proj = _dot(xb, w_in_ref[...])
    cq = _rms_norm(proj[:, :Q_LORA], qg_ref[...])
    gt = proj[:, Q_LORA:]
    gate_ref[...] = (gt * _sigmoid_pair(gt)[0]).reshape(bblk, tt, B_WIDTH)
    return cq


def _mla_proj_rows_kernel(x_ref, cos_ref, sin_ref, w_dkv_ref, kvg_ref, w_in_ref, qg_ref, w_uq_ref, w_uk_ref,
                          ckv_ref, krope_ref, kcat_ref, q_ref, gate_ref, *, bblk, tt):
    rows = bblk * tt
    xb = x_ref[...].reshape(rows, D_MODEL).astype(BF16)
    cos2 = jnp.concatenate([cos_ref[...]] * bblk, axis=0)
    sin2 = jnp.concatenate([sin_ref[...]] * bblk, axis=0)
    _latent_kv(xb, cos2, sin2, w_dkv_ref, kvg_ref, ckv_ref, krope_ref, kcat_ref, bblk, tt)
    cq = _query_latent(xb, w_in_ref, qg_ref, gate_ref, bblk, tt)
    qall = _dot(cq.astype(BF16), w_uq_ref[...])
    nope_w = B_HEADS * NOPE_DIM
    pair_w = (B_HEADS // 2) * LANES
    lane = lax.broadcasted_iota(jnp.int32, (rows, LANES), 1)
    for j in range(B_HEADS // 2):
        raw = qall[:, nope_w + j * LANES:nope_w + (j + 1) * LANES]
        rot = qall[:, nope_w + pair_w + j * LANES:nope_w + pair_w + (j + 1) * LANES]
        rp = (raw * cos2 + rot * sin2) * Q_SCALE
        for e in range(2):
            h = 2 * j + e
            ql = lax.dot_general(qall[:, h * NOPE_DIM:(h + 1) * NOPE_DIM].astype(BF16), w_uk_ref[h], _NT,
                                 preferred_element_type=F32) * Q_SCALE
            keep = (lane < ROPE_DIM) if e == 0 else (lane >= ROPE_DIM)
            qc = jnp.concatenate([ql, jnp.where(keep, rp, 0.0)], axis=1).astype(BF16)
            for bb in range(bblk):
                q_ref[bb, h] = qc[bb * tt:(bb + 1) * tt]


def _mla_proj_cols_kernel(x_ref, cos_ref, sin_ref, cost_ref, sint_ref, w_dkv_ref, kvg_ref, w_in_ref, qg_ref,
                          w_uqt_ref, w_uk_ref, ckv_ref, krope_ref, kcat_ref, kt_ref, q_ref, gate_ref, *, tt):
    xb = x_ref[0].astype(BF16)
    kcat = _latent_kv(xb, cos_ref[...], sin_ref[...], w_dkv_ref, kvg_ref, ckv_ref, krope_ref, kcat_ref, 1, tt)
    c_t = kcat[:, :KV_LORA].T
    kt_ref[0, 0] = jnp.concatenate([c_t, jnp.ones((VEXT - KV_LORA, tt), F32)], axis=0).astype(BF16)
    cq = _query_latent(xb, w_in_ref, qg_ref, gate_ref, 1, tt)
    qall_t = _dot(w_uqt_ref[...], cq.T.astype(BF16))
    nope_w = B_HEADS * NOPE_DIM
    half = ROPE_DIM // 2
    cos_t = cost_ref[...]
    sin_t = sint_ref[...]
    zeros = jnp.zeros((ROPE_DIM, tt), F32)
    for j in range(B_HEADS // 2):
        raw = qall_t[nope_w + j * LANES:nope_w + (j + 1) * LANES]
        rot = jnp.concatenate([-raw[half:2 * half], raw[0:half], -raw[3 * half:4 * half], raw[2 * half:3 * half]],
                              axis=0)
        rp = (raw * cos_t + rot * sin_t) * Q_SCALE
        for e in range(2):
            h = 2 * j + e
            ql = _dot(w_uk_ref[h], qall_t[h * NOPE_DIM:(h + 1) * NOPE_DIM].astype(BF16)) * Q_SCALE
            rope_rows = [rp[:ROPE_DIM], zeros] if e == 0 else [zeros, rp[ROPE_DIM:]]
            q_ref[0, 0, :, h * tt:(h + 1) * tt] = jnp.concatenate([ql] + rope_rows, axis=0).astype(BF16)


def _mla_proj(x, rope_tabs, w_dkv_ext, kv_g, w_in, q_g, w_q, w_k, *, bblk, tt, transposed, tag):
    bsz, t, _ = x.shape
    const2 = lambda b, i: (0, 0)
    const3 = lambda b, i: (0, 0, 0)
    tok = lambda w: pl.BlockSpec((bblk, tt, w), lambda b, i: (b, i, 0))
    tab = pl.BlockSpec((tt, LANES), lambda b, i: (i, 0))
    in_specs = [tok(D_MODEL), tab, tab]
    out_specs = [tok(KV_LORA), tok(ROPE_DIM), tok(KCAT)]
    out_shape = [
        jax.ShapeDtypeStruct((bsz, t, KV_LORA), F32),
        jax.ShapeDtypeStruct((bsz, t, ROPE_DIM), F32),
        jax.ShapeDtypeStruct((bsz, t, KCAT), BF16),
    ]
    if transposed:
        assert bblk == 1
        tab_t = pl.BlockSpec((LANES, tt), lambda b, i: (0, i))
        in_specs += [tab_t, tab_t]
        out_specs += [pl.BlockSpec((1, 1, VEXT, tt), lambda b, i: (b, i, 0, 0)),
                      pl.BlockSpec((1, 1, KCAT, B_HEADS * tt), lambda b, i: (b, i, 0, 0))]
        out_shape += [jax.ShapeDtypeStruct((bsz, t // tt, VEXT, tt), BF16),
                      jax.ShapeDtypeStruct((bsz, t // tt, KCAT, B_HEADS * tt), BF16)]
        kern = functools.partial(_mla_proj_cols_kernel, tt=tt)
    else:
        out_specs += [pl.BlockSpec((bblk, B_HEADS, tt, KCAT), lambda b, i: (b, 0, i, 0))]
        out_shape += [jax.ShapeDtypeStruct((bsz, B_HEADS, t, KCAT), BF16)]
        kern = functools.partial(_mla_proj_rows_kernel, bblk=bblk, tt=tt)
    in_specs += [
        _resident(w_dkv_ext.shape, const2),
        pl.BlockSpec((1, KV_LORA), const2),
        _resident(w_in.shape, const2),
        pl.BlockSpec((1, Q_LORA), const2),
        _resident(w_q.shape, const2),
        _resident(w_k.shape, const3),
    ]
    out_specs += [tok(B_WIDTH)]
    out_shape += [jax.ShapeDtypeStruct((bsz, t, B_WIDTH), F32)]
    return pl.pallas_call(
        kern,
        grid=(bsz // bblk, t // tt),
        in_specs=in_specs,
        out_specs=out_specs,
        out_shape=out_shape,
        compiler_params=pltpu.CompilerParams(
            dimension_semantics=("arbitrary", "arbitrary"), vmem_limit_bytes=VMEM_LIMIT),
        name=f"mla_proj_{tag}",
    )(x, *rope_tabs, w_dkv_ext, kv_g, w_in, q_g, w_q, w_k)


def _softmax_step(s, v, m_ref, l_ref, acc_ref):
    m_prev = m_ref[...]
    m_new = jnp.maximum(m_prev, jnp.max(s, axis=1, keepdims=True))
    a = jnp.exp2(m_prev - m_new)
    p = jnp.exp2(s - m_new)
    l_ref[...] = a * l_ref[...] + jnp.sum(p, axis=1, keepdims=True)
    acc_ref[...] = a * acc_ref[...] + _dot(p.astype(BF16), v)
    m_ref[...] = m_new


def _attn_output(acc_ref, l_ref, o_ref, gate, x, w_uv_ref, w_out_ref, lng_ref, lnb_ref, tq):
    for h in range(B_HEADS):
        hs = slice(h * tq, (h + 1) * tq)
        o_lat = acc_ref[hs, :] / l_ref[hs, :]
        oh = lax.dot_general(o_lat.astype(BF16), w_uv_ref[h], _NT, preferred_element_type=F32)
        o_ref[:, h * B_DV:(h + 1) * B_DV] = (oh * gate[:, h * B_DV:(h + 1) * B_DV]).astype(BF16)
    out = _dot(o_ref[...], w_out_ref[...])
    return _layer_norm(ALPHA * x + out, lng_ref[...], lnb_ref[...])


def _attn_prompt_kernel(qt_ref, kcat_ref, kt_ref, gate_ref, x_ref, w_uvt_ref, w_out_ref, lng_ref, lnb_ref,
                        y_ref, m_ref, acc_ref, o_ref, *, tq, rc):
    i = pl.program_id(1)
    rows = B_HEADS * tq
    n_rc = rows // rc

    def kv_step(j, visible, nblk=1):
        init = visible is not None
        kb = kcat_ref[0, pl.ds(pl.multiple_of(j * tq, tq), nblk * tq), :]
        ct = jnp.concatenate([kt_ref[0, j + n] for n in range(nblk)], axis=1)

        def scores(r):
            return _dot(kb, qt_ref[0, 0, :, r * rc:(r + 1) * rc])

        def softmax(r, st):
            ls = slice(r * rc, (r + 1) * rc)
            if init:
                st = jnp.where(visible, st, NEG_INF)
                m_new = jnp.max(st, axis=0, keepdims=True)
                a = None
            else:
                m_prev = m_ref[:, ls]
                m_new = jnp.maximum(m_prev, jnp.max(st, axis=0, keepdims=True))
                a = jnp.exp2(m_prev - m_new)
            pt = jnp.exp2(st - m_new)
            m_ref[:, ls] = m_new
            return a, pt.astype(BF16)

        st_cur = scores(0)
        pend = None
        for s in range(n_rc + 1):
            st_next = scores(s + 1) if s + 1 < n_rc else None
            if pend is not None:
                a_prev, pt_prev = pend
                pv = _dot(ct, pt_prev)
            new_pend = softmax(s, st_cur) if s < n_rc else None
            if pend is not None:
                ls = slice((s - 1) * rc, s * rc)
                acc_ref[:, ls] = pv if init else a_prev * acc_ref[:, ls] + pv
            pend = new_pend
            st_cur = st_next

    key_chunk = lax.broadcasted_iota(jnp.int32, (tq, rc), 0) >> CHUNK_SHIFT
    q_chunk = (lax.broadcasted_iota(jnp.int32, (tq, rc), 1) & (tq - 1)) >> CHUNK_SHIFT
    kv_step(i, key_chunk <= q_chunk)

    def body(jj, carry):
        kv_step(4 * jj, None, 4)
        return carry

    n_quads = i // 4
    lax.fori_loop(0, n_quads, body, 0)

    @pl.when(i - 4 * n_quads >= 2)
    def _():
        kv_step(4 * n_quads, None, 2)

    @pl.when(i % 2 == 1)
    def _():
        kv_step(i - 1, None)

    gate = gate_ref[0]
    for h in range(B_HEADS):
        hs = slice(h * tq, (h + 1) * tq)
        o_lat = acc_ref[0:KV_LORA, hs] / acc_ref[KV_LORA:KV_LORA + 1, hs]
        oh = _dot(w_uvt_ref[h], o_lat.astype(BF16)).T
        o_ref[:, h * B_DV:(h + 1) * B_DV] = (oh * gate[:, h * B_DV:(h + 1) * B_DV]).astype(BF16)
    out = _dot(o_ref[...], w_out_ref[...])
    y_ref[0] = _layer_norm(ALPHA * x_ref[0] + out, lng_ref[...], lnb_ref[...])


def _attn_prompt(qt, kcat, kt, gate, x, w_uvt_h, w_out, ln_g, ln_b, *, rc):
    bsz, nq, _, rows = qt.shape
    tq = rows // B_HEADS
    t = kcat.shape[1]
    assert kt.shape[-1] == tq and tq & (tq - 1) == 0 and tq % CHUNK == 0 and rc % tq == 0
    const2 = lambda b, i: (0, 0)
    kern = functools.partial(_attn_prompt_kernel, tq=tq, rc=rc)
    return pl.pallas_call(
        kern,
        grid=(bsz, nq),
        in_specs=[
            pl.BlockSpec((1, 1, KCAT, rows), lambda b, i: (b, i, 0, 0)),
            pl.BlockSpec((1, t, KCAT), lambda b, i: (b, 0, 0)),
            pl.BlockSpec((1, nq, VEXT, tq), lambda b, i: (b, 0, 0, 0)),
            pl.BlockSpec((1, tq, B_WIDTH), lambda b, i: (b, i, 0)),
            pl.BlockSpec((1, tq, D_MODEL), lambda b, i: (b, i, 0)),
            _resident(w_uvt_h.shape, lambda b, i: (0, 0, 0)),
            _resident(w_out.shape, const2),
            pl.BlockSpec((1, D_MODEL), const2),
            pl.BlockSpec((1, D_MODEL), const2),
        ],
        out_specs=pl.BlockSpec((1, tq, D_MODEL), lambda b, i: (b, i, 0)),
        out_shape=jax.ShapeDtypeStruct((bsz, t, D_MODEL), F32),
        scratch_shapes=[
            pltpu.VMEM((1, rows), F32),
            pltpu.VMEM((VEXT, rows), F32),
            pltpu.VMEM((tq, B_WIDTH), BF16),
        ],
        compiler_params=pltpu.CompilerParams(
            dimension_semantics=("arbitrary", "arbitrary"), vmem_limit_bytes=VMEM_LIMIT),
        name="mla_attend_prompt",
    )(qt, kcat, kt, gate, x, w_uvt_h, w_out, ln_g, ln_b)


def _attn_sample_kernel(q_ref, cc_ref, ckr_ref, kn_ref, gate_ref, x_ref, w_uv_ref, w_out_ref,
                        lng_ref, lnb_ref, y_ref, m_ref, l_ref, acc_ref, o_ref, *, tq, tk):
    rows = B_HEADS * tq
    past = cc_ref.shape[1]
    q = q_ref[0].reshape(rows, KCAT)
    q_lat = q[:, :KV_LORA]
    q3 = q[:, KV_LORA:].astype(F32)
    q_rope = (q3[:, :ROPE_DIM] + q3[:, ROPE_DIM:]).astype(BF16)
    m_ref[...] = jnp.full((rows, 1), NEG_INF, F32)
    l_ref[...] = jnp.zeros((rows, 1), F32)
    acc_ref[...] = jnp.zeros((rows, KV_LORA), F32)

    for j in range(past // tk):
        cb = cc_ref[0, j * tk:(j + 1) * tk, :].astype(BF16)
        krb = ckr_ref[0, j * tk:(j + 1) * tk, :].astype(BF16)
        s = (lax.dot_general(q_lat, cb, _NT, preferred_element_type=F32)
             + lax.dot_general(q_rope, krb, _NT, preferred_element_type=F32))
        _softmax_step(s, cb, m_ref, l_ref, acc_ref)

    kn = kn_ref[0]
    s = lax.dot_general(q, kn, _NT, preferred_element_type=F32)
    _softmax_step(s, kn[:, :KV_LORA], m_ref, l_ref, acc_ref)

    y_ref[0] = _attn_output(acc_ref, l_ref, o_ref, gate_ref[0], x_ref[0],
                            w_uv_ref, w_out_ref, lng_ref, lnb_ref, tq)


def _attn_sample(qcat, cache_c, cache_kr, kcat_new, gate, x, w_uv_h, w_out, ln_g, ln_b, *, tk):
    bsz, _, tq, _ = qcat.shape
    past = cache_c.shape[1]
    rows = B_HEADS * tq
    const2 = lambda b: (0, 0)
    per_b = lambda shape: pl.BlockSpec((1,) + shape, lambda b: (b,) + (0,) * len(shape))
    kern = functools.partial(_attn_sample_kernel, tq=tq, tk=tk)
    return pl.pallas_call(
        kern,
        grid=(bsz,),
        in_specs=[
            per_b((B_HEADS, tq, KCAT)),
            per_b((past, KV_LORA)),
            per_b((past, ROPE_DIM)),
            per_b((tq, KCAT)),
            per_b((tq, B_WIDTH)),
            per_b((tq, D_MODEL)),
            pl.BlockSpec(w_uv_h.shape, lambda b: (0, 0, 0)),
            pl.BlockSpec(w_out.shape, const2),
            pl.BlockSpec((1, D_MODEL), const2),
            pl.BlockSpec((1, D_MODEL), const2),
        ],
        out_specs=per_b((tq, D_MODEL)),
        out_shape=jax.ShapeDtypeStruct((bsz, tq, D_MODEL), F32),
        scratch_shapes=[
            pltpu.VMEM((rows, 1), F32),
            pltpu.VMEM((rows, 1), F32),
            pltpu.VMEM((rows, KV_LORA), F32),
            pltpu.VMEM((tq, B_WIDTH), BF16),
        ],
        compiler_params=pltpu.CompilerParams(
            dimension_semantics=("arbitrary",), vmem_limit_bytes=VMEM_LIMIT),
        name="mla_attend_sample",
    )(qcat, cache_c, cache_kr, kcat_new, gate, x, w_uv_h, w_out, ln_g, ln_b)


def _rot_cols(w):
    half = ROPE_DIM // 2
    return jnp.concatenate([-w[..., half:], w[..., :half]], axis=-1)


def _rope_tables(pos):
    half = ROPE_DIM // 2
    inv = jnp.power(ROPE_BASE, -jnp.arange(half, dtype=F32) / half)
    ang = pos.astype(F32)[:, None] * inv[None, :]
    cos, sin = jnp.cos(ang), jnp.sin(ang)
    return jnp.concatenate([cos] * 4, axis=1), jnp.concatenate([sin] * 4, axis=1)


def kernel(x_prompt, x_sample, state_hgrn, cache_ckv, cache_krope, w_in_a, lb_gamma, a_norm_g, w_out_a,
           w_dkv, kv_norm_g, w_uk, w_uv, w_in_b, q_norm_g, w_uq, w_out_b, ln_g, ln_b):
    assert N_A == 1 and DEPTH == 2
    past = cache_ckv.shape[1]
    row = lambda a: a.reshape(1, -1)

    w_in_a_b = w_in_a[0].astype(BF16)
    w_out_a_b = w_out_a[0].astype(BF16)
    kr_w = w_dkv[:, KV_LORA:]
    w_dkv_ext = jnp.concatenate(
        [w_dkv[:, :KV_LORA], kr_w, kr_w, _rot_cols(kr_w), _rot_cols(kr_w)], axis=1).astype(BF16)
    w_uq3 = w_uq[0].reshape(Q_LORA, B_HEADS, NOPE_DIM + ROPE_DIM)
    q_rope_w = w_uq3[:, :, NOPE_DIM:]
    w_uq_ext = jnp.concatenate([
        w_uq3[:, :, :NOPE_DIM].reshape(Q_LORA, B_HEADS * NOPE_DIM),
        q_rope_w.reshape(Q_LORA, B_HEADS * ROPE_DIM),
        _rot_cols(q_rope_w).reshape(Q_LORA, B_HEADS * ROPE_DIM)], axis=1).astype(BF16)
    w_uqt = jnp.concatenate([
        w_uq3[:, :, :NOPE_DIM].reshape(Q_LORA, B_HEADS * NOPE_DIM),
        q_rope_w.reshape(Q_LORA, B_HEADS * ROPE_DIM)], axis=1).T.astype(BF16)
    w_uk_h = jnp.transpose(w_uk, (1, 0, 2)).astype(BF16)
    w_uvt_h = jnp.transpose(w_uv, (1, 2, 0)).astype(BF16)
    w_in_b_b = w_in_b[0].astype(BF16)
    w_out_b_b = w_out_b[0].astype(BF16)

    outs = []
    for x, s0, pos0, prompt in ((x_prompt, None, 0, True), (x_sample, state_hgrn[0], past, False)):
        bsz, t, _ = x.shape
        if prompt:
            bblk, tt, chunk, sub, tm = 1, HGRN_STEP_ROWS, CHUNK, HGRN_SUB_ROWS, TOKEN_TILE
        else:
            bblk, tt, chunk, sub, tm = bsz, t, t, bsz * t, t
        x1, s_fin = _hgrn_layer(x, s0, w_in_a_b, lb_gamma, row(a_norm_g[0]), w_out_a_b,
                                row(ln_g[0]), row(ln_b[0]), layer=0, bblk=bblk, tt=tt, chunk=chunk, sub=sub)
        cos2, sin2 = _rope_tables(pos0 + jnp.arange(t, dtype=jnp.int32))
        tabs = (cos2, sin2, cos2.T, sin2.T) if prompt else (cos2, sin2)
        proj_outs = _mla_proj(
            x1, tabs, w_dkv_ext, row(kv_norm_g), w_in_b_b, row(q_norm_g[0]),
            w_uqt if prompt else w_uq_ext, w_uk_h,
            bblk=bblk, tt=tm, transposed=prompt, tag="prompt" if prompt else "sample")
        if prompt:
            ckv, krope, kcat, kt, qt, gate = proj_outs
            y = _attn_prompt(qt, kcat, kt, gate, x1, w_uvt_h, w_out_b_b, row(ln_g[1]), row(ln_b[1]),
                             rc=ATTN_ROW_CHUNK)
        else:
            ckv, krope, kcat, qcat, gate = proj_outs
            y = _attn_sample(qcat, cache_ckv, cache_krope, kcat, gate, x1, w_uvt_h, w_out_b_b,
                             row(ln_g[1]), row(ln_b[1]), tk=SAMPLE_KEY_CHUNK)
        outs.append((y, s_fin[None], ckv, krope))
    (y_p, s_p, c_p, kr_p), (y_s, s_s, c_s, kr_s) = outs
    return (y_p, y_s, s_p, c_p, kr_p, s_s, c_s, kr_s)
```
